```python
import math
import jax, jax.numpy as jnp
from jax import lax
import numpy as np

D_MODEL = 1024
BATCH = 4
SEQ = 8192
DEPTH = 2

GRID_W = 64
CTX_LEN = 256
EPS = 1e-6
ROPE_BASE = 10000.0
QBLK = 128

GQA_HEADS = 4
GQA_KV_HEADS = 2
GQA_GROUP = GQA_HEADS // GQA_KV_HEADS
GQA_DIM = 64
NA_HEADS = 4
NA_DIM = 64
NA_WIN_ROWS = 8
NA_WIN_COLS = 16
DIFF_HEADS = 4
DIFF_QK_DIM = 32
DIFF_V_DIM = 64
MLA_HEADS = 4
MLA_NOPE = 64
MLA_ROPE = 32
MLA_V = 64
MLA_Q_LORA = 192
MLA_KV_LORA = 128

MIX_WIDTH = GQA_HEADS * GQA_DIM + NA_HEADS * NA_DIM + DIFF_HEADS * DIFF_V_DIM + MLA_HEADS * MLA_V
IN_SPLITS = (
    GQA_HEADS * GQA_DIM, GQA_KV_HEADS * GQA_DIM, GQA_KV_HEADS * GQA_DIM,
    NA_HEADS * NA_DIM, NA_HEADS * NA_DIM, NA_HEADS * NA_DIM,
    DIFF_HEADS * 2 * DIFF_QK_DIM, DIFF_HEADS * 2 * DIFF_QK_DIM, DIFF_HEADS * DIFF_V_DIM,
    MLA_Q_LORA, MLA_KV_LORA, MLA_ROPE,
)
IN_COLS = sum(IN_SPLITS)

N_EXPERTS = 16
MOE_GROUPS = 4
EXPERTS_PER_GROUP = N_EXPERTS // MOE_GROUPS
MOE_TOPK_GROUPS = 1
MOE_TOPK = 2
D_EXPERT = 256
D_SHARED = 256

kernel_name = 'hymba_style_hybrid_dit_block'


def rms_norm(x, g):
    xf = x.astype(jnp.float32)
    y = xf * lax.rsqrt(jnp.mean(xf * xf, axis=-1, keepdims=True) + EPS)
    return (y * g.astype(jnp.float32)).astype(x.dtype)


def modulate(h, shift, scale):
    return h * (1 + scale) + shift


def axial_angles(n_tok, rot_dim):
    half = rot_dim // 2
    freqs = ROPE_BASE ** (-jnp.arange(0, half, 2, dtype=jnp.float32) / half)
    t = jnp.arange(n_tok, dtype=jnp.int32)
    row = (t // GRID_W).astype(jnp.float32)
    col = (t % GRID_W).astype(jnp.float32)
    return row[:, None] * freqs, col[:, None] * freqs


def rope_1d(x, ang):
    shape = (1, ang.shape[0]) + (1,) * (x.ndim - 3) + (ang.shape[1],)
    cos = jnp.cos(ang).reshape(shape).astype(x.dtype)
    sin = jnp.sin(ang).reshape(shape).astype(x.dtype)
    x1, x2 = jnp.split(x, 2, axis=-1)
    return jnp.concatenate([x1 * cos - x2 * sin, x1 * sin + x2 * cos], axis=-1)


def rope_2d(x, ang):
    ang_r, ang_c = ang
    m = x.shape[-1] // 2
    return jnp.concatenate([rope_1d(x[..., :m], ang_r), rope_1d(x[..., m:], ang_c)], axis=-1)


def split_cols(u):
    offsets = [int(o) for o in np.cumsum(IN_SPLITS)[:-1]]
    return jnp.split(u, offsets, axis=-1)


def attend(q, kvs, scale):
    s = jnp.concatenate([jnp.einsum('bqhgd,bkhd->bhgqk', q, k) for k, _ in kvs], axis=-1)
    p = jax.nn.softmax(s.astype(jnp.float32) * scale, axis=-1)
    out = 0
    start = 0
    for k, v in kvs:
        n = k.shape[1]
        out = out + jnp.einsum('bhgqk,bkhd->bqhgd', p[..., start:start + n].astype(v.dtype), v)
        start += n
    return out


def blocked_attend(q, kvs, scale):
    b, s = q.shape[:2]
    nb = s // QBLK
    qb = jnp.moveaxis(q.reshape((b, nb, QBLK) + q.shape[2:]), 1, 0)
    ob = lax.map(lambda qi: attend(qi, kvs, scale), qb)
    return jnp.moveaxis(ob, 0, 1).reshape((b, s) + ob.shape[3:])


def gqa_mixer(q, k, v, qc, kc, vc, q_gain, k_gain, ang, want_ctx):
    scale = GQA_DIM ** -0.5
    def heads(qx, kx, vx, rotate):
        b, n = qx.shape[:2]
        qx = rms_norm(qx.reshape(b, n, GQA_KV_HEADS, GQA_GROUP, GQA_DIM), q_gain)
        kx = rms_norm(kx.reshape(b, n, GQA_KV_HEADS, GQA_DIM), k_gain)
        vx = vx.reshape(b, n, GQA_KV_HEADS, GQA_DIM)
        if rotate:
            qx = rope_2d(qx, ang)
            kx = rope_2d(kx, ang)
        return qx, kx, vx
    ql, kl, vl = heads(q, k, v, True)
    qcx, kcx, vcx = heads(qc, kc, vc, False)
    b, s = q.shape[:2]
    out_lat = blocked_attend(ql, ((kl, vl), (kcx, vcx)), scale).reshape(b, s, GQA_HEADS * GQA_DIM)
    out_ctx = None
    if want_ctx:
        out_ctx = attend(qcx, ((kcx, vcx),), scale).reshape(b, qc.shape[1], GQA_HEADS * GQA_DIM)
    return out_lat, out_ctx


def na_mixer(q, k, v, qc, kc, vc, rel_bias, want_ctx):
    scale = NA_DIM ** -0.5
    b, s = q.shape[:2]
    L = qc.shape[1]
    rows = s // GRID_W
    kh = min(NA_WIN_ROWS, rows)
    kw = NA_WIN_COLS
    n_nb = kh * kw
    qg = q.reshape(b, rows, GRID_W, NA_HEADS, NA_DIM)
    kg = k.reshape(b, rows, GRID_W, NA_HEADS, NA_DIM)
    vg = v.reshape(b, rows, GRID_W, NA_HEADS, NA_DIM)
    qcx = qc.reshape(b, L, NA_HEADS, NA_DIM)
    kcx = kc.reshape(b, L, NA_HEADS, NA_DIM)
    vcx = vc.reshape(b, L, NA_HEADS, NA_DIM)
    cols = jnp.arange(GRID_W)
    col_idx = jnp.clip(cols - kw // 2, 0, GRID_W - kw)[:, None] + jnp.arange(kw)[None, :]
    dc = col_idx - cols[:, None]

    def row_block(r):
        r0 = jnp.clip(r - kh // 2, 0, rows - kh)
        k_rows = lax.dynamic_slice_in_dim(kg, r0, kh, axis=1)
        v_rows = lax.dynamic_slice_in_dim(vg, r0, kh, axis=1)
        k_nb = jnp.moveaxis(k_rows[:, :, col_idx], 1, 2).reshape(b, GRID_W, n_nb, NA_HEADS, NA_DIM)
        v_nb = jnp.moveaxis(v_rows[:, :, col_idx], 1, 2).reshape(b, GRID_W, n_nb, NA_HEADS, NA_DIM)
        q_row = lax.dynamic_index_in_dim(qg, r, axis=1, keepdims=False)
        dr = r0 + jnp.arange(kh) - r
        bias = rel_bias[:, dr[:, None, None] + (NA_WIN_ROWS - 1), dc[None] + (NA_WIN_COLS - 1)]
        bias = jnp.moveaxis(bias, 1, 2).reshape(NA_HEADS, GRID_W, n_nb)
        s_nb = jnp.einsum('bqhd,bqkhd->bhqk', q_row, k_nb).astype(jnp.float32) * scale + bias.astype(jnp.float32)
        s_c = jnp.einsum('bqhd,bkhd->bhqk', q_row, kcx).astype(jnp.float32) * scale
        p = jax.nn.softmax(jnp.concatenate([s_nb, s_c], axis=-1), axis=-1)
        return (jnp.einsum('bhqk,bqkhd->bqhd', p[..., :n_nb].astype(v.dtype), v_nb)
                + jnp.einsum('bhqk,bkhd->bqhd', p[..., n_nb:].astype(v.dtype), vcx))

    o = lax.map(row_block, jnp.arange(rows))
    out_lat = jnp.moveaxis(o, 0, 1).reshape(b, s, NA_HEADS * NA_DIM)
    out_ctx = None
    if want_ctx:
        out_ctx = attend(qcx[:, :, :, None], ((kcx, vcx),), scale).reshape(b, L, NA_HEADS * NA_DIM)
    return out_lat, out_ctx


def diff_mixer(q, k, v, qc, kc, vc, lq1, lk1, lq2, lk2, subln, lam_init, ang, want_ctx):
    scale = DIFF_QK_DIM ** -0.5
    lam = (jnp.exp(jnp.sum(lq1.astype(jnp.float32) * lk1.astype(jnp.float32)))
           - jnp.exp(jnp.sum(lq2.astype(jnp.float32) * lk2.astype(jnp.float32))) + lam_init).astype(q.dtype)
    def heads(qx, kx, vx, rotate):
        b, n = qx.shape[:2]
        qx = qx.reshape(b, n, DIFF_HEADS * 2, DIFF_QK_DIM)
        kx = kx.reshape(b, n, DIFF_HEADS * 2, DIFF_QK_DIM)
        if rotate:
            qx = rope_2d(qx, ang)
            kx = rope_2d(kx, ang)
        return (qx.reshape(b, n, DIFF_HEADS, 2, DIFF_QK_DIM), kx.reshape(b, n, DIFF_HEADS, 2, DIFF_QK_DIM),
                vx.reshape(b, n, DIFF_HEADS, DIFF_V_DIM))
    def combine(o1, o2):
        o = rms_norm(o1 - lam * o2, subln) * (1.0 - lam_init)
        return o.reshape(o.shape[0], o.shape[1], DIFF_HEADS * DIFF_V_DIM)
    ql, kl, vl = heads(q, k, v, True)
    qcx, kcx, vcx = heads(qc, kc, vc, False)
    o1 = blocked_attend(ql[:, :, :, 0:1], ((kl[:, :, :, 0], vl), (kcx[:, :, :, 0], vcx)), scale)
    o2 = blocked_attend(ql[:, :, :, 1:2], ((kl[:, :, :, 1], vl), (kcx[:, :, :, 1], vcx)), scale)
    out_lat = combine(o1, o2)
    out_ctx = None
    if want_ctx:
        c1 = attend(qcx[:, :, :, 0:1], ((kcx[:, :, :, 0], vcx),), scale)
        c2 = attend(qcx[:, :, :, 1:2], ((kcx[:, :, :, 1], vcx),), scale)
        out_ctx = combine(c1, c2)
    return out_lat, out_ctx


def mla_mixer(cq, ckv, kpe, cqc, ckvc, kpec, q_norm, w_q_up, kv_norm, w_kv_up, ang, want_ctx):
    scale = (MLA_NOPE + MLA_ROPE) ** -0.5
    def heads(cqx, ckvx, kpex, rotate):
        b, n = cqx.shape[:2]
        qx = (rms_norm(cqx, q_norm) @ w_q_up).reshape(b, n, MLA_HEADS, MLA_NOPE + MLA_ROPE)
        kvx = (rms_norm(ckvx, kv_norm) @ w_kv_up).reshape(b, n, MLA_HEADS, MLA_NOPE + MLA_V)
        q_nope, q_pe = qx[..., :MLA_NOPE], qx[..., MLA_NOPE:]
        k_nope, vx = kvx[..., :MLA_NOPE], kvx[..., MLA_NOPE:]
        k_pe = kpex[:, :, None, :]
        if rotate:
            q_pe = rope_2d(q_pe, ang)
            k_pe = rope_2d(k_pe, ang)
        qx = jnp.concatenate([q_nope, q_pe], axis=-1)
        kx = jnp.concatenate([k_nope, jnp.broadcast_to(k_pe, k_nope.shape[:-1] + (MLA_ROPE,))], axis=-1)
        return qx[:, :, :, None], kx, vx
    ql, kl, vl = heads(cq, ckv, kpe, True)
    qcx, kcx, vcx = heads(cqc, ckvc, kpec, False)
    b, s = cq.shape[:2]
    out_lat = blocked_attend(ql, ((kl, vl), (kcx, vcx)), scale).reshape(b, s, MLA_HEADS * MLA_V)
    out_ctx = None
    if want_ctx:
        out_ctx = attend(qcx, ((kcx, vcx),), scale).reshape(b, cqc.shape[1], MLA_HEADS * MLA_V)
    return out_lat, out_ctx


def swiglu(x, wg, wu, wd):
    return (jax.nn.silu(x @ wg) * (x @ wu)) @ wd


def moe_ffn(h, router_w, router_b, w_gate, w_up, w_down, ws_gate, ws_up, ws_down):
    shp = h.shape
    xs = h.reshape(-1, shp[-1])
    scores = jax.nn.sigmoid((xs @ router_w).astype(jnp.float32))
    sel = scores + router_b.astype(jnp.float32)
    grp = sel.reshape(-1, MOE_GROUPS, EXPERTS_PER_GROUP)
    grp_score = lax.top_k(grp, 2)[0].sum(-1)
    _, g_idx = lax.top_k(grp_score, MOE_TOPK_GROUPS)
    g_mask = (g_idx[..., None] == jnp.arange(MOE_GROUPS)).any(axis=1)
    e_mask = jnp.repeat(g_mask, EXPERTS_PER_GROUP, axis=1)
    _, e_idx = lax.top_k(jnp.where(e_mask, sel, -jnp.inf), MOE_TOPK)
    w = jnp.take_along_axis(scores, e_idx, axis=-1)
    w = w / jnp.sum(w, axis=-1, keepdims=True)
    gates = jnp.sum(jax.nn.one_hot(e_idx, N_EXPERTS, dtype=jnp.float32) * w[..., None], axis=1).astype(xs.dtype)
    out = swiglu(xs, ws_gate, ws_up, ws_down)
    for e in range(N_EXPERTS):
        out = out + gates[:, e:e + 1] * swiglu(xs, w_gate[e], w_up[e], w_down[e])
    return out.reshape(shp)


def setup_inputs(seed: int = 0) -> dict:
    key = jax.random.key(seed)
    ks = list(jax.random.split(key, 31))
    def nrm(shape, scale):
        return jax.random.normal(ks.pop(), shape, jnp.float32) * scale
    def gain(shape):
        return 1.0 + nrm(shape, 0.1)
    D = D_MODEL
    return {
        'x': nrm((BATCH, SEQ, D), 1.0),
        'c': nrm((BATCH, D), 1.0),
        'ctx': nrm((BATCH, CTX_LEN, D), 1.0),
        'c_ctx': nrm((D,), 1.0),
        'w_mod': nrm((DEPTH, D, 6 * D), 0.5 * D ** -0.5),
        'b_mod': nrm((DEPTH, 6 * D), 0.01),
        'norm_mix': gain((DEPTH, D)),
        'norm_ffn': gain((DEPTH, D)),
        'w_in': nrm((DEPTH, D, IN_COLS), D ** -0.5),
        'w_out': nrm((DEPTH, MIX_WIDTH, D), MIX_WIDTH ** -0.5),
        'gqa_q_norm': gain((DEPTH, GQA_DIM)),
        'gqa_k_norm': gain((DEPTH, GQA_DIM)),
        'na_rel_bias': nrm((DEPTH, NA_HEADS, 2 * NA_WIN_ROWS - 1, 2 * NA_WIN_COLS - 1), 0.1),
        'diff_lq1': nrm((DEPTH, DIFF_QK_DIM), 0.1),
        'diff_lk1': nrm((DEPTH, DIFF_QK_DIM), 0.1),
        'diff_lq2': nrm((DEPTH, DIFF_QK_DIM), 0.1),
        'diff_lk2': nrm((DEPTH, DIFF_QK_DIM), 0.1),
        'diff_subln': gain((DEPTH, DIFF_V_DIM)),
        'mla_q_norm': gain((DEPTH, MLA_Q_LORA)),
        'mla_w_q_up': nrm((DEPTH, MLA_Q_LORA, MLA_HEADS * (MLA_NOPE + MLA_ROPE)), MLA_Q_LORA ** -0.5),
        'mla_kv_norm': gain((DEPTH, MLA_KV_LORA)),
        'mla_w_kv_up': nrm((DEPTH, MLA_KV_LORA, MLA_HEADS * (MLA_NOPE + MLA_V)), MLA_KV_LORA ** -0.5),
        'router_w': nrm((D, N_EXPERTS), D ** -0.5),
        'router_b': nrm((N_EXPERTS,), 0.01),
        'moe_w_gate': nrm((DEPTH, N_EXPERTS, D, D_EXPERT), D ** -0.5),
        'moe_w_up': nrm((DEPTH, N_EXPERTS, D, D_EXPERT), D ** -0.5),
        'moe_w_down': nrm((DEPTH, N_EXPERTS, D_EXPERT, D), D_EXPERT ** -0.5),
        'shared_w_gate': nrm((DEPTH, D, D_SHARED), D ** -0.5),
        'shared_w_up': nrm((DEPTH, D, D_SHARED), D ** -0.5),
        'shared_w_down': nrm((DEPTH, D_SHARED, D), D_SHARED ** -0.5),
        'final_norm': gain((D,)),
    }


def reference(x, c, ctx, c_ctx, w_mod, b_mod, norm_mix, norm_ffn, w_in, w_out,
              gqa_q_norm, gqa_k_norm, na_rel_bias, diff_lq1, diff_lk1, diff_lq2, diff_lk2, diff_subln,
              mla_q_norm, mla_w_q_up, mla_kv_norm, mla_w_kv_up, router_w, router_b,
              moe_w_gate, moe_w_up, moe_w_down, shared_w_gate, shared_w_up, shared_w_down, final_norm):
    s = x.shape[1]
    ang_gqa = axial_angles(s, GQA_DIM)
    ang_diff = axial_angles(s, DIFF_QK_DIM)
    ang_mla = axial_angles(s, MLA_ROPE)
    xc = ctx
    for l in range(DEPTH):
        want_ctx = l < DEPTH - 1
        lam_init = 0.8 - 0.6 * math.exp(-0.3 * l)
        mod = jax.nn.silu(c) @ w_mod[l] + b_mod[l]
        mod_c = jax.nn.silu(c_ctx) @ w_mod[l] + b_mod[l]
        sh1, sc1, g1, sh2, sc2, g2 = jnp.split(mod[:, None, :], 6, axis=-1)
        sh1c, sc1c, g1c, sh2c, sc2c, g2c = jnp.split(mod_c, 6, axis=-1)

        h = modulate(rms_norm(x, norm_mix[l]), sh1, sc1)
        hc = modulate(rms_norm(xc, norm_mix[l]), sh1c, sc1c)
        u = split_cols(h @ w_in[l])
        uc = split_cols(hc @ w_in[l])
        a_lat, a_ctx = gqa_mixer(u[0], u[1], u[2], uc[0], uc[1], uc[2], gqa_q_norm[l], gqa_k_norm[l], ang_gqa, want_ctx)
        b_lat, b_ctx = na_mixer(u[3], u[4], u[5], uc[3], uc[4], uc[5], na_rel_bias[l], want_ctx)
        c_lat, c_ctx_out = diff_mixer(u[6], u[7], u[8], uc[6], uc[7], uc[8], diff_lq1[l], diff_lk1[l],
                                      diff_lq2[l], diff_lk2[l], diff_subln[l], lam_init, ang_diff, want_ctx)
        d_lat, d_ctx = mla_mixer(u[9], u[10], u[11], uc[9], uc[10], uc[11], mla_q_norm[l], mla_w_q_up[l],
                                 mla_kv_norm[l], mla_w_kv_up[l], ang_mla, want_ctx)
        x = x + g1 * (jnp.concatenate([a_lat, b_lat, c_lat, d_lat], axis=-1) @ w_out[l])

        h2 = modulate(rms_norm(x, norm_ffn[l]), sh2, sc2)
        x = x + g2 * moe_ffn(h2, router_w, router_b, moe_w_gate[l], moe_w_up[l], moe_w_down[l],
                             shared_w_gate[l], shared_w_up[l], shared_w_down[l])

        if want_ctx:
            xc = xc + g1c * (jnp.concatenate([a_ctx, b_ctx, c_ctx_out, d_ctx], axis=-1) @ w_out[l])
            h2c = modulate(rms_norm(xc, norm_ffn[l]), sh2c, sc2c)
            xc = xc + g2c * moe_ffn(h2c, router_w, router_b, moe_w_gate[l], moe_w_up[l], moe_w_down[l],
                                    shared_w_gate[l], shared_w_up[l], shared_w_down[l])
    return rms_norm(x, final_norm)
```

```python
import functools
import math

import numpy as np
import jax
import jax.numpy as jnp
from jax import lax
from jax.experimental import pallas as pl
from jax.experimental.pallas import tpu as pltpu

F32 = jnp.float32
BF16 = jnp.bfloat16

GRID_W = 64
EPS = 1e-6
ROPE_BASE = 10000.0
GQA_HEADS, GQA_KV_HEADS, GQA_DIM = 4, 2, 64
NA_HEADS, NA_DIM, NA_WIN_ROWS, NA_WIN_COLS = 4, 64, 8, 16
DIFF_HEADS, DIFF_QK_DIM, DIFF_V_DIM = 4, 32, 64
MLA_HEADS, MLA_NOPE, MLA_ROPE, MLA_V, MLA_Q_LORA, MLA_KV_LORA = 4, 64, 32, 64, 192, 128
N_EXPERTS, MOE_GROUPS, D_EXPERT = 16, 4, 256
EXPERTS_PER_GROUP = N_EXPERTS // MOE_GROUPS

LANES = 128
SUBLANES = 8
VMEM_LIMIT = 56 * 1024 * 1024

TOK_TILE = 512
Q_TILE = 512
MOE_TILE = 1024
MOD_TILE = 1024
NEG = -1e30

SEG_W = 256
OFF_AQ, OFF_AK, OFF_AV = 0, 256, 512
OFF_BQ, OFF_BK, OFF_BV = 768, 1024, 1280
OFF_CQ, OFF_CK, OFF_CV = 1536, 1792, 2048
OFF_DQ, OFF_DKV, OFF_DPE = 2304, 2560, 2688
IN_PACKED = 2816


def _cparams(sem):
    return pltpu.CompilerParams(dimension_semantics=sem, vmem_limit_bytes=VMEM_LIMIT)


def _lane_iota(shape):
    return lax.broadcasted_iota(jnp.int32, shape, 1)


def _lane_group(shape, width):
    return lax.shift_right_logical(_lane_iota(shape), int(math.log2(width)))


def _sigmoid(x):
    return 1.0 / (1.0 + jnp.exp(-x))


def _mod_kernel(c_ref, w_ref, b_ref, o_ref):
    c = c_ref[...]
    s = c * _sigmoid(c)
    o_ref[0] = jnp.dot(s, w_ref[0], preferred_element_type=F32,
                       precision=lax.Precision.HIGHEST) + b_ref[0]


def _modulation(cc, w_mod, b_mod):
    depth, d, n = w_mod.shape
    rows = cc.shape[0]
    return pl.pallas_call(
        _mod_kernel,
        grid=(depth, n // MOD_TILE),
        in_specs=[
            pl.BlockSpec((rows, d), lambda l, j: (0, 0)),
            pl.BlockSpec((1, d, MOD_TILE), lambda l, j: (l, 0, j)),
            pl.BlockSpec((1, 1, MOD_TILE), lambda l, j: (l, 0, j)),
        ],
        out_specs=pl.BlockSpec((1, rows, MOD_TILE), lambda l, j: (l, 0, j)),
        out_shape=jax.ShapeDtypeStruct((depth, rows, n), F32),
        compiler_params=_cparams(("parallel", "parallel")),
        name="modulation",
    )(cc, w_mod, b_mod.reshape(depth, 1, n))


def _swap_halves(x, half):
    w = x.shape[1]
    lane = _lane_iota(x.shape)
    fwd = pltpu.roll(x, half, 1)
    bwd = pltpu.roll(x, w - half, 1)
    return jnp.where((lane & (2 * half - 1)) < half, bwd, fwd)


def _rope(x, cos, sin, half):
    reps = x.shape[1] // cos.shape[1]
    c = jnp.concatenate([cos] * reps, axis=1) if reps > 1 else cos
    s = jnp.concatenate([sin] * reps, axis=1) if reps > 1 else sin
    return x * c + _swap_halves(x, half) * s


def _group_rms(x, group, count):
    gid = _lane_group(x.shape, group)
    x2 = x * x
    inv = jnp.zeros_like(x)
    for g in range(x.shape[1] // group):
        msk = gid == g
        ms = jnp.sum(jnp.where(msk, x2, 0.0), axis=1, keepdims=True) * (1.0 / count)
        inv = jnp.where(msk, lax.rsqrt(ms + EPS), inv)
    return x * inv


def _proj_kernel(*refs, rope):
    (x_ref, sh_ref, sc_ref, gmix_ref, w_ref, wqu_ref, wkk_ref, wkv_ref,
     gq_ref, gk_ref, gcq_ref, gckv_ref) = refs[:12]
    pos = 12
    if rope:
        cg_ref, sg_ref, cd_ref, sd_ref, cm_ref, sm_ref = refs[pos:pos + 6]
        pos += 6
    (qa_ref, kta_ref, va_ref, qb_ref, ktb_ref, vb_ref,
     qc_ref, ktc_ref, vc_ref, qd_ref, ktd_ref, vd_ref) = refs[pos:pos + 12]

    x = x_ref[0]
    ms = jnp.mean(x * x, axis=-1, keepdims=True)
    h = x * lax.rsqrt(ms + EPS) * gmix_ref[...]
    h = h * (1.0 + sc_ref[0]) + sh_ref[0]
    hb = h.astype(BF16)

    def seg(off, width=SEG_W):
        return jnp.dot(hb, w_ref[:, off:off + width], preferred_element_type=F32)

    q = _group_rms(seg(OFF_AQ), GQA_DIM, GQA_DIM) * gq_ref[...]
    k = _group_rms(seg(OFF_AK), GQA_DIM, GQA_DIM) * gk_ref[...]
    if rope:
        q = _rope(q, cg_ref[...], sg_ref[...], GQA_DIM // 4)
        k = _rope(k, cg_ref[...], sg_ref[...], GQA_DIM // 4)
    qa_ref[0] = q.astype(BF16)
    kta_ref[0, 0] = k.T.astype(BF16)
    va_ref[0] = seg(OFF_AV).astype(BF16)

    qb_ref[0] = (seg(OFF_BQ) * (NA_DIM ** -0.5)).astype(BF16)
    ktb_ref[0, 0] = seg(OFF_BK).T.astype(BF16)
    vb_ref[0] = seg(OFF_BV).astype(BF16)

    q = seg(OFF_CQ) * (DIFF_QK_DIM ** -0.5)
    k = seg(OFF_CK)
    if rope:
        q = _rope(q, cd_ref[...], sd_ref[...], DIFF_QK_DIM // 4)
        k = _rope(k, cd_ref[...], sd_ref[...], DIFF_QK_DIM // 4)
    qc_ref[0] = q.astype(BF16)
    ktc_ref[0, 0] = k.T.astype(BF16)
    vc_ref[0] = seg(OFF_CV).astype(BF16)

    cq = _group_rms(seg(OFF_DQ), SEG_W, MLA_Q_LORA) * gcq_ref[...]
    q = jnp.dot(cq.astype(BF16), wqu_ref[...], preferred_element_type=F32)
    q = q * ((MLA_NOPE + MLA_ROPE) ** -0.5)
    ckv = _group_rms(seg(OFF_DKV, LANES), LANES, MLA_KV_LORA) * gckv_ref[...]
    ckvb = ckv.astype(BF16)
    k = jnp.dot(ckvb, wkk_ref[...], preferred_element_type=F32)
    kpe = seg(OFF_DPE, LANES)
    if rope:
        q = _rope(q, cm_ref[...], sm_ref[...], MLA_ROPE // 4)
        kpe = _rope(kpe, cm_ref[...], sm_ref[...], MLA_ROPE // 4)
    k = k + jnp.concatenate([kpe] * MLA_HEADS, axis=1)
    qd_ref[0] = q.astype(BF16)
    ktd_ref[0, 0] = k.T.astype(BF16)
    vd_ref[0] = jnp.dot(ckvb, wkv_ref[...], preferred_element_type=F32).astype(BF16)


def _project(x, sh, sc, row_of_batch, pw, tables, tm):
    b, n, d = x.shape
    nt = n // tm
    rope = tables is not None
    full = lambda a: pl.BlockSpec(a.shape, lambda bi, i: (0,) * a.ndim)
    mod_spec = pl.BlockSpec((1, 1, d), lambda bi, i: (row_of_batch(bi), 0, 0))
    ins = [x, sh, sc, pw["gmix"], pw["w_in"], pw["wq_up"], pw["wkv_k"], pw["wkv_v"],
           pw["gq"], pw["gk"], pw["gcq"], pw["gckv"]]
    in_specs = [pl.BlockSpec((1, tm, d), lambda bi, i: (bi, i, 0)), mod_spec, mod_spec]
    in_specs += [full(a) for a in ins[3:]]
    if rope:
        ins += list(tables)
        in_specs += [pl.BlockSpec((tm, LANES), lambda bi, i: (i, 0)) for _ in tables]

    def q_out(w):
        return (jax.ShapeDtypeStruct((b, n, w), BF16),
                pl.BlockSpec((1, tm, w), lambda bi, i: (bi, i, 0)))

    def kt_out(w):
        return (jax.ShapeDtypeStruct((b, nt, w, tm), BF16),
                pl.BlockSpec((1, 1, w, tm), lambda bi, i: (bi, i, 0, 0)))

    outs = []
    for qw in (SEG_W, SEG_W, SEG_W, 2 * SEG_W):
        outs += [q_out(qw), kt_out(qw), q_out(SEG_W)]
    return pl.pallas_call(
        functools.partial(_proj_kernel, rope=rope),
        grid=(b, nt),
        in_specs=in_specs,
        out_specs=[o[1] for o in outs],
        out_shape=[o[0] for o in outs],
        compiler_params=_cparams(("parallel", "parallel")),
        name="proj_rope" if rope else "proj_ctx",
    )(*ins)


def _tile_lanes(x, width):
    reps = width // x.shape[1]
    return jnp.tile(x, (1, reps)) if reps > 1 else x


def _flash_kernel(*refs, kind, has_latent, lam_init):
    q_ref, kc_ref, vc_ref = refs[:3]
    pos = 3
    if has_latent:
        kt_ref, v_ref = refs[pos:pos + 2]
        pos += 2
    if kind == "diff":
        lq1_ref, lk1_ref, lq2_ref, lk2_ref, sub_ref = refs[pos:pos + 5]
        pos += 5
    o_ref = refs[pos]
    qs_ref, m_ref, l_ref, acc_ref = refs[pos + 1:pos + 5]

    tq = q_ref.shape[1]
    nm = qs_ref.shape[0]
    lane = _lane_iota((tq, LANES))

    for m in range(nm):
        if kind == "mla":
            qs_ref[m] = q_ref[0, :, m * LANES:(m + 1) * LANES]
        else:
            qv = q_ref[0]
            qs_ref[m] = jnp.where(_lane_group(qv.shape, LANES // nm) == m, qv, jnp.zeros_like(qv))

    def k_rows(m):
        return slice(m * LANES, (m + 1) * LANES) if kind == "mla" else slice(None)

    for m in range(nm):
        s = jnp.dot(qs_ref[m], kc_ref[0, 0, k_rows(m), :], preferred_element_type=F32)
        mx = jnp.max(s, axis=1, keepdims=True)
        p = jnp.exp(s - mx)
        m_ref[m] = jnp.broadcast_to(mx, (tq, LANES))
        l_ref[m] = jnp.broadcast_to(jnp.sum(p, axis=1, keepdims=True), (tq, LANES))
        acc_ref[m] = jnp.dot(p.astype(BF16), vc_ref[0], preferred_element_type=F32)

    if has_latent:
        nk, tk = kt_ref.shape[1], kt_ref.shape[3]

        def chunk(j, carry):
            vv = v_ref[0, pl.ds(pl.multiple_of(j * tk, tk), tk), :]
            for m in range(nm):
                s = jnp.dot(qs_ref[m], kt_ref[0, j, k_rows(m), :], preferred_element_type=F32)
                m_prev = m_ref[m]
                m_new = jnp.maximum(m_prev, jnp.max(s, axis=1, keepdims=True))
                alpha = jnp.exp(m_prev - m_new)
                p = jnp.exp(s - _tile_lanes(m_new, tk))
                l_ref[m] = alpha * l_ref[m] + jnp.sum(p, axis=1, keepdims=True)
                acc_ref[m] = alpha * acc_ref[m] + jnp.dot(
                    p.astype(BF16), vv, preferred_element_type=F32)
                m_ref[m] = m_new
            return carry

        lax.fori_loop(0, nk, chunk, 0)

    outs = [acc_ref[m] / l_ref[m] for m in range(nm)]
    low = lane < (LANES // 2)
    if kind == "diff":
        lam = (jnp.exp(jnp.sum(lq1_ref[...] * lk1_ref[...], axis=1, keepdims=True))
               - jnp.exp(jnp.sum(lq2_ref[...] * lk2_ref[...], axis=1, keepdims=True))
               + lam_init)
        dlt = jnp.where(low, outs[0] - lam * outs[1], outs[2] - lam * outs[3])
        o = _group_rms(dlt, DIFF_V_DIM, DIFF_V_DIM) * sub_ref[...] * (1.0 - lam_init)
    else:
        o = jnp.where(low, outs[0], outs[1])
    o_ref[0] = o.astype(o_ref.dtype)


def _flash(q, ktc, vc, kt, v, kind, extras=(), lam_init=0.0, tq=Q_TILE):
    b, n, wtot = q.shape
    wq = 2 * LANES if kind == "mla" else LANES
    npair = wtot // wq
    nm = 4 if kind == "diff" else 2
    c = vc.shape[1]
    has_latent = kt is not None
    tq = min(tq, n)
    ins = [q, ktc, vc]
    in_specs = [
        pl.BlockSpec((1, tq, wq), lambda bi, p, i: (bi, i, p)),
        pl.BlockSpec((1, 1, wq, c), lambda bi, p, i: (bi, 0, p, 0)),
        pl.BlockSpec((1, c, LANES), lambda bi, p, i: (bi, 0, p)),
    ]
    if has_latent:
        nk, tk = kt.shape[1], kt.shape[3]
        ins += [kt, v]
        in_specs += [
            pl.BlockSpec((1, nk, wq, tk), lambda bi, p, i: (bi, 0, p, 0)),
            pl.BlockSpec((1, v.shape[1], LANES), lambda bi, p, i: (bi, 0, p)),
        ]
    for e in extras:
        ins.append(e)
        in_specs.append(pl.BlockSpec(e.shape, lambda bi, p, i: (0, 0)))
    return pl.pallas_call(
        functools.partial(_flash_kernel, kind=kind, has_latent=has_latent, lam_init=lam_init),
        grid=(b, npair, n // tq),
        in_specs=in_specs,
        out_specs=pl.BlockSpec((1, tq, LANES), lambda bi, p, i: (bi, i, p)),
        out_shape=jax.ShapeDtypeStruct((b, n, npair * LANES), BF16),
        scratch_shapes=[
            pltpu.VMEM((nm, tq, LANES), BF16),
            pltpu.VMEM((nm, tq, LANES), F32),
            pltpu.VMEM((nm, tq, LANES), F32),
            pltpu.VMEM((nm, tq, LANES), F32),
        ],
        compiler_params=_cparams(("parallel", "parallel", "parallel")),
        name="flash_" + kind + ("" if has_latent else "_ctx"),
    )(*ins)


NA_CHUNKS = 3


def _na_kernel(q_ref, kt_ref, v_ref, kc_ref, vc_ref, bias_ref, o_ref):
    tq = q_ref.shape[1]
    nk, tk = kt_ref.shape[1], kt_ref.shape[3]
    i = pl.program_id(2)
    c0 = jnp.clip(i - 1, 0, nk - NA_CHUNKS)
    lane = _lane_iota((tq, LANES))
    qv = q_ref[0]
    outs = []
    for m in range(2):
        qm = jnp.where(_lane_group(qv.shape, LANES // 2) == m, qv, jnp.zeros_like(qv))
        ss = [jnp.dot(qm, kt_ref[0, c0 + j], preferred_element_type=F32)
              + bias_ref[0, m, :, j * tk:(j + 1) * tk] for j in range(NA_CHUNKS)]
        sc = jnp.dot(qm, kc_ref[0, 0], preferred_element_type=F32)
        mx = jnp.max(sc, axis=1, keepdims=True)
        for s in ss:
            mx = jnp.maximum(mx, jnp.max(s, axis=1, keepdims=True))
        pc = jnp.exp(sc - mx)
        l = jnp.sum(pc, axis=1, keepdims=True)
        acc = jnp.dot(pc.astype(BF16), vc_ref[0], preferred_element_type=F32)
        for j, s in enumerate(ss):
            p = jnp.exp(s - mx)
            l = l + jnp.sum(p, axis=1, keepdims=True)
            vv = v_ref[0, pl.ds(pl.multiple_of((c0 + j) * tk, tk), tk), :]
            acc = acc + jnp.dot(p.astype(BF16), vv, preferred_element_type=F32)
        outs.append(acc / l)
    o_ref[0] = jnp.where(lane < LANES // 2, outs[0], outs[1]).astype(o_ref.dtype)


def _na_bias_tiles(rel_bias, rows, tk):
    qrows = Q_TILE // GRID_W
    nblk = rows // qrows
    nk = rows * GRID_W // tk
    krows = NA_CHUNKS * tk // GRID_W
    kh = min(NA_WIN_ROWS, rows)
    kw = NA_WIN_COLS
    blocks = np.array([0, 1, nblk - 1])
    c0 = np.clip(blocks - 1, 0, nk - NA_CHUNKS)
    r = blocks[:, None] * qrows + np.arange(qrows)[None, :]
    kr = c0[:, None] * (tk // GRID_W) + np.arange(krows)[None, :]
    r0 = np.clip(r - kh // 2, 0, rows - kh)
    dr = kr[:, None, :] - r[:, :, None]
    valid_r = (kr[:, None, :] >= r0[:, :, None]) & (kr[:, None, :] < r0[:, :, None] + kh)
    cols = np.arange(GRID_W)
    cstart = np.clip(cols - kw // 2, 0, GRID_W - kw)
    dc = cols[None, :] - cols[:, None]
    valid_c = (cols[None, :] >= cstart[:, None]) & (cols[None, :] < cstart[:, None] + kw)
    ndc = 2 * NA_WIN_COLS - 1
    onehot = ((dc[None] + NA_WIN_COLS - 1) == np.arange(ndc)[:, None, None]) & valid_c[None]
    dr_idx = np.clip(dr + NA_WIN_ROWS - 1, 0, 2 * NA_WIN_ROWS - 2)
    g1 = rel_bias[:, dr_idx, :]
    tiles = jnp.einsum("htabd,dcx->thacbx", g1, jnp.asarray(onehot, F32),
                       precision=lax.Precision.HIGHEST)
    valid = valid_r[:, None, :, None, :, None] & valid_c[None, None, None, :, None, :]
    tiles = jnp.where(jnp.asarray(valid), tiles, NEG)
    return tiles.reshape(3, rel_bias.shape[0], qrows * GRID_W, krows * GRID_W)


def _na_attention(q, ktc, vc, kt, v, bias):
    b, n, wtot = q.shape
    npair = wtot // LANES
    nk, tk = kt.shape[1], kt.shape[3]
    c = vc.shape[1]
    nblk = n // Q_TILE

    def bias_map(bi, p, i):
        return (jnp.where(i == 0, 0, jnp.where(i == nblk - 1, 2, 1)), p, 0, 0)

    return pl.pallas_call(
        _na_kernel,
        grid=(b, npair, nblk),
        in_specs=[
            pl.BlockSpec((1, Q_TILE, LANES), lambda bi, p, i: (bi, i, p)),
            pl.BlockSpec((1, nk, LANES, tk), lambda bi, p, i: (bi, 0, p, 0)),
            pl.BlockSpec((1, n, LANES), lambda bi, p, i: (bi, 0, p)),
            pl.BlockSpec((1, 1, LANES, c), lambda bi, p, i: (bi, 0, p, 0)),
            pl.BlockSpec((1, c, LANES), lambda bi, p, i: (bi, 0, p)),
            pl.BlockSpec((1, 2, Q_TILE, NA_CHUNKS * tk), bias_map),
        ],
        out_specs=pl.BlockSpec((1, Q_TILE, LANES), lambda bi, p, i: (bi, i, p)),
        out_shape=jax.ShapeDtypeStruct((b, n, wtot), BF16),
        compiler_params=_cparams(("parallel", "parallel", "parallel")),
        name="na_attention",
    )(q, kt, v, ktc, vc, bias)


def _first_index_of_max(vals, row, big):
    mx = jnp.max(vals, axis=0, keepdims=True)
    idx = jnp.min(jnp.where(vals == mx, row, big), axis=0, keepdims=True)
    return mx, idx


def _router_gates(scores_t, sel_t):
    irow = lax.broadcasted_iota(jnp.int32, sel_t.shape, 0)
    row = irow.astype(F32)
    grp = lax.shift_right_logical(irow, int(math.log2(EXPERTS_PER_GROUP))).astype(F32)
    big = float(N_EXPERTS)
    best = None
    for g in range(MOE_GROUPS):
        vals = jnp.where(grp == float(g), sel_t, -jnp.inf)
        m1, i1 = _first_index_of_max(vals, row, big)
        m2 = jnp.max(jnp.where(row == i1, -jnp.inf, vals), axis=0, keepdims=True)
        gs = m1 + m2
        if best is None:
            best, bi = gs, jnp.zeros(gs.shape, F32)
        else:
            better = gs > best
            bi = jnp.where(better, float(g), bi)
            best = jnp.where(better, gs, best)
    msel = jnp.where(grp == bi, sel_t, -jnp.inf)
    _, i1 = _first_index_of_max(msel, row, big)
    msel2 = jnp.where(row == i1, -jnp.inf, msel)
    _, i2 = _first_index_of_max(msel2, row, big)
    w = jnp.where((row == i1) | (row == i2), scores_t, 0.0)
    return w / jnp.sum(w, axis=0, keepdims=True)


def _out_kernel(a_ref, b_ref, c_ref, d_ref, x_ref, g1_ref, sh_ref, sc_ref, gffn_ref,
                w_ref, rwh_ref, rwl_ref, rb_ref, xo_ref, h2_ref, gate_ref):
    y = None
    for gi, r in enumerate((a_ref, b_ref, c_ref, d_ref)):
        part = jnp.dot(r[0], w_ref[gi * SEG_W:(gi + 1) * SEG_W, :], preferred_element_type=F32)
        y = part if y is None else y + part
    xn = x_ref[0] + g1_ref[0] * y
    xo_ref[0] = xn
    ms = jnp.mean(xn * xn, axis=-1, keepdims=True)
    h2 = xn * lax.rsqrt(ms + EPS) * gffn_ref[...]
    h2 = h2 * (1.0 + sc_ref[0]) + sh_ref[0]
    hi = h2.astype(BF16)
    h2_ref[0] = hi
    lo = (h2 - hi.astype(F32)).astype(BF16)
    logits = (jnp.dot(hi, rwh_ref[...], preferred_element_type=F32)
              + jnp.dot(lo, rwh_ref[...], preferred_element_type=F32)
              + jnp.dot(hi, rwl_ref[...], preferred_element_type=F32))
    tm = logits.shape[0]
    scores_t = _sigmoid(logits).T[:N_EXPERTS]
    sel_t = scores_t + _tile_lanes(rb_ref[...], tm)
    gates_t = _router_gates(scores_t, sel_t)
    tail_row = lax.broadcasted_iota(jnp.int32, (LANES - N_EXPERTS, tm), 0)
    tail = jnp.where(tail_row == 0, 1.0, 0.0).astype(F32)
    gate_ref[0] = jnp.concatenate([gates_t, tail], axis=0).T


def _out_router(parts, x, g1, sh2, sc2, row_of_batch, pw, tm):
    b, n, d = x.shape
    full = lambda a: pl.BlockSpec(a.shape, lambda bi, i: (0,) * a.ndim)
    tok = lambda w: pl.BlockSpec((1, tm, w), lambda bi, i: (bi, i, 0))
    mod_spec = pl.BlockSpec((1, 1, d), lambda bi, i: (row_of_batch(bi), 0, 0))
    consts = [pw["gffn"], pw["w_out"], pw["rw_hi"], pw["rw_lo"], pw["rb"]]
    return pl.pallas_call(
        _out_kernel,
        grid=(b, n // tm),
        in_specs=[tok(SEG_W)] * 4 + [tok(d), mod_spec, mod_spec, mod_spec] + [full(a) for a in consts],
        out_specs=[tok(d), tok(d), tok(LANES)],
        out_shape=[jax.ShapeDtypeStruct((b, n, d), F32), jax.ShapeDtypeStruct((b, n, d), BF16),
                   jax.ShapeDtypeStruct((b, n, LANES), F32)],
        compiler_params=_cparams(("parallel", "parallel")),
        name="out_router",
    )(*parts, x, g1, sh2, sc2, *consts)


def _moe_kernel(h_ref, gate_ref, x_ref, g2_ref, wg_ref, wu_ref, wd_ref, fin_ref, o_ref, acc_ref,
                *, final_norm):
    e = pl.program_id(1)
    hb = h_ref[...]
    g = jnp.dot(hb, wg_ref[0], preferred_element_type=F32)
    u = jnp.dot(hb, wu_ref[0], preferred_element_type=F32)
    gates = gate_ref[...]
    col = jnp.sum(jnp.where(_lane_iota(gates.shape) == e, gates, 0.0), axis=1, keepdims=True)
    act = (g * _sigmoid(g)) * u * col
    contrib = jnp.dot(act.astype(BF16), wd_ref[0], preferred_element_type=F32)

    @pl.when(e == 0)
    def _():
        acc_ref[...] = contrib

    @pl.when(e > 0)
    def _():
        acc_ref[...] += contrib

    @pl.when(e == pl.num_programs(1) - 1)
    def _():
        y = x_ref[...] + g2_ref[0] * acc_ref[...]
        if final_norm:
            ms = jnp.mean(y * y, axis=-1, keepdims=True)
            y = y * lax.rsqrt(ms + EPS) * fin_ref[...]
        o_ref[...] = y


def _moe(h2, gates, x, g2, row_of_tile, pw, tm, final_norm):
    t, d = x.shape
    ne = pw["wg"].shape[0]
    de = pw["wg"].shape[2]
    return pl.pallas_call(
        functools.partial(_moe_kernel, final_norm=final_norm),
        grid=(t // tm, ne),
        in_specs=[
            pl.BlockSpec((tm, d), lambda i, e: (i, 0)),
            pl.BlockSpec((tm, LANES), lambda i, e: (i, 0)),
            pl.BlockSpec((tm, d), lambda i, e: (i, 0)),
            pl.BlockSpec((1, 1, d), lambda i, e: (row_of_tile(i), 0, 0)),
            pl.BlockSpec((1, d, de), lambda i, e: (e, 0, 0)),
            pl.BlockSpec((1, d, de), lambda i, e: (e, 0, 0)),
            pl.BlockSpec((1, de, d), lambda i, e: (e, 0, 0)),
            pl.BlockSpec((1, d), lambda i, e: (0, 0)),
        ],
        out_specs=pl.BlockSpec((tm, d), lambda i, e: (i, 0)),
        out_shape=jax.ShapeDtypeStruct((t, d), F32),
        scratch_shapes=[pltpu.VMEM((tm, d), F32)],
        compiler_params=_cparams(("parallel", "arbitrary")),
        name="moe_final" if final_norm else "moe",
    )(h2, gates, x, g2, pw["wg"], pw["wu"], pw["wd"], pw["fin"])


def _axial_angles(n_tok, rot_dim):
    half = rot_dim // 2
    freqs = ROPE_BASE ** (-jnp.arange(0, half, 2, dtype=F32) / half)
    t = jnp.arange(n_tok, dtype=jnp.int32)
    row = (t // GRID_W).astype(F32)
    col = (t % GRID_W).astype(F32)
    return row[:, None] * freqs, col[:, None] * freqs


def _rope_table(n_tok, rot_dim):
    ar, ac = _axial_angles(n_tok, rot_dim)
    cos = jnp.concatenate([jnp.cos(ar)] * 2 + [jnp.cos(ac)] * 2, axis=1)
    sin = jnp.concatenate([-jnp.sin(ar), jnp.sin(ar), -jnp.sin(ac), jnp.sin(ac)], axis=1)
    return cos, sin


def _rope_tables(n_tok):
    cg, sg = _rope_table(n_tok, GQA_DIM)
    cd, sd = _rope_table(n_tok, DIFF_QK_DIM)
    cm32, sm32 = _rope_table(n_tok, MLA_ROPE)
    ones = jnp.ones((n_tok, MLA_NOPE), F32)
    zeros = jnp.zeros((n_tok, MLA_NOPE), F32)
    pad = LANES - MLA_NOPE - MLA_ROPE
    cm = jnp.concatenate([ones, cm32, ones[:, :pad]], axis=1)
    sm = jnp.concatenate([zeros, sm32, zeros[:, :pad]], axis=1)
    return (jnp.tile(cg, (1, LANES // GQA_DIM)), jnp.tile(sg, (1, LANES // GQA_DIM)),
            jnp.tile(cd, (1, LANES // DIFF_QK_DIM)), jnp.tile(sd, (1, LANES // DIFF_QK_DIM)),
            cm, sm)


def _pack_layer(l, p):
    d = p["w_in"].shape[1]
    w = p["w_in"][l]
    o = np.cumsum([0, 256, 128, 128, 256, 256, 256, 256, 256, 256, MLA_Q_LORA, MLA_KV_LORA, MLA_ROPE])
    col = lambda i: w[:, int(o[i]):int(o[i + 1])]
    dup = lambda a: jnp.concatenate([a[:, :GQA_DIM], a[:, :GQA_DIM], a[:, GQA_DIM:], a[:, GQA_DIM:]], axis=1)
    z = lambda n: jnp.zeros((d, n), F32)
    w_in = jnp.concatenate(
        [col(0), dup(col(1)), dup(col(2)), col(3), col(4), col(5), col(6), col(7), col(8),
         col(9), z(SEG_W - MLA_Q_LORA), col(10),
         z(MLA_NOPE), col(11), z(LANES - MLA_NOPE - MLA_ROPE)], axis=1).astype(BF16)
    dq = MLA_NOPE + MLA_ROPE
    wqu = p["mla_w_q_up"][l].reshape(MLA_Q_LORA, MLA_HEADS, dq)
    wqu = jnp.pad(wqu, ((0, SEG_W - MLA_Q_LORA), (0, 0), (0, LANES - dq)))
    wkv = p["mla_w_kv_up"][l].reshape(MLA_KV_LORA, MLA_HEADS, MLA_NOPE + MLA_V)
    wkk = jnp.pad(wkv[:, :, :MLA_NOPE], ((0, 0), (0, 0), (0, LANES - MLA_NOPE)))
    e = p["moe_w_gate"].shape[1]
    rw = jnp.pad(p["router_w"], ((0, 0), (0, LANES - e)))
    rw_hi = rw.astype(BF16)
    rw_lo = (rw - rw_hi.astype(F32)).astype(BF16)
    return {
        "gmix": p["norm_mix"][l][None], "gffn": p["norm_ffn"][l][None],
        "w_in": w_in,
        "wq_up": wqu.reshape(SEG_W, MLA_HEADS * LANES).astype(BF16),
        "wkv_k": wkk.reshape(MLA_KV_LORA, MLA_HEADS * LANES).astype(BF16),
        "wkv_v": wkv[:, :, MLA_NOPE:].reshape(MLA_KV_LORA, MLA_HEADS * MLA_V).astype(BF16),
        "gq": (jnp.tile(p["gqa_q_norm"][l], GQA_HEADS) * (GQA_DIM ** -0.5))[None],
        "gk": jnp.tile(p["gqa_k_norm"][l], GQA_HEADS)[None],
        "gcq": jnp.pad(p["mla_q_norm"][l], (0, SEG_W - MLA_Q_LORA))[None],
        "gckv": p["mla_kv_norm"][l][None],
        "w_out": p["w_out"][l].astype(BF16),
        "rw_hi": rw_hi, "rw_lo": rw_lo,
        "rb": jnp.broadcast_to(p["router_b"][:, None], (e, LANES)),
        "wg": jnp.concatenate([p["moe_w_gate"][l], p["shared_w_gate"][l][None]], axis=0).astype(BF16),
        "wu": jnp.concatenate([p["moe_w_up"][l], p["shared_w_up"][l][None]], axis=0).astype(BF16),
        "wd": jnp.concatenate([p["moe_w_down"][l], p["shared_w_down"][l][None]], axis=0).astype(BF16),
        "fin": p["final_norm"][None],
        "diff": (p["diff_lq1"][l][None], p["diff_lk1"][l][None], p["diff_lq2"][l][None],
                 p["diff_lk2"][l][None], jnp.tile(p["diff_subln"][l], 2)[None]),
    }


def kernel(x, c, ctx, c_ctx, w_mod, b_mod, norm_mix, norm_ffn, w_in, w_out, gqa_q_norm, gqa_k_norm,
           na_rel_bias, diff_lq1, diff_lk1, diff_lq2, diff_lk2, diff_subln, mla_q_norm, mla_w_q_up,
           mla_kv_norm, mla_w_kv_up, router_w, router_b, moe_w_gate, moe_w_up, moe_w_down,
           shared_w_gate, shared_w_up, shared_w_down, final_norm):
    p = dict(w_in=w_in, norm_mix=norm_mix, norm_ffn=norm_ffn, w_out=w_out, gqa_q_norm=gqa_q_norm,
             gqa_k_norm=gqa_k_norm, diff_lq1=diff_lq1, diff_lk1=diff_lk1, diff_lq2=diff_lq2,
             diff_lk2=diff_lk2, diff_subln=diff_subln, mla_q_norm=mla_q_norm, mla_w_q_up=mla_w_q_up,
             mla_kv_norm=mla_kv_norm, mla_w_kv_up=mla_w_kv_up, router_w=router_w, router_b=router_b,
             moe_w_gate=moe_w_gate, moe_w_up=moe_w_up, moe_w_down=moe_w_down,
             shared_w_gate=shared_w_gate, shared_w_up=shared_w_up, shared_w_down=shared_w_down,
             final_norm=final_norm)
    b, s, d = x.shape
    nctx = ctx.shape[1]
    depth = w_mod.shape[0]
    rows = s // GRID_W
    assert s % Q_TILE == 0 and s % TOK_TILE == 0 and s // TOK_TILE >= NA_CHUNKS
    assert nctx % LANES == 0 and (b * s) % MOE_TILE == 0 and s % MOE_TILE == 0

    mrows = -(-(b + 1) // SUBLANES) * SUBLANES
    cc = jnp.zeros((mrows, d), F32).at[:b].set(c).at[b].set(c_ctx)
    mod = _modulation(cc, w_mod, b_mod)
    tables = _rope_tables(s)
    lat_row = lambda bi: bi
    ctx_row = lambda bi: b
    tm_ctx = b * nctx if b * nctx <= MOE_TILE else nctx

    xc = ctx
    for l in range(depth):
        want_ctx = l < depth - 1
        lam_init = 0.8 - 0.6 * math.exp(-0.3 * l)
        pw = _pack_layer(l, p)
        m6 = mod[l].reshape(mrows, 6, 1, d)
        sh1, sc1, g1, sh2, sc2, g2 = (m6[:, k] for k in range(6))

        lat = _project(x, sh1, sc1, lat_row, pw, tables, TOK_TILE)
        cx = _project(xc, sh1, sc1, ctx_row, pw, None, nctx)
        qa, kta, va, qb, ktb, vb, qc, ktc, vc, qd, ktd, vd = lat
        qa_c, kta_c, va_c, qb_c, ktb_c, vb_c, qc_c, ktc_c, vc_c, qd_c, ktd_c, vd_c = cx

        a_lat = _flash(qa, kta_c, va_c, kta, va, "pair")
        bias = _na_bias_tiles(na_rel_bias[l], rows, TOK_TILE)
        b_lat = _na_attention(qb, ktb_c, vb_c, ktb, vb, bias)
        c_lat = _flash(qc, ktc_c, vc_c, ktc, vc, "diff", pw["diff"], lam_init)
        d_lat = _flash(qd, ktd_c, vd_c, ktd, vd, "mla")
        x1, h2, gates = _out_router((a_lat, b_lat, c_lat, d_lat), x, g1, sh2, sc2, lat_row, pw, TOK_TILE)
        tiles_per_batch = s // MOE_TILE
        x = _moe(h2.reshape(b * s, d), gates.reshape(b * s, LANES), x1.reshape(b * s, d), g2,
                 lambda i: i // tiles_per_batch, pw, MOE_TILE, not want_ctx).reshape(b, s, d)

        if want_ctx:
            a_c = _flash(qa_c, kta_c, va_c, None, None, "pair")
            b_c = _flash(qb_c, ktb_c, vb_c, None, None, "pair")
            c_c = _flash(qc_c, ktc_c, vc_c, None, None, "diff", pw["diff"], lam_init)
            d_c = _flash(qd_c, ktd_c, vd_c, None, None, "mla")
            xc1, h2c, gates_c = _out_router((a_c, b_c, c_c, d_c), xc, g1, sh2, sc2, ctx_row, pw, nctx)
            xc = _moe(h2c.reshape(b * nctx, d), gates_c.reshape(b * nctx, LANES),
                      xc1.reshape(b * nctx, d), g2, lambda i: b, pw, tm_ctx, False).reshape(b, nctx, d)
    return x
```

```python
import functools
import math

import numpy as np
import jax
import jax.numpy as jnp
from jax import lax
from jax.experimental import pallas as pl
from jax.experimental.pallas import tpu as pltpu

F32 = jnp.float32
BF16 = jnp.bfloat16

GRID_W = 64
EPS = 1e-6
ROPE_BASE = 10000.0
GQA_HEADS, GQA_KV_HEADS, GQA_DIM = 4, 2, 64
NA_HEADS, NA_DIM, NA_WIN_ROWS, NA_WIN_COLS = 4, 64, 8, 16
DIFF_HEADS, DIFF_QK_DIM, DIFF_V_DIM = 4, 32, 64
MLA_HEADS, MLA_NOPE, MLA_ROPE, MLA_V, MLA_Q_LORA, MLA_KV_LORA = 4, 64, 32, 64, 192, 128
N_EXPERTS, MOE_GROUPS, D_EXPERT = 16, 4, 256
EXPERTS_PER_GROUP = N_EXPERTS // MOE_GROUPS

LANES = 128
SUBLANES = 8
VMEM_LIMIT = 56 * 1024 * 1024

TOK_TILE = 512
Q_TILE = 512
MOE_TILE = 1024
MOD_TILE = 1024
NEG = -1e30
LOG2E = math.log2(math.e)

SEG_W = 256
OFF_AQ, OFF_AK, OFF_AV = 0, 256, 512
OFF_BQ, OFF_BK, OFF_BV = 768, 1024, 1280
OFF_CQ, OFF_CK, OFF_CV = 1536, 1792, 2048
OFF_DQ, OFF_DKV, OFF_DPE = 2304, 2560, 2688
IN_PACKED = 2816


def _cparams(sem):
    return pltpu.CompilerParams(dimension_semantics=sem, vmem_limit_bytes=VMEM_LIMIT)


def _lane_iota(shape):
    return lax.broadcasted_iota(jnp.int32, shape, 1)


def _lane_group(shape, width):
    return lax.shift_right_logical(_lane_iota(shape), int(math.log2(width)))


def _sigmoid(x):
    return 1.0 / (1.0 + jnp.exp(-x))


def _mod_kernel(c_ref, w_ref, b_ref, o_ref):
    c = c_ref[...]
    s = c * _sigmoid(c)
    o_ref[0] = jnp.dot(s, w_ref[0], preferred_element_type=F32,
                       precision=lax.Precision.HIGHEST) + b_ref[0]


def _modulation(cc, w_mod, b_mod):
    depth, d, n = w_mod.shape
    rows = cc.shape[0]
    return pl.pallas_call(
        _mod_kernel,
        grid=(depth, n // MOD_TILE),
        in_specs=[
            pl.BlockSpec((rows, d), lambda l, j: (0, 0)),
            pl.BlockSpec((1, d, MOD_TILE), lambda l, j: (l, 0, j)),
            pl.BlockSpec((1, 1, MOD_TILE), lambda l, j: (l, 0, j)),
        ],
        out_specs=pl.BlockSpec((1, rows, MOD_TILE), lambda l, j: (l, 0, j)),
        out_shape=jax.ShapeDtypeStruct((depth, rows, n), F32),
        compiler_params=_cparams(("parallel", "parallel")),
        name="modulation",
    )(cc, w_mod, b_mod.reshape(depth, 1, n))


def _swap_halves(x, half):
    w = x.shape[1]
    lane = _lane_iota(x.shape)
    fwd = pltpu.roll(x, half, 1)
    bwd = pltpu.roll(x, w - half, 1)
    return jnp.where((lane & (2 * half - 1)) < half, bwd, fwd)


def _rope(x, cos, sin, half):
    reps = x.shape[1] // cos.shape[1]
    c = jnp.concatenate([cos] * reps, axis=1) if reps > 1 else cos
    s = jnp.concatenate([sin] * reps, axis=1) if reps > 1 else sin
    return x * c + _swap_halves(x, half) * s


def _group_rms(x, group, count):
    gid = _lane_group(x.shape, group)
    x2 = x * x
    inv = jnp.zeros_like(x)
    for g in range(x.shape[1] // group):
        msk = gid == g
        ms = jnp.sum(jnp.where(msk, x2, 0.0), axis=1, keepdims=True) * (1.0 / count)
        inv = jnp.where(msk, lax.rsqrt(ms + EPS), inv)
    return x * inv


LAT_OUTS = ("A.qT", "A.k", "A.vT", "B.q", "B.kT", "B.v", "C.qT", "C.k", "C.vT", "D.qT", "D.k", "D.vT")
CTX_KV_OUTS = ("A.k", "A.vT", "B.kT", "B.v", "C.k", "C.vT", "D.k", "D.vT")
CTX_OUTS = CTX_KV_OUTS + ("A.qT", "B.qT", "B.k", "B.vT", "C.qT", "D.qT")
GROUP_W = {"A": SEG_W, "B": SEG_W, "C": SEG_W, "D": 2 * SEG_W}


def _proj_kernel(*refs, rope, outs):
    (x_ref, sh_ref, sc_ref, gmix_ref, w_ref, wqu_ref, wkk_ref, wkv_ref,
     gq_ref, gk_ref, gcq_ref, gckv_ref) = refs[:12]
    pos = 12
    if rope:
        cg_ref, sg_ref, cd_ref, sd_ref, cm_ref, sm_ref = refs[pos:pos + 6]
        pos += 6
    out = dict(zip(outs, refs[pos:pos + len(outs)]))

    def want(g, t):
        return (g + "." + t) in out or (g + "." + t + "T") in out

    def put(g, t, val):
        if g + "." + t in out:
            out[g + "." + t][0] = val.astype(BF16)
        if g + "." + t + "T" in out:
            out[g + "." + t + "T"][0, 0] = val.T.astype(BF16)

    x = x_ref[0]
    ms = jnp.mean(x * x, axis=-1, keepdims=True)
    h = x * lax.rsqrt(ms + EPS) * gmix_ref[...]
    h = h * (1.0 + sc_ref[0]) + sh_ref[0]
    hb = h.astype(BF16)

    def seg(off, width=SEG_W):
        return jnp.dot(hb, w_ref[:, off:off + width], preferred_element_type=F32)

    if want("A", "q"):
        q = _group_rms(seg(OFF_AQ), GQA_DIM, GQA_DIM) * gq_ref[...]
        if rope:
            q = _rope(q, cg_ref[...], sg_ref[...], GQA_DIM // 4)
        put("A", "q", q)
    k = _group_rms(seg(OFF_AK), GQA_DIM, GQA_DIM) * gk_ref[...]
    if rope:
        k = _rope(k, cg_ref[...], sg_ref[...], GQA_DIM // 4)
    put("A", "k", k)
    put("A", "v", seg(OFF_AV))

    if want("B", "q"):
        put("B", "q", seg(OFF_BQ) * (NA_DIM ** -0.5 * LOG2E))
    put("B", "k", seg(OFF_BK))
    put("B", "v", seg(OFF_BV))

    if want("C", "q"):
        q = seg(OFF_CQ) * (DIFF_QK_DIM ** -0.5 * LOG2E)
        if rope:
            q = _rope(q, cd_ref[...], sd_ref[...], DIFF_QK_DIM // 4)
        put("C", "q", q)
    k = seg(OFF_CK)
    if rope:
        k = _rope(k, cd_ref[...], sd_ref[...], DIFF_QK_DIM // 4)
    put("C", "k", k)
    put("C", "v", seg(OFF_CV))

    if want("D", "q"):
        cq = _group_rms(seg(OFF_DQ), SEG_W, MLA_Q_LORA) * gcq_ref[...]
        q = jnp.dot(cq.astype(BF16), wqu_ref[...], preferred_element_type=F32)
        q = q * ((MLA_NOPE + MLA_ROPE) ** -0.5 * LOG2E)
        if rope:
            q = _rope(q, cm_ref[...], sm_ref[...], MLA_ROPE // 4)
        put("D", "q", q)
    ckv = _group_rms(seg(OFF_DKV, LANES), LANES, MLA_KV_LORA) * gckv_ref[...]
    ckvb = ckv.astype(BF16)
    k = jnp.dot(ckvb, wkk_ref[...], preferred_element_type=F32)
    kpe = seg(OFF_DPE, LANES)
    if rope:
        kpe = _rope(kpe, cm_ref[...], sm_ref[...], MLA_ROPE // 4)
    put("D", "k", k + jnp.concatenate([kpe] * MLA_HEADS, axis=1))
    put("D", "v", jnp.dot(ckvb, wkv_ref[...], preferred_element_type=F32))


def _project(x, sh, sc, row_of_batch, pw, tables, tm, outs):
    b, n, d = x.shape
    nt = n // tm
    rope = tables is not None
    full = lambda a: pl.BlockSpec(a.shape, lambda bi, i: (0,) * a.ndim)
    mod_spec = pl.BlockSpec((1, 1, d), lambda bi, i: (row_of_batch(bi), 0, 0))
    ins = [x, sh, sc, pw["gmix"], pw["w_in"], pw["wq_up"], pw["wkv_k"], pw["wkv_v"],
           pw["gq"], pw["gk"], pw["gcq"], pw["gckv"]]
    in_specs = [pl.BlockSpec((1, tm, d), lambda bi, i: (bi, i, 0)), mod_spec, mod_spec]
    in_specs += [full(a) for a in ins[3:]]
    if rope:
        ins += list(tables)
        in_specs += [pl.BlockSpec((tm, LANES), lambda bi, i: (i, 0)) for _ in tables]
    shapes, specs = [], []
    for name in outs:
        g, t = name.split(".")
        w = SEG_W if t[0] == "v" else GROUP_W[g]
        if t.endswith("T"):
            shapes.append(jax.ShapeDtypeStruct((b, nt, w, tm), BF16))
            specs.append(pl.BlockSpec((1, 1, w, tm), lambda bi, i: (bi, i, 0, 0)))
        else:
            shapes.append(jax.ShapeDtypeStruct((b, n, w), BF16))
            specs.append(pl.BlockSpec((1, tm, w), lambda bi, i: (bi, i, 0)))
    res = pl.pallas_call(
        functools.partial(_proj_kernel, rope=rope, outs=tuple(outs)),
        grid=(b, nt),
        in_specs=in_specs,
        out_specs=specs,
        out_shape=shapes,
        compiler_params=_cparams(("parallel", "parallel")),
        name="proj_rope" if rope else "proj_ctx",
    )(*ins)
    return dict(zip(outs, res))


HEAD_V = 64


def _tile_lanes(x, width):
    reps = width // x.shape[1]
    return jnp.tile(x, (1, reps)) if reps > 1 else x


def _flash_kernel(*refs, kind, has_latent, lam_init):
    qt_ref, kc_ref, vct_ref = refs[:3]
    pos = 3
    if has_latent:
        k_ref, vt_ref = refs[pos:pos + 2]
        pos += 2
    if kind == "diff":
        lq1_ref, lk1_ref, lq2_ref, lk2_ref, sub_ref = refs[pos:pos + 5]
        pos += 5
    o_ref = refs[pos]
    qs_ref, acc_ref = refs[pos + 1:pos + 3]
    if has_latent:
        s_ref = refs[pos + 3]

    nm = qs_ref.shape[0]

    for m in range(nm):
        if kind == "mla":
            qs_ref[m] = qt_ref[0, 0, m * LANES:(m + 1) * LANES, :]
        else:
            qv = qt_ref[0, 0]
            rows = lax.broadcasted_iota(jnp.int32, qv.shape, 0)
            own = lax.shift_right_logical(rows, int(math.log2(LANES // nm))) == m
            qs_ref[m] = jnp.where(own, qv, jnp.zeros_like(qv))

    def k_lanes(m):
        return slice(m * LANES, (m + 1) * LANES) if kind == "mla" else slice(None)

    def v_rows(m):
        hv = m // 2 if kind == "diff" else m
        return slice(hv * HEAD_V, (hv + 1) * HEAD_V)

    def with_ones(vt):
        rows = lax.broadcasted_iota(jnp.int32, (LANES - HEAD_V, vt.shape[1]), 0)
        ones = jnp.where(rows == 0, 1.0, 0.0).astype(BF16)
        return jnp.concatenate([vt, ones], axis=0)

    ms = []
    for m in range(nm):
        st = jnp.dot(kc_ref[0, :, k_lanes(m)], qs_ref[m], preferred_element_type=F32)
        mx = jnp.max(st, axis=0, keepdims=True)
        p = jnp.exp2(st - mx).astype(BF16)
        acc_ref[m] = jnp.dot(with_ones(vct_ref[0, 0, v_rows(m), :]), p, preferred_element_type=F32)
        ms.append(mx)

    if has_latent:
        nk, tk = vt_ref.shape[1], vt_ref.shape[3]

        def produce(slot, j, m):
            rows = pl.ds(pl.multiple_of(j * tk, tk), tk)
            st = jnp.dot(k_ref[0, rows, k_lanes(m)], qs_ref[m], preferred_element_type=F32)
            s_ref[slot, m] = st
            return jnp.max(st, axis=0, keepdims=True)

        def consume(slot, j, m, m_prev, m_chunk):
            m_new = jnp.maximum(m_prev, m_chunk)
            alpha = jnp.exp2(m_prev - m_new)
            p = jnp.exp2(s_ref[slot, m] - m_new).astype(BF16)
            acc_ref[m] = alpha * acc_ref[m] + jnp.dot(
                with_ones(vt_ref[0, j, v_rows(m), :]), p, preferred_element_type=F32)
            return m_new

        def pair_of_chunks(i, carry, last):
            ms, mcs = carry
            j0 = 2 * i
            c1 = [produce(1, j0 + 1, m) for m in range(nm)]
            m1 = [consume(0, j0, m, ms[m], mcs[m]) for m in range(nm)]
            c2 = c1 if last else [produce(0, j0 + 2, m) for m in range(nm)]
            m2 = [consume(1, j0 + 1, m, m1[m], c1[m]) for m in range(nm)]
            return tuple(m2), tuple(c2)

        carry = (tuple(ms), tuple(produce(0, 0, m) for m in range(nm)))
        carry = lax.fori_loop(0, nk // 2 - 1, functools.partial(pair_of_chunks, last=False), carry)
        pair_of_chunks(nk // 2 - 1, carry, True)

    outs = [acc_ref[m, 0:HEAD_V, :] / acc_ref[m, HEAD_V:HEAD_V + 1, :] for m in range(nm)]
    if kind == "diff":
        lam = (jnp.exp(jnp.sum(lq1_ref[...] * lk1_ref[...], axis=1, keepdims=True))
               - jnp.exp(jnp.sum(lq2_ref[...] * lk2_ref[...], axis=1, keepdims=True))
               + lam_init)
        heads = []
        for hd in range(2):
            dlt = outs[2 * hd] - lam * outs[2 * hd + 1]
            ms2 = jnp.mean(dlt * dlt, axis=0, keepdims=True)
            heads.append(dlt * lax.rsqrt(ms2 + EPS) * sub_ref[...] * (1.0 - lam_init))
        outs = heads
    o_ref[0] = jnp.concatenate(outs, axis=0).T.astype(o_ref.dtype)


def _flash(qt, kc, vct, k, vt, kind, extras=(), lam_init=0.0):
    b, nq, wtot, tq = qt.shape
    wq = 2 * LANES if kind == "mla" else LANES
    npair = wtot // wq
    nm = 4 if kind == "diff" else 2
    c = kc.shape[1]
    has_latent = k is not None
    ins = [qt, kc, vct]
    in_specs = [
        pl.BlockSpec((1, 1, wq, tq), lambda bi, p, i: (bi, i, p, 0)),
        pl.BlockSpec((1, c, wq), lambda bi, p, i: (bi, 0, p)),
        pl.BlockSpec((1, 1, LANES, c), lambda bi, p, i: (bi, 0, p, 0)),
    ]
    if has_latent:
        nk, tk = vt.shape[1], vt.shape[3]
        assert nk % 2 == 0, "key chunks are processed in pairs"
        ins += [k, vt]
        in_specs += [
            pl.BlockSpec((1, k.shape[1], wq), lambda bi, p, i: (bi, 0, p)),
            pl.BlockSpec((1, nk, LANES, tk), lambda bi, p, i: (bi, 0, p, 0)),
        ]
    for e in extras:
        ins.append(e)
        in_specs.append(pl.BlockSpec(e.shape, lambda bi, p, i: (0, 0)))
    return pl.pallas_call(
        functools.partial(_flash_kernel, kind=kind, has_latent=has_latent, lam_init=lam_init),
        grid=(b, npair, nq),
        in_specs=in_specs,
        out_specs=pl.BlockSpec((1, tq, LANES), lambda bi, p, i: (bi, i, p)),
        out_shape=jax.ShapeDtypeStruct((b, nq * tq, npair * LANES), BF16),
        scratch_shapes=[
            pltpu.VMEM((nm, LANES, tq), BF16),
            pltpu.VMEM((nm, LANES, tq), F32),
        ] + ([pltpu.VMEM((2, nm, tk, tq), F32)] if has_latent else []),
        compiler_params=_cparams(("parallel", "parallel", "parallel")),
        name="flash_" + kind + ("" if has_latent else "_ctx"),
    )(*ins)


NA_CHUNKS = 3


def _na_kernel(q_ref, kt_ref, v_ref, kc_ref, vc_ref, bias_ref, o_ref):
    tq = q_ref.shape[1]
    nk, tk = kt_ref.shape[1], kt_ref.shape[3]
    i = pl.program_id(2)
    c0 = jnp.clip(i - 1, 0, nk - NA_CHUNKS)
    lane = _lane_iota((tq, LANES))
    qv = q_ref[0]
    outs = []
    for m in range(2):
        qm = jnp.where(_lane_group(qv.shape, LANES // 2) == m, qv, jnp.zeros_like(qv))
        ss = [jnp.dot(qm, kt_ref[0, c0 + j], preferred_element_type=F32)
              + bias_ref[0, m, :, j * tk:(j + 1) * tk] for j in range(NA_CHUNKS)]
        sc = jnp.dot(qm, kc_ref[0, 0], preferred_element_type=F32)
        mx = jnp.max(sc, axis=1, keepdims=True)
        for s in ss:
            mx = jnp.maximum(mx, jnp.max(s, axis=1, keepdims=True))
        pc = jnp.exp2(sc - mx)
        l = jnp.sum(pc, axis=1, keepdims=True)
        acc = jnp.dot(pc.astype(BF16), vc_ref[0], preferred_element_type=F32)
        for j, s in enumerate(ss):
            p = jnp.exp2(s - mx)
            l = l + jnp.sum(p, axis=1, keepdims=True)
            vv = v_ref[0, pl.ds(pl.multiple_of((c0 + j) * tk, tk), tk), :]
            acc = acc + jnp.dot(p.astype(BF16), vv, preferred_element_type=F32)
        outs.append(acc / l)
    o_ref[0] = jnp.where(lane < LANES // 2, outs[0], outs[1]).astype(o_ref.dtype)


def _na_bias_tiles(rel_bias, rows, tk):
    qrows = Q_TILE // GRID_W
    nblk = rows // qrows
    nk = rows * GRID_W // tk
    krows = NA_CHUNKS * tk // GRID_W
    kh = min(NA_WIN_ROWS, rows)
    kw = NA_WIN_COLS
    blocks = np.array([0, 1, nblk - 1])
    c0 = np.clip(blocks - 1, 0, nk - NA_CHUNKS)
    r = blocks[:, None] * qrows + np.arange(qrows)[None, :]
    kr = c0[:, None] * (tk // GRID_W) + np.arange(krows)[None, :]
    r0 = np.clip(r - kh // 2, 0, rows - kh)
    dr = kr[:, None, :] - r[:, :, None]
    valid_r = (kr[:, None, :] >= r0[:, :, None]) & (kr[:, None, :] < r0[:, :, None] + kh)
    cols = np.arange(GRID_W)
    cstart = np.clip(cols - kw // 2, 0, GRID_W - kw)
    dc = cols[None, :] - cols[:, None]
    valid_c = (cols[None, :] >= cstart[:, None]) & (cols[None, :] < cstart[:, None] + kw)
    ndc = 2 * NA_WIN_COLS - 1
    onehot = ((dc[None] + NA_WIN_COLS - 1) == np.arange(ndc)[:, None, None]) & valid_c[None]
    dr_idx = np.clip(dr + NA_WIN_ROWS - 1, 0, 2 * NA_WIN_ROWS - 2)
    g1 = rel_bias[:, dr_idx, :]
    tiles = jnp.einsum("htabd,dcx->thacbx", g1, jnp.asarray(onehot, F32),
                       precision=lax.Precision.HIGHEST)
    valid = valid_r[:, None, :, None, :, None] & valid_c[None, None, None, :, None, :]
    tiles = jnp.where(jnp.asarray(valid), tiles * LOG2E, NEG)
    return tiles.reshape(3, rel_bias.shape[0], qrows * GRID_W, krows * GRID_W)


def _na_attention(q, ktc, vc, kt, v, bias):
    b, n, wtot = q.shape
    npair = wtot // LANES
    nk, tk = kt.shape[1], kt.shape[3]
    c = vc.shape[1]
    nblk = n // Q_TILE

    def bias_map(bi, p, i):
        return (jnp.where(i == 0, 0, jnp.where(i == nblk - 1, 2, 1)), p, 0, 0)

    return pl.pallas_call(
        _na_kernel,
        grid=(b, npair, nblk),
        in_specs=[
            pl.BlockSpec((1, Q_TILE, LANES), lambda bi, p, i: (bi, i, p)),
            pl.BlockSpec((1, nk, LANES, tk), lambda bi, p, i: (bi, 0, p, 0)),
            pl.BlockSpec((1, n, LANES), lambda bi, p, i: (bi, 0, p)),
            pl.BlockSpec((1, 1, LANES, c), lambda bi, p, i: (bi, 0, p, 0)),
            pl.BlockSpec((1, c, LANES), lambda bi, p, i: (bi, 0, p)),
            pl.BlockSpec((1, 2, Q_TILE, NA_CHUNKS * tk), bias_map),
        ],
        out_specs=pl.BlockSpec((1, Q_TILE, LANES), lambda bi, p, i: (bi, i, p)),
        out_shape=jax.ShapeDtypeStruct((b, n, wtot), BF16),
        compiler_params=_cparams(("parallel", "parallel", "parallel")),
        name="na_attention",
    )(q, kt, v, ktc, vc, bias)


def _first_index_of_max(vals, row, big):
    mx = jnp.max(vals, axis=0, keepdims=True)
    idx = jnp.min(jnp.where(vals == mx, row, big), axis=0, keepdims=True)
    return mx, idx


def _router_gates(scores_t, sel_t):
    irow = lax.broadcasted_iota(jnp.int32, sel_t.shape, 0)
    row = irow.astype(F32)
    grp = lax.shift_right_logical(irow, int(math.log2(EXPERTS_PER_GROUP))).astype(F32)
    big = float(N_EXPERTS)
    best = None
    for g in range(MOE_GROUPS):
        vals = jnp.where(grp == float(g), sel_t, -jnp.inf)
        m1, i1 = _first_index_of_max(vals, row, big)
        m2 = jnp.max(jnp.where(row == i1, -jnp.inf, vals), axis=0, keepdims=True)
        gs = m1 + m2
        if best is None:
            best, bi = gs, jnp.zeros(gs.shape, F32)
        else:
            better = gs > best
            bi = jnp.where(better, float(g), bi)
            best = jnp.where(better, gs, best)
    msel = jnp.where(grp == bi, sel_t, -jnp.inf)
    _, i1 = _first_index_of_max(msel, row, big)
    msel2 = jnp.where(row == i1, -jnp.inf, msel)
    _, i2 = _first_index_of_max(msel2, row, big)
    w = jnp.where((row == i1) | (row == i2), scores_t, 0.0)
    return w / jnp.sum(w, axis=0, keepdims=True)


def _out_kernel(a_ref, b_ref, c_ref, d_ref, x_ref, g1_ref, sh_ref, sc_ref, gffn_ref,
                w_ref, rwh_ref, rwl_ref, rb_ref, xo_ref, h2_ref, gate_ref):
    y = None
    for gi, r in enumerate((a_ref, b_ref, c_ref, d_ref)):
        part = jnp.dot(r[0], w_ref[gi * SEG_W:(gi + 1) * SEG_W, :], preferred_element_type=F32)
        y = part if y is None else y + part
    xn = x_ref[0] + g1_ref[0] * y
    xo_ref[0] = xn
    ms = jnp.mean(xn * xn, axis=-1, keepdims=True)
    h2 = xn * lax.rsqrt(ms + EPS) * gffn_ref[...]
    h2 = h2 * (1.0 + sc_ref[0]) + sh_ref[0]
    hi = h2.astype(BF16)
    h2_ref[0] = hi
    lo = (h2 - hi.astype(F32)).astype(BF16)
    logits = (jnp.dot(hi, rwh_ref[...], preferred_element_type=F32)
              + jnp.dot(lo, rwh_ref[...], preferred_element_type=F32)
              + jnp.dot(hi, rwl_ref[...], preferred_element_type=F32))
    tm = logits.shape[0]
    scores_t = _sigmoid(logits).T[:N_EXPERTS]
    sel_t = scores_t + _tile_lanes(rb_ref[...], tm)
    gates_t = _router_gates(scores_t, sel_t)
    tail_row = lax.broadcasted_iota(jnp.int32, (LANES - N_EXPERTS, tm), 0)
    tail = jnp.where(tail_row == 0, 1.0, 0.0).astype(F32)
    gate_ref[0] = jnp.concatenate([gates_t, tail], axis=0).T


def _out_router(parts, x, g1, sh2, sc2, row_of_batch, pw, tm):
    b, n, d = x.shape
    full = lambda a: pl.BlockSpec(a.shape, lambda bi, i: (0,) * a.ndim)
    tok = lambda w: pl.BlockSpec((1, tm, w), lambda bi, i: (bi, i, 0))
    mod_spec = pl.BlockSpec((1, 1, d), lambda bi, i: (row_of_batch(bi), 0, 0))
    consts = [pw["gffn"], pw["w_out"], pw["rw_hi"], pw["rw_lo"], pw["rb"]]
    return pl.pallas_call(
        _out_kernel,
        grid=(b, n // tm),
        in_specs=[tok(SEG_W)] * 4 + [tok(d), mod_spec, mod_spec, mod_spec] + [full(a) for a in consts],
        out_specs=[tok(d), tok(d), tok(LANES)],
        out_shape=[jax.ShapeDtypeStruct((b, n, d), F32), jax.ShapeDtypeStruct((b, n, d), BF16),
                   jax.ShapeDtypeStruct((b, n, LANES), F32)],
        compiler_params=_cparams(("parallel", "parallel")),
        name="out_router",
    )(*parts, x, g1, sh2, sc2, *consts)


def _moe_kernel(h_ref, gate_ref, x_ref, g2_ref, wg_ref, wu_ref, wd_ref, fin_ref, o_ref, acc_ref,
                *, final_norm):
    e = pl.program_id(1)
    hb = h_ref[...]
    g = jnp.dot(hb, wg_ref[0], preferred_element_type=F32)
    u = jnp.dot(hb, wu_ref[0], preferred_element_type=F32)
    gates = gate_ref[...]
    col = jnp.sum(jnp.where(_lane_iota(gates.shape) == e, gates, 0.0), axis=1, keepdims=True)
    act = (g * _sigmoid(g)) * u * col
    contrib = jnp.dot(act.astype(BF16), wd_ref[0], preferred_element_type=F32)

    @pl.when(e == 0)
    def _():
        acc_ref[...] = contrib

    @pl.when(e > 0)
    def _():
        acc_ref[...] += contrib

    @pl.when(e == pl.num_programs(1) - 1)
    def _():
        y = x_ref[...] + g2_ref[0] * acc_ref[...]
        if final_norm:
            ms = jnp.mean(y * y, axis=-1, keepdims=True)
            y = y * lax.rsqrt(ms + EPS) * fin_ref[...]
        o_ref[...] = y


def _moe(h2, gates, x, g2, row_of_tile, pw, tm, final_norm):
    t, d = x.shape
    ne = pw["wg"].shape[0]
    de = pw["wg"].shape[2]
    return pl.pallas_call(
        functools.partial(_moe_kernel, final_norm=final_norm),
        grid=(t // tm, ne),
        in_specs=[
            pl.BlockSpec((tm, d), lambda i, e: (i, 0)),
            pl.BlockSpec((tm, LANES), lambda i, e: (i, 0)),
            pl.BlockSpec((tm, d), lambda i, e: (i, 0)),
            pl.BlockSpec((1, 1, d), lambda i, e: (row_of_tile(i), 0, 0)),
            pl.BlockSpec((1, d, de), lambda i, e: (e, 0, 0)),
            pl.BlockSpec((1, d, de), lambda i, e: (e, 0, 0)),
            pl.BlockSpec((1, de, d), lambda i, e: (e, 0, 0)),
            pl.BlockSpec((1, d), lambda i, e: (0, 0)),
        ],
        out_specs=pl.BlockSpec((tm, d), lambda i, e: (i, 0)),
        out_shape=jax.ShapeDtypeStruct((t, d), F32),
        scratch_shapes=[pltpu.VMEM((tm, d), F32)],
        compiler_params=_cparams(("parallel", "arbitrary")),
        name="moe_final" if final_norm else "moe",
    )(h2, gates, x, g2, pw["wg"], pw["wu"], pw["wd"], pw["fin"])


def _axial_angles(n_tok, rot_dim):
    half = rot_dim // 2
    freqs = ROPE_BASE ** (-jnp.arange(0, half, 2, dtype=F32) / half)
    t = jnp.arange(n_tok, dtype=jnp.int32)
    row = (t // GRID_W).astype(F32)
    col = (t % GRID_W).astype(F32)
    return row[:, None] * freqs, col[:, None] * freqs


def _rope_table(n_tok, rot_dim):
    ar, ac = _axial_angles(n_tok, rot_dim)
    cos = jnp.concatenate([jnp.cos(ar)] * 2 + [jnp.cos(ac)] * 2, axis=1)
    sin = jnp.concatenate([-jnp.sin(ar), jnp.sin(ar), -jnp.sin(ac), jnp.sin(ac)], axis=1)
    return cos, sin


def _rope_tables(n_tok):
    cg, sg = _rope_table(n_tok, GQA_DIM)
    cd, sd = _rope_table(n_tok, DIFF_QK_DIM)
    cm32, sm32 = _rope_table(n_tok, MLA_ROPE)
    ones = jnp.ones((n_tok, MLA_NOPE), F32)
    zeros = jnp.zeros((n_tok, MLA_NOPE), F32)
    pad = LANES - MLA_NOPE - MLA_ROPE
    cm = jnp.concatenate([ones, cm32, ones[:, :pad]], axis=1)
    sm = jnp.concatenate([zeros, sm32, zeros[:, :pad]], axis=1)
    return (jnp.tile(cg, (1, LANES // GQA_DIM)), jnp.tile(sg, (1, LANES // GQA_DIM)),
            jnp.tile(cd, (1, LANES // DIFF_QK_DIM)), jnp.tile(sd, (1, LANES // DIFF_QK_DIM)),
            cm, sm)


def _pack_layer(l, p):
    d = p["w_in"].shape[1]
    w = p["w_in"][l]
    o = np.cumsum([0, 256, 128, 128, 256, 256, 256, 256, 256, 256, MLA_Q_LORA, MLA_KV_LORA, MLA_ROPE])
    col = lambda i: w[:, int(o[i]):int(o[i + 1])]
    dup = lambda a: jnp.concatenate([a[:, :GQA_DIM], a[:, :GQA_DIM], a[:, GQA_DIM:], a[:, GQA_DIM:]], axis=1)
    z = lambda n: jnp.zeros((d, n), F32)
    w_in = jnp.concatenate(
        [col(0), dup(col(1)), dup(col(2)), col(3), col(4), col(5), col(6), col(7), col(8),
         col(9), z(SEG_W - MLA_Q_LORA), col(10),
         z(MLA_NOPE), col(11), z(LANES - MLA_NOPE - MLA_ROPE)], axis=1).astype(BF16)
    dq = MLA_NOPE + MLA_ROPE
    wqu = p["mla_w_q_up"][l].reshape(MLA_Q_LORA, MLA_HEADS, dq)
    wqu = jnp.pad(wqu, ((0, SEG_W - MLA_Q_LORA), (0, 0), (0, LANES - dq)))
    wkv = p["mla_w_kv_up"][l].reshape(MLA_KV_LORA, MLA_HEADS, MLA_NOPE + MLA_V)
    wkk = jnp.pad(wkv[:, :, :MLA_NOPE], ((0, 0), (0, 0), (0, LANES - MLA_NOPE)))
    e = p["moe_w_gate"].shape[1]
    rw = jnp.pad(p["router_w"], ((0, 0), (0, LANES - e)))
    rw_hi = rw.astype(BF16)
    rw_lo = (rw - rw_hi.astype(F32)).astype(BF16)
    return {
        "gmix": p["norm_mix"][l][None], "gffn": p["norm_ffn"][l][None],
        "w_in": w_in,
        "wq_up": wqu.reshape(SEG_W, MLA_HEADS * LANES).astype(BF16),
        "wkv_k": wkk.reshape(MLA_KV_LORA, MLA_HEADS * LANES).astype(BF16),
        "wkv_v": wkv[:, :, MLA_NOPE:].reshape(MLA_KV_LORA, MLA_HEADS * MLA_V).astype(BF16),
        "gq": (jnp.tile(p["gqa_q_norm"][l], GQA_HEADS) * (GQA_DIM ** -0.5 * LOG2E))[None],
        "gk": jnp.tile(p["gqa_k_norm"][l], GQA_HEADS)[None],
        "gcq": jnp.pad(p["mla_q_norm"][l], (0, SEG_W - MLA_Q_LORA))[None],
        "gckv": p["mla_kv_norm"][l][None],
        "w_out": p["w_out"][l].astype(BF16),
        "rw_hi": rw_hi, "rw_lo": rw_lo,
        "rb": jnp.broadcast_to(p["router_b"][:, None], (e, LANES)),
        "wg": jnp.concatenate([p["moe_w_gate"][l], p["shared_w_gate"][l][None]], axis=0).astype(BF16),
        "wu": jnp.concatenate([p["moe_w_up"][l], p["shared_w_up"][l][None]], axis=0).astype(BF16),
        "wd": jnp.concatenate([p["moe_w_down"][l], p["shared_w_down"][l][None]], axis=0).astype(BF16),
        "fin": p["final_norm"][None],
        "diff": (p["diff_lq1"][l][None], p["diff_lk1"][l][None], p["diff_lq2"][l][None],
                 p["diff_lk2"][l][None], p["diff_subln"][l][:, None]),
    }


def kernel(x, c, ctx, c_ctx, w_mod, b_mod, norm_mix, norm_ffn, w_in, w_out, gqa_q_norm, gqa_k_norm,
           na_rel_bias, diff_lq1, diff_lk1, diff_lq2, diff_lk2, diff_subln, mla_q_norm, mla_w_q_up,
           mla_kv_norm, mla_w_kv_up, router_w, router_b, moe_w_gate, moe_w_up, moe_w_down,
           shared_w_gate, shared_w_up, shared_w_down, final_norm):
    p = dict(w_in=w_in, norm_mix=norm_mix, norm_ffn=norm_ffn, w_out=w_out, gqa_q_norm=gqa_q_norm,
             gqa_k_norm=gqa_k_norm, diff_lq1=diff_lq1, diff_lk1=diff_lk1, diff_lq2=diff_lq2,
             diff_lk2=diff_lk2, diff_subln=diff_subln, mla_q_norm=mla_q_norm, mla_w_q_up=mla_w_q_up,
             mla_kv_norm=mla_kv_norm, mla_w_kv_up=mla_w_kv_up, router_w=router_w, router_b=router_b,
             moe_w_gate=moe_w_gate, moe_w_up=moe_w_up, moe_w_down=moe_w_down,
             shared_w_gate=shared_w_gate, shared_w_up=shared_w_up, shared_w_down=shared_w_down,
             final_norm=final_norm)
    b, s, d = x.shape
    nctx = ctx.shape[1]
    depth = w_mod.shape[0]
    rows = s // GRID_W
    assert s % Q_TILE == 0 and s % TOK_TILE == 0 and s // TOK_TILE >= NA_CHUNKS
    assert nctx % LANES == 0 and (b * s) % MOE_TILE == 0 and s % MOE_TILE == 0

    mrows = -(-(b + 1) // SUBLANES) * SUBLANES
    cc = jnp.zeros((mrows, d), F32).at[:b].set(c).at[b].set(c_ctx)
    mod = _modulation(cc, w_mod, b_mod)
    tables = _rope_tables(s)
    lat_row = lambda bi: bi
    ctx_row = lambda bi: b
    tm_ctx = b * nctx if b * nctx <= MOE_TILE else nctx

    xc = ctx
    for l in range(depth):
        want_ctx = l < depth - 1
        lam_init = 0.8 - 0.6 * math.exp(-0.3 * l)
        pw = _pack_layer(l, p)
        m6 = mod[l].reshape(mrows, 6, 1, d)
        sh1, sc1, g1, sh2, sc2, g2 = (m6[:, k] for k in range(6))

        lat = _project(x, sh1, sc1, lat_row, pw, tables, TOK_TILE, LAT_OUTS)
        cx = _project(xc, sh1, sc1, ctx_row, pw, None, nctx, CTX_OUTS if want_ctx else CTX_KV_OUTS)

        def dense(g, kind, extras=()):
            return _flash(lat[g + ".qT"], cx[g + ".k"], cx[g + ".vT"], lat[g + ".k"], lat[g + ".vT"],
                          kind, extras, lam_init)

        def ctx_only(g, kind, extras=()):
            return _flash(cx[g + ".qT"], cx[g + ".k"], cx[g + ".vT"], None, None, kind, extras, lam_init)

        a_lat = dense("A", "pair")
        bias = _na_bias_tiles(na_rel_bias[l], rows, TOK_TILE)
        b_lat = _na_attention(lat["B.q"], cx["B.kT"], cx["B.v"], lat["B.kT"], lat["B.v"], bias)
        c_lat = dense("C", "diff", pw["diff"])
        d_lat = dense("D", "mla")
        x1, h2, gates = _out_router((a_lat, b_lat, c_lat, d_lat), x, g1, sh2, sc2, lat_row, pw, TOK_TILE)
        tiles_per_batch = s // MOE_TILE
        x = _moe(h2.reshape(b * s, d), gates.reshape(b * s, LANES), x1.reshape(b * s, d), g2,
                 lambda i: i // tiles_per_batch, pw, MOE_TILE, not want_ctx).reshape(b, s, d)

        if want_ctx:
            a_c = ctx_only("A", "pair")
            b_c = ctx_only("B", "pair")
            c_c = ctx_only("C", "diff", pw["diff"])
            d_c = ctx_only("D", "mla")
            xc1, h2c, gates_c = _out_router((a_c, b_c, c_c, d_c), xc, g1, sh2, sc2, ctx_row, pw, nctx)
            xc = _moe(h2c.reshape(b * nctx, d), gates_c.reshape(b * nctx, LANES),
                      xc1.reshape(b * nctx, d), g2, lambda i: b, pw, tm_ctx, False).reshape(b, nctx, d)
    return x
```

```python
import functools
import math

import numpy as np
import jax
import jax.numpy as jnp
from jax import lax
from jax.experimental import pallas as pl
from jax.experimental.pallas import tpu as pltpu

F32 = jnp.float32
BF16 = jnp.bfloat16

GRID_W = 64
EPS = 1e-6
ROPE_BASE = 10000.0
GQA_HEADS, GQA_KV_HEADS, GQA_DIM = 4, 2, 64
NA_HEADS, NA_DIM, NA_WIN_ROWS, NA_WIN_COLS = 4, 64, 8, 16
DIFF_HEADS, DIFF_QK_DIM, DIFF_V_DIM = 4, 32, 64
MLA_HEADS, MLA_NOPE, MLA_ROPE, MLA_V, MLA_Q_LORA, MLA_KV_LORA = 4, 64, 32, 64, 192, 128
N_EXPERTS, MOE_GROUPS, D_EXPERT = 16, 4, 256
EXPERTS_PER_GROUP = N_EXPERTS // MOE_GROUPS

LANES = 128
SUBLANES = 8
VMEM_LIMIT = 56 * 1024 * 1024

TOK_TILE = 512
Q_TILE = 512
NA_TILE = 256
MOE_TILE = 1024
MOD_TILE = 1024
NEG = -1e30
LOG2E = math.log2(math.e)

SEG_W = 256
OFF_AQ, OFF_AK, OFF_AV = 0, 256, 512
OFF_BQ, OFF_BK, OFF_BV = 768, 1024, 1280
OFF_CQ, OFF_CK, OFF_CV = 1536, 1792, 2048
OFF_DQ, OFF_DKV, OFF_DPE = 2304, 2560, 2688
IN_PACKED = 2816


def _cparams(sem):
    return pltpu.CompilerParams(dimension_semantics=sem, vmem_limit_bytes=VMEM_LIMIT)


def _lane_iota(shape):
    return lax.broadcasted_iota(jnp.int32, shape, 1)


def _lane_group(shape, width):
    return lax.shift_right_logical(_lane_iota(shape), int(math.log2(width)))


def _sigmoid(x):
    return 1.0 / (1.0 + jnp.exp(-x))


def _mod_kernel(c_ref, w_ref, b_ref, o_ref):
    c = c_ref[...]
    s = c * _sigmoid(c)
    o_ref[0] = jnp.dot(s, w_ref[0], preferred_element_type=F32,
                       precision=lax.Precision.HIGHEST) + b_ref[0]


def _modulation(cc, w_mod, b_mod):
    depth, d, n = w_mod.shape
    rows = cc.shape[0]
    return pl.pallas_call(
        _mod_kernel,
        grid=(depth, n // MOD_TILE),
        in_specs=[
            pl.BlockSpec((rows, d), lambda l, j: (0, 0)),
            pl.BlockSpec((1, d, MOD_TILE), lambda l, j: (l, 0, j)),
            pl.BlockSpec((1, 1, MOD_TILE), lambda l, j: (l, 0, j)),
        ],
        out_specs=pl.BlockSpec((1, rows, MOD_TILE), lambda l, j: (l, 0, j)),
        out_shape=jax.ShapeDtypeStruct((depth, rows, n), F32),
        compiler_params=_cparams(("parallel", "parallel")),
        name="modulation",
    )(cc, w_mod, b_mod.reshape(depth, 1, n))


def _swap_halves(x, half):
    w = x.shape[1]
    lane = _lane_iota(x.shape)
    fwd = pltpu.roll(x, half, 1)
    bwd = pltpu.roll(x, w - half, 1)
    return jnp.where((lane & (2 * half - 1)) < half, bwd, fwd)


def _rope(x, cos, sin, half):
    reps = x.shape[1] // cos.shape[1]
    c = jnp.concatenate([cos] * reps, axis=1) if reps > 1 else cos
    s = jnp.concatenate([sin] * reps, axis=1) if reps > 1 else sin
    return x * c + _swap_halves(x, half) * s


def _group_rms(x, group, count):
    gid = _lane_group(x.shape, group)
    x2 = x * x
    inv = jnp.zeros_like(x)
    for g in range(x.shape[1] // group):
        msk = gid == g
        ms = jnp.sum(jnp.where(msk, x2, 0.0), axis=1, keepdims=True) * (1.0 / count)
        inv = jnp.where(msk, lax.rsqrt(ms + EPS), inv)
    return x * inv


LAT_OUTS = ("A.qT", "A.k", "A.vT", "B.q", "B.kT", "B.v", "C.qT", "C.k", "C.vT", "D.qT", "D.k", "D.vT")
CTX_KV_OUTS = ("A.k", "A.vT", "B.kT", "B.v", "C.k", "C.vT", "D.k", "D.vT")
CTX_OUTS = CTX_KV_OUTS + ("A.qT", "B.qT", "B.k", "B.vT", "C.qT", "D.qT")
GROUP_W = {"A": SEG_W, "B": SEG_W, "C": SEG_W, "D": 2 * SEG_W}


def _proj_kernel(*refs, rope, outs):
    (x_ref, sh_ref, sc_ref, gmix_ref, w_ref, wqu_ref, wkk_ref, wkv_ref,
     gq_ref, gk_ref, gcq_ref, gckv_ref) = refs[:12]
    pos = 12
    if rope:
        cg_ref, sg_ref, cd_ref, sd_ref, cm_ref, sm_ref = refs[pos:pos + 6]
        pos += 6
    out = dict(zip(outs, refs[pos:pos + len(outs)]))

    def want(g, t):
        return (g + "." + t) in out or (g + "." + t + "T") in out

    def put(g, t, val):
        if g + "." + t in out:
            out[g + "." + t][0] = val.astype(BF16)
        if g + "." + t + "T" in out:
            ref = out[g + "." + t + "T"]
            vt = val.T.astype(BF16)
            cw = ref.shape[3]
            for c in range(ref.shape[1]):
                ref[0, c] = vt[:, c * cw:(c + 1) * cw]

    x = x_ref[0]
    ms = jnp.mean(x * x, axis=-1, keepdims=True)
    h = x * lax.rsqrt(ms + EPS) * gmix_ref[...]
    h = h * (1.0 + sc_ref[0]) + sh_ref[0]
    hb = h.astype(BF16)

    def seg(off, width=SEG_W):
        return jnp.dot(hb, w_ref[:, off:off + width], preferred_element_type=F32)

    if want("A", "q"):
        q = _group_rms(seg(OFF_AQ), GQA_DIM, GQA_DIM) * gq_ref[...]
        if rope:
            q = _rope(q, cg_ref[...], sg_ref[...], GQA_DIM // 4)
        put("A", "q", q)
    k = _group_rms(seg(OFF_AK), GQA_DIM, GQA_DIM) * gk_ref[...]
    if rope:
        k = _rope(k, cg_ref[...], sg_ref[...], GQA_DIM // 4)
    put("A", "k", k)
    put("A", "v", seg(OFF_AV))

    if want("B", "q"):
        put("B", "q", seg(OFF_BQ) * (NA_DIM ** -0.5 * LOG2E))
    put("B", "k", seg(OFF_BK))
    put("B", "v", seg(OFF_BV))

    if want("C", "q"):
        q = seg(OFF_CQ) * (DIFF_QK_DIM ** -0.5 * LOG2E)
        if rope:
            q = _rope(q, cd_ref[...], sd_ref[...], DIFF_QK_DIM // 4)
        put("C", "q", q)
    k = seg(OFF_CK)
    if rope:
        k = _rope(k, cd_ref[...], sd_ref[...], DIFF_QK_DIM // 4)
    put("C", "k", k)
    put("C", "v", seg(OFF_CV))

    if want("D", "q"):
        cq = _group_rms(seg(OFF_DQ), SEG_W, MLA_Q_LORA) * gcq_ref[...]
        q = jnp.dot(cq.astype(BF16), wqu_ref[...], preferred_element_type=F32)
        q = q * ((MLA_NOPE + MLA_ROPE) ** -0.5 * LOG2E)
        if rope:
            q = _rope(q, cm_ref[...], sm_ref[...], MLA_ROPE // 4)
        put("D", "q", q)
    ckv = _group_rms(seg(OFF_DKV, LANES), LANES, MLA_KV_LORA) * gckv_ref[...]
    ckvb = ckv.astype(BF16)
    k = jnp.dot(ckvb, wkk_ref[...], preferred_element_type=F32)
    kpe = seg(OFF_DPE, LANES)
    if rope:
        kpe = _rope(kpe, cm_ref[...], sm_ref[...], MLA_ROPE // 4)
    put("D", "k", k + jnp.concatenate([kpe] * MLA_HEADS, axis=1))
    put("D", "v", jnp.dot(ckvb, wkv_ref[...], preferred_element_type=F32))


def _project(x, sh, sc, row_of_batch, pw, tables, tm, outs):
    b, n, d = x.shape
    nt = n // tm
    rope = tables is not None
    full = lambda a: pl.BlockSpec(a.shape, lambda bi, i: (0,) * a.ndim)
    mod_spec = pl.BlockSpec((1, 1, d), lambda bi, i: (row_of_batch(bi), 0, 0))
    ins = [x, sh, sc, pw["gmix"], pw["w_in"], pw["wq_up"], pw["wkv_k"], pw["wkv_v"],
           pw["gq"], pw["gk"], pw["gcq"], pw["gckv"]]
    in_specs = [pl.BlockSpec((1, tm, d), lambda bi, i: (bi, i, 0)), mod_spec, mod_spec]
    in_specs += [full(a) for a in ins[3:]]
    if rope:
        ins += list(tables)
        in_specs += [pl.BlockSpec((tm, LANES), lambda bi, i: (i, 0)) for _ in tables]
    shapes, specs = [], []
    for name in outs:
        g, t = name.split(".")
        w = SEG_W if t[0] == "v" else GROUP_W[g]
        if t.endswith("T"):
            cw = min(tm, NA_TILE) if name == "B.kT" else tm
            shapes.append(jax.ShapeDtypeStruct((b, n // cw, w, cw), BF16))
            specs.append(pl.BlockSpec((1, tm // cw, w, cw), lambda bi, i: (bi, i, 0, 0)))
        else:
            shapes.append(jax.ShapeDtypeStruct((b, n, w), BF16))
            specs.append(pl.BlockSpec((1, tm, w), lambda bi, i: (bi, i, 0)))
    res = pl.pallas_call(
        functools.partial(_proj_kernel, rope=rope, outs=tuple(outs)),
        grid=(b, nt),
        in_specs=in_specs,
        out_specs=specs,
        out_shape=shapes,
        compiler_params=_cparams(("parallel", "parallel")),
        name="proj_rope" if rope else "proj_ctx",
    )(*ins)
    return dict(zip(outs, res))


HEAD_V = 64


def _tile_lanes(x, width):
    reps = width // x.shape[1]
    return jnp.tile(x, (1, reps)) if reps > 1 else x


def _flash_kernel(*refs, kind, has_latent, lam_init):
    qt_ref, kc_ref, vct_ref = refs[:3]
    pos = 3
    if has_latent:
        k_ref, vt_ref = refs[pos:pos + 2]
        pos += 2
    if kind == "diff":
        lq1_ref, lk1_ref, lq2_ref, lk2_ref, sub_ref = refs[pos:pos + 5]
        pos += 5
    o_ref = refs[pos]
    qs_ref, acc_ref = refs[pos + 1:pos + 3]
    if has_latent:
        s_ref = refs[pos + 3]

    nm = qs_ref.shape[0]

    for m in range(nm):
        if kind == "mla":
            qs_ref[m] = qt_ref[0, 0, m * LANES:(m + 1) * LANES, :]
        else:
            qv = qt_ref[0, 0]
            rows = lax.broadcasted_iota(jnp.int32, qv.shape, 0)
            own = lax.shift_right_logical(rows, int(math.log2(LANES // nm))) == m
            qs_ref[m] = jnp.where(own, qv, jnp.zeros_like(qv))

    def k_lanes(m):
        return slice(m * LANES, (m + 1) * LANES) if kind == "mla" else slice(None)

    def v_rows(m):
        hv = m // 2 if kind == "diff" else m
        return slice(hv * HEAD_V, (hv + 1) * HEAD_V)

    def with_ones(vt):
        rows = lax.broadcasted_iota(jnp.int32, (LANES - HEAD_V, vt.shape[1]), 0)
        ones = jnp.where(rows == 0, 1.0, 0.0).astype(BF16)
        return jnp.concatenate([vt, ones], axis=0)

    sts = [jnp.dot(kc_ref[0, :, k_lanes(m)], qs_ref[m], preferred_element_type=F32)
           for m in range(nm)]
    ms = [jnp.max(st, axis=0, keepdims=True) for st in sts]

    def init_from_context():
        for m in range(nm):
            p = jnp.exp2(sts[m] - ms[m]).astype(BF16)
            acc_ref[m] = jnp.dot(with_ones(vct_ref[0, 0, v_rows(m), :]), p,
                                 preferred_element_type=F32)

    if not has_latent:
        init_from_context()

    if has_latent:
        nk, tk = vt_ref.shape[1], vt_ref.shape[3]

        def produce(slot, j, m):
            rows = pl.ds(pl.multiple_of(j * tk, tk), tk)
            st = jnp.dot(k_ref[0, rows, k_lanes(m)], qs_ref[m], preferred_element_type=F32)
            s_ref[slot, m] = st
            return jnp.max(st, axis=0, keepdims=True)

        def consume(slot, j, m, m_prev, m_chunk):
            m_new = jnp.maximum(m_prev, m_chunk)
            alpha = jnp.exp2(m_prev - m_new)
            p = jnp.exp2(s_ref[slot, m] - m_new).astype(BF16)
            acc_ref[m] = alpha * acc_ref[m] + jnp.dot(
                with_ones(vt_ref[0, j, v_rows(m), :]), p, preferred_element_type=F32)
            return m_new

        def pair_of_chunks(i, carry, last):
            ms, mcs = carry
            j0 = 2 * i
            c1 = [produce(1, j0 + 1, m) for m in range(nm)]
            m1 = [consume(0, j0, m, ms[m], mcs[m]) for m in range(nm)]
            c2 = c1 if last else [produce(0, j0 + 2, m) for m in range(nm)]
            m2 = [consume(1, j0 + 1, m, m1[m], c1[m]) for m in range(nm)]
            return tuple(m2), tuple(c2)

        first = tuple(produce(0, 0, m) for m in range(nm))
        init_from_context()
        carry = (tuple(ms), first)
        carry = lax.fori_loop(0, nk // 2 - 1, functools.partial(pair_of_chunks, last=False), carry)
        pair_of_chunks(nk // 2 - 1, carry, True)

    outs = [acc_ref[m, 0:HEAD_V, :] / acc_ref[m, HEAD_V:HEAD_V + 1, :] for m in range(nm)]
    if kind == "diff":
        lam = (jnp.exp(jnp.sum(lq1_ref[...] * lk1_ref[...], axis=1, keepdims=True))
               - jnp.exp(jnp.sum(lq2_ref[...] * lk2_ref[...], axis=1, keepdims=True))
               + lam_init)
        heads = []
        for hd in range(2):
            dlt = outs[2 * hd] - lam * outs[2 * hd + 1]
            ms2 = jnp.mean(dlt * dlt, axis=0, keepdims=True)
            heads.append(dlt * lax.rsqrt(ms2 + EPS) * sub_ref[...] * (1.0 - lam_init))
        outs = heads
    o_ref[0] = jnp.concatenate(outs, axis=0).T.astype(o_ref.dtype)


def _flash(qt, kc, vct, k, vt, kind, extras=(), lam_init=0.0):
    b, nq, wtot, tq = qt.shape
    wq = 2 * LANES if kind == "mla" else LANES
    npair = wtot // wq
    nm = 4 if kind == "diff" else 2
    c = kc.shape[1]
    has_latent = k is not None
    ins = [qt, kc, vct]
    in_specs = [
        pl.BlockSpec((1, 1, wq, tq), lambda bi, p, i: (bi, i, p, 0)),
        pl.BlockSpec((1, c, wq), lambda bi, p, i: (bi, 0, p)),
        pl.BlockSpec((1, 1, LANES, c), lambda bi, p, i: (bi, 0, p, 0)),
    ]
    if has_latent:
        nk, tk = vt.shape[1], vt.shape[3]
        assert nk % 2 == 0, "key chunks are processed in pairs"
        ins += [k, vt]
        in_specs += [
            pl.BlockSpec((1, k.shape[1], wq), lambda bi, p, i: (bi, 0, p)),
            pl.BlockSpec((1, nk, LANES, tk), lambda bi, p, i: (bi, 0, p, 0)),
        ]
    for e in extras:
        ins.append(e)
        in_specs.append(pl.BlockSpec(e.shape, lambda bi, p, i: (0, 0)))
    return pl.pallas_call(
        functools.partial(_flash_kernel, kind=kind, has_latent=has_latent, lam_init=lam_init),
        grid=(b, npair, nq),
        in_specs=in_specs,
        out_specs=pl.BlockSpec((1, tq, LANES), lambda bi, p, i: (bi, i, p)),
        out_shape=jax.ShapeDtypeStruct((b, nq * tq, npair * LANES), BF16),
        scratch_shapes=[
            pltpu.VMEM((nm, LANES, tq), BF16),
            pltpu.VMEM((nm, LANES, tq), F32),
        ] + ([pltpu.VMEM((2, nm, tk, tq), F32)] if has_latent else []),
        compiler_params=_cparams(("parallel", "parallel", "parallel")),
        name="flash_" + kind + ("" if has_latent else "_ctx"),
    )(*ins)


NA_CHUNKS = 3


def _na_kernel(q_ref, kt_ref, v_ref, kc_ref, vc_ref, bias_ref, o_ref):
    tq = q_ref.shape[1]
    nk, tk = kt_ref.shape[1], kt_ref.shape[3]
    i = pl.program_id(2)
    c0 = jnp.clip(i - 1, 0, nk - NA_CHUNKS)
    lane = _lane_iota((tq, LANES))
    qv = q_ref[0]
    outs = []
    for m in range(2):
        qm = jnp.where(_lane_group(qv.shape, LANES // 2) == m, qv, jnp.zeros_like(qv))
        ss = [jnp.dot(qm, kt_ref[0, c0 + j], preferred_element_type=F32)
              + bias_ref[0, m, :, j * tk:(j + 1) * tk] for j in range(NA_CHUNKS)]
        sc = jnp.dot(qm, kc_ref[0, 0], preferred_element_type=F32)
        mx = jnp.max(sc, axis=1, keepdims=True)
        for s in ss:
            mx = jnp.maximum(mx, jnp.max(s, axis=1, keepdims=True))
        pc = jnp.exp2(sc - mx)
        l = jnp.sum(pc, axis=1, keepdims=True)
        acc = jnp.dot(pc.astype(BF16), vc_ref[0], preferred_element_type=F32)
        for j, s in enumerate(ss):
            p = jnp.exp2(s - mx)
            l = l + jnp.sum(p, axis=1, keepdims=True)
            vv = v_ref[0, pl.ds(pl.multiple_of((c0 + j) * tk, tk), tk), :]
            acc = acc + jnp.dot(p.astype(BF16), vv, preferred_element_type=F32)
        outs.append(acc / l)
    o_ref[0] = jnp.where(lane < LANES // 2, outs[0], outs[1]).astype(o_ref.dtype)


def _na_bias_tiles(rel_bias, rows, tk):
    qrows = NA_TILE // GRID_W
    nblk = rows // qrows
    nk = rows * GRID_W // tk
    krows = NA_CHUNKS * tk // GRID_W
    kh = min(NA_WIN_ROWS, rows)
    kw = NA_WIN_COLS
    blocks = np.array([0, 1, nblk - 1])
    c0 = np.clip(blocks - 1, 0, nk - NA_CHUNKS)
    r = blocks[:, None] * qrows + np.arange(qrows)[None, :]
    kr = c0[:, None] * (tk // GRID_W) + np.arange(krows)[None, :]
    r0 = np.clip(r - kh // 2, 0, rows - kh)
    dr = kr[:, None, :] - r[:, :, None]
    valid_r = (kr[:, None, :] >= r0[:, :, None]) & (kr[:, None, :] < r0[:, :, None] + kh)
    cols = np.arange(GRID_W)
    cstart = np.clip(cols - kw // 2, 0, GRID_W - kw)
    dc = cols[None, :] - cols[:, None]
    valid_c = (cols[None, :] >= cstart[:, None]) & (cols[None, :] < cstart[:, None] + kw)
    ndc = 2 * NA_WIN_COLS - 1
    onehot = ((dc[None] + NA_WIN_COLS - 1) == np.arange(ndc)[:, None, None]) & valid_c[None]
    dr_idx = np.clip(dr + NA_WIN_ROWS - 1, 0, 2 * NA_WIN_ROWS - 2)
    g1 = rel_bias[:, dr_idx, :]
    tiles = jnp.einsum("htabd,dcx->thacbx", g1, jnp.asarray(onehot, F32),
                       precision=lax.Precision.HIGHEST)
    valid = valid_r[:, None, :, None, :, None] & valid_c[None, None, None, :, None, :]
    tiles = jnp.where(jnp.asarray(valid), tiles * LOG2E, NEG)
    return tiles.reshape(3, rel_bias.shape[0], qrows * GRID_W, krows * GRID_W)


def _na_attention(q, ktc, vc, kt, v, bias):
    b, n, wtot = q.shape
    npair = wtot // LANES
    nk, tk = kt.shape[1], kt.shape[3]
    c = vc.shape[1]
    nblk = n // NA_TILE

    def bias_map(bi, p, i):
        return (jnp.where(i == 0, 0, jnp.where(i == nblk - 1, 2, 1)), p, 0, 0)

    return pl.pallas_call(
        _na_kernel,
        grid=(b, npair, nblk),
        in_specs=[
            pl.BlockSpec((1, NA_TILE, LANES), lambda bi, p, i: (bi, i, p)),
            pl.BlockSpec((1, nk, LANES, tk), lambda bi, p, i: (bi, 0, p, 0)),
            pl.BlockSpec((1, n, LANES), lambda bi, p, i: (bi, 0, p)),
            pl.BlockSpec((1, 1, LANES, c), lambda bi, p, i: (bi, 0, p, 0)),
            pl.BlockSpec((1, c, LANES), lambda bi, p, i: (bi, 0, p)),
            pl.BlockSpec((1, 2, NA_TILE, NA_CHUNKS * tk), bias_map),
        ],
        out_specs=pl.BlockSpec((1, NA_TILE, LANES), lambda bi, p, i: (bi, i, p)),
        out_shape=jax.ShapeDtypeStruct((b, n, wtot), BF16),
        compiler_params=_cparams(("parallel", "parallel", "parallel")),
        name="na_attention",
    )(q, kt, v, ktc, vc, bias)


def _first_index_of_max(vals, row, big):
    mx = jnp.max(vals, axis=0, keepdims=True)
    idx = jnp.min(jnp.where(vals == mx, row, big), axis=0, keepdims=True)
    return mx, idx


def _router_gates(scores_t, sel_t):
    irow = lax.broadcasted_iota(jnp.int32, sel_t.shape, 0)
    row = irow.astype(F32)
    grp = lax.shift_right_logical(irow, int(math.log2(EXPERTS_PER_GROUP))).astype(F32)
    big = float(N_EXPERTS)
    best = None
    for g in range(MOE_GROUPS):
        vals = jnp.where(grp == float(g), sel_t, -jnp.inf)
        m1, i1 = _first_index_of_max(vals, row, big)
        m2 = jnp.max(jnp.where(row == i1, -jnp.inf, vals), axis=0, keepdims=True)
        gs = m1 + m2
        if best is None:
            best, bi = gs, jnp.zeros(gs.shape, F32)
        else:
            better = gs > best
            bi = jnp.where(better, float(g), bi)
            best = jnp.where(better, gs, best)
    msel = jnp.where(grp == bi, sel_t, -jnp.inf)
    _, i1 = _first_index_of_max(msel, row, big)
    msel2 = jnp.where(row == i1, -jnp.inf, msel)
    _, i2 = _first_index_of_max(msel2, row, big)
    w = jnp.where((row == i1) | (row == i2), scores_t, 0.0)
    return w / jnp.sum(w, axis=0, keepdims=True)


def _out_kernel(a_ref, b_ref, c_ref, d_ref, x_ref, g1_ref, sh_ref, sc_ref, gffn_ref,
                w_ref, rwh_ref, rwl_ref, rb_ref, xo_ref, h2_ref, gate_ref):
    y = None
    for gi, r in enumerate((a_ref, b_ref, c_ref, d_ref)):
        part = jnp.dot(r[0], w_ref[gi * SEG_W:(gi + 1) * SEG_W, :], preferred_element_type=F32)
        y = part if y is None else y + part
    xn = x_ref[0] + g1_ref[0] * y
    xo_ref[0] = xn
    ms = jnp.mean(xn * xn, axis=-1, keepdims=True)
    h2 = xn * lax.rsqrt(ms + EPS) * gffn_ref[...]
    h2 = h2 * (1.0 + sc_ref[0]) + sh_ref[0]
    hi = h2.astype(BF16)
    h2_ref[0] = hi
    lo = (h2 - hi.astype(F32)).astype(BF16)
    logits = (jnp.dot(hi, rwh_ref[...], preferred_element_type=F32)
              + jnp.dot(lo, rwh_ref[...], preferred_element_type=F32)
              + jnp.dot(hi, rwl_ref[...], preferred_element_type=F32))
    tm = logits.shape[0]
    scores_t = _sigmoid(logits).T[:N_EXPERTS]
    sel_t = scores_t + _tile_lanes(rb_ref[...], tm)
    gates_t = _router_gates(scores_t, sel_t)
    pad = jnp.zeros((LANES - N_EXPERTS, tm), F32)
    gate_ref[0] = jnp.concatenate([gates_t, pad], axis=0).T


def _out_router(parts, x, g1, sh2, sc2, row_of_batch, pw, tm):
    b, n, d = x.shape
    full = lambda a: pl.BlockSpec(a.shape, lambda bi, i: (0,) * a.ndim)
    tok = lambda w: pl.BlockSpec((1, tm, w), lambda bi, i: (bi, i, 0))
    mod_spec = pl.BlockSpec((1, 1, d), lambda bi, i: (row_of_batch(bi), 0, 0))
    consts = [pw["gffn"], pw["w_out"], pw["rw_hi"], pw["rw_lo"], pw["rb"]]
    return pl.pallas_call(
        _out_kernel,
        grid=(b, n // tm),
        in_specs=[tok(SEG_W)] * 4 + [tok(d), mod_spec, mod_spec, mod_spec] + [full(a) for a in consts],
        out_specs=[tok(d), tok(d), tok(LANES)],
        out_shape=[jax.ShapeDtypeStruct((b, n, d), F32), jax.ShapeDtypeStruct((b, n, d), BF16),
                   jax.ShapeDtypeStruct((b, n, LANES), F32)],
        compiler_params=_cparams(("parallel", "parallel")),
        name="out_router",
    )(*parts, x, g1, sh2, sc2, *consts)


MOE_EXPERTS_PER_STEP = 2


def _swiglu_act(gu, scale=None):
    de = gu.shape[1] // 2
    g, u = gu[:, :de], gu[:, de:]
    act = (g * _sigmoid(g)) * u
    return (act if scale is None else act * scale).astype(BF16)


def _moe_kernel(h_ref, gate_ref, x_ref, g2_ref, wgu_ref, wd_ref, sgu_ref, sd_ref, fin_ref,
                o_ref, acc_ref, *, final_norm):
    step = pl.program_id(1)
    k = wgu_ref.shape[0]
    hb = h_ref[...]
    gates = gate_ref[...]
    lane = _lane_iota(gates.shape)
    acts = []
    for i in range(k):
        gu = jnp.dot(hb, wgu_ref[i], preferred_element_type=F32)
        col = jnp.sum(jnp.where(lane == step * k + i, gates, 0.0), axis=1, keepdims=True)
        acts.append(_swiglu_act(gu, col))
    wd = wd_ref[...]
    contrib = jnp.dot(jnp.concatenate(acts, axis=1), wd.reshape(k * wd.shape[1], wd.shape[2]),
                      preferred_element_type=F32)

    @pl.when(step == 0)
    def _():
        shared = _swiglu_act(jnp.dot(hb, sgu_ref[...], preferred_element_type=F32))
        acc_ref[...] = contrib + jnp.dot(shared, sd_ref[...], preferred_element_type=F32)

    @pl.when(step > 0)
    def _():
        acc_ref[...] += contrib

    @pl.when(step == pl.num_programs(1) - 1)
    def _():
        y = x_ref[...] + g2_ref[0] * acc_ref[...]
        if final_norm:
            ms = jnp.mean(y * y, axis=-1, keepdims=True)
            y = y * lax.rsqrt(ms + EPS) * fin_ref[...]
        o_ref[...] = y


def _moe(h2, gates, x, g2, row_of_tile, pw, tm, final_norm):
    t, d = x.shape
    ne, _, de2 = pw["wgu"].shape
    k = MOE_EXPERTS_PER_STEP
    full = lambda a: pl.BlockSpec(a.shape, lambda i, e: (0,) * a.ndim)
    return pl.pallas_call(
        functools.partial(_moe_kernel, final_norm=final_norm),
        grid=(t // tm, ne // k),
        in_specs=[
            pl.BlockSpec((tm, d), lambda i, e: (i, 0)),
            pl.BlockSpec((tm, LANES), lambda i, e: (i, 0)),
            pl.BlockSpec((tm, d), lambda i, e: (i, 0)),
            pl.BlockSpec((1, 1, d), lambda i, e: (row_of_tile(i), 0, 0)),
            pl.BlockSpec((k, d, de2), lambda i, e: (e, 0, 0)),
            pl.BlockSpec((k, de2 // 2, d), lambda i, e: (e, 0, 0)),
            full(pw["sgu"]), full(pw["sd"]), full(pw["fin"]),
        ],
        out_specs=pl.BlockSpec((tm, d), lambda i, e: (i, 0)),
        out_shape=jax.ShapeDtypeStruct((t, d), F32),
        scratch_shapes=[pltpu.VMEM((tm, d), F32)],
        compiler_params=_cparams(("parallel", "arbitrary")),
        name="moe_final" if final_norm else "moe",
    )(h2, gates, x, g2, pw["wgu"], pw["wd"], pw["sgu"], pw["sd"], pw["fin"])


def _axial_angles(n_tok, rot_dim):
    half = rot_dim // 2
    freqs = ROPE_BASE ** (-jnp.arange(0, half, 2, dtype=F32) / half)
    t = jnp.arange(n_tok, dtype=jnp.int32)
    row = (t // GRID_W).astype(F32)
    col = (t % GRID_W).astype(F32)
    return row[:, None] * freqs, col[:, None] * freqs


def _rope_table(n_tok, rot_dim):
    ar, ac = _axial_angles(n_tok, rot_dim)
    cos = jnp.concatenate([jnp.cos(ar)] * 2 + [jnp.cos(ac)] * 2, axis=1)
    sin = jnp.concatenate([-jnp.sin(ar), jnp.sin(ar), -jnp.sin(ac), jnp.sin(ac)], axis=1)
    return cos, sin


def _rope_tables(n_tok):
    cg, sg = _rope_table(n_tok, GQA_DIM)
    cd, sd = _rope_table(n_tok, DIFF_QK_DIM)
    cm32, sm32 = _rope_table(n_tok, MLA_ROPE)
    ones = jnp.ones((n_tok, MLA_NOPE), F32)
    zeros = jnp.zeros((n_tok, MLA_NOPE), F32)
    pad = LANES - MLA_NOPE - MLA_ROPE
    cm = jnp.concatenate([ones, cm32, ones[:, :pad]], axis=1)
    sm = jnp.concatenate([zeros, sm32, zeros[:, :pad]], axis=1)
    return (jnp.tile(cg, (1, LANES // GQA_DIM)), jnp.tile(sg, (1, LANES // GQA_DIM)),
            jnp.tile(cd, (1, LANES // DIFF_QK_DIM)), jnp.tile(sd, (1, LANES // DIFF_QK_DIM)),
            cm, sm)


def _pack_layer(l, p):
    d = p["w_in"].shape[1]
    w = p["w_in"][l]
    o = np.cumsum([0, 256, 128, 128, 256, 256, 256, 256, 256, 256, MLA_Q_LORA, MLA_KV_LORA, MLA_ROPE])
    col = lambda i: w[:, int(o[i]):int(o[i + 1])]
    dup = lambda a: jnp.concatenate([a[:, :GQA_DIM], a[:, :GQA_DIM], a[:, GQA_DIM:], a[:, GQA_DIM:]], axis=1)
    z = lambda n: jnp.zeros((d, n), F32)
    w_in = jnp.concatenate(
        [col(0), dup(col(1)), dup(col(2)), col(3), col(4), col(5), col(6), col(7), col(8),
         col(9), z(SEG_W - MLA_Q_LORA), col(10),
         z(MLA_NOPE), col(11), z(LANES - MLA_NOPE - MLA_ROPE)], axis=1).astype(BF16)
    dq = MLA_NOPE + MLA_ROPE
    wqu = p["mla_w_q_up"][l].reshape(MLA_Q_LORA, MLA_HEADS, dq)
    wqu = jnp.pad(wqu, ((0, SEG_W - MLA_Q_LORA), (0, 0), (0, LANES - dq)))
    wkv = p["mla_w_kv_up"][l].reshape(MLA_KV_LORA, MLA_HEADS, MLA_NOPE + MLA_V)
    wkk = jnp.pad(wkv[:, :, :MLA_NOPE], ((0, 0), (0, 0), (0, LANES - MLA_NOPE)))
    e = p["moe_w_gate"].shape[1]
    rw = jnp.pad(p["router_w"], ((0, 0), (0, LANES - e)))
    rw_hi = rw.astype(BF16)
    rw_lo = (rw - rw_hi.astype(F32)).astype(BF16)
    return {
        "gmix": p["norm_mix"][l][None], "gffn": p["norm_ffn"][l][None],
        "w_in": w_in,
        "wq_up": wqu.reshape(SEG_W, MLA_HEADS * LANES).astype(BF16),
        "wkv_k": wkk.reshape(MLA_KV_LORA, MLA_HEADS * LANES).astype(BF16),
        "wkv_v": wkv[:, :, MLA_NOPE:].reshape(MLA_KV_LORA, MLA_HEADS * MLA_V).astype(BF16),
        "gq": (jnp.tile(p["gqa_q_norm"][l], GQA_HEADS) * (GQA_DIM ** -0.5 * LOG2E))[None],
        "gk": jnp.tile(p["gqa_k_norm"][l], GQA_HEADS)[None],
        "gcq": jnp.pad(p["mla_q_norm"][l], (0, SEG_W - MLA_Q_LORA))[None],
        "gckv": p["mla_kv_norm"][l][None],
        "w_out": p["w_out"][l].astype(BF16),
        "rw_hi": rw_hi, "rw_lo": rw_lo,
        "rb": jnp.broadcast_to(p["router_b"][:, None], (e, LANES)),
        "wgu": jnp.concatenate([p["moe_w_gate"][l], p["moe_w_up"][l]], axis=2).astype(BF16),
        "wd": p["moe_w_down"][l].astype(BF16),
        "sgu": jnp.concatenate([p["shared_w_gate"][l], p["shared_w_up"][l]], axis=1).astype(BF16),
        "sd": p["shared_w_down"][l].astype(BF16),
        "fin": p["final_norm"][None],
        "diff": (p["diff_lq1"][l][None], p["diff_lk1"][l][None], p["diff_lq2"][l][None],
                 p["diff_lk2"][l][None], p["diff_subln"][l][:, None]),
    }


def kernel(x, c, ctx, c_ctx, w_mod, b_mod, norm_mix, norm_ffn, w_in, w_out, gqa_q_norm, gqa_k_norm,
           na_rel_bias, diff_lq1, diff_lk1, diff_lq2, diff_lk2, diff_subln, mla_q_norm, mla_w_q_up,
           mla_kv_norm, mla_w_kv_up, router_w, router_b, moe_w_gate, moe_w_up, moe_w_down,
           shared_w_gate, shared_w_up, shared_w_down, final_norm):
    p = dict(w_in=w_in, norm_mix=norm_mix, norm_ffn=norm_ffn, w_out=w_out, gqa_q_norm=gqa_q_norm,
             gqa_k_norm=gqa_k_norm, diff_lq1=diff_lq1, diff_lk1=diff_lk1, diff_lq2=diff_lq2,
             diff_lk2=diff_lk2, diff_subln=diff_subln, mla_q_norm=mla_q_norm, mla_w_q_up=mla_w_q_up,
             mla_kv_norm=mla_kv_norm, mla_w_kv_up=mla_w_kv_up, router_w=router_w, router_b=router_b,
             moe_w_gate=moe_w_gate, moe_w_up=moe_w_up, moe_w_down=moe_w_down,
             shared_w_gate=shared_w_gate, shared_w_up=shared_w_up, shared_w_down=shared_w_down,
             final_norm=final_norm)
    b, s, d = x.shape
    nctx = ctx.shape[1]
    depth = w_mod.shape[0]
    rows = s // GRID_W
    assert s % Q_TILE == 0 and s % TOK_TILE == 0 and s // NA_TILE >= NA_CHUNKS
    assert nctx % LANES == 0 and (b * s) % MOE_TILE == 0 and s % MOE_TILE == 0

    mrows = -(-(b + 1) // SUBLANES) * SUBLANES
    cc = jnp.zeros((mrows, d), F32).at[:b].set(c).at[b].set(c_ctx)
    mod = _modulation(cc, w_mod, b_mod)
    tables = _rope_tables(s)
    lat_row = lambda bi: bi
    ctx_row = lambda bi: b
    tm_ctx = b * nctx if b * nctx <= MOE_TILE else nctx

    xc = ctx
    for l in range(depth):
        want_ctx = l < depth - 1
        lam_init = 0.8 - 0.6 * math.exp(-0.3 * l)
        pw = _pack_layer(l, p)
        m6 = mod[l].reshape(mrows, 6, 1, d)
        sh1, sc1, g1, sh2, sc2, g2 = (m6[:, k] for k in range(6))

        lat = _project(x, sh1, sc1, lat_row, pw, tables, TOK_TILE, LAT_OUTS)
        cx = _project(xc, sh1, sc1, ctx_row, pw, None, nctx, CTX_OUTS if want_ctx else CTX_KV_OUTS)

        def dense(g, kind, extras=()):
            return _flash(lat[g + ".qT"], cx[g + ".k"], cx[g + ".vT"], lat[g + ".k"], lat[g + ".vT"],
                          kind, extras, lam_init)

        def ctx_only(g, kind, extras=()):
            return _flash(cx[g + ".qT"], cx[g + ".k"], cx[g + ".vT"], None, None, kind, extras, lam_init)

        a_lat = dense("A", "pair")
        bias = _na_bias_tiles(na_rel_bias[l], rows, NA_TILE)
        b_lat = _na_attention(lat["B.q"], cx["B.kT"], cx["B.v"], lat["B.kT"], lat["B.v"], bias)
        c_lat = dense("C", "diff", pw["diff"])
        d_lat = dense("D", "mla")
        x1, h2, gates = _out_router((a_lat, b_lat, c_lat, d_lat), x, g1, sh2, sc2, lat_row, pw, TOK_TILE)
        tiles_per_batch = s // MOE_TILE
        x = _moe(h2.reshape(b * s, d), gates.reshape(b * s, LANES), x1.reshape(b * s, d), g2,
                 lambda i: i // tiles_per_batch, pw, MOE_TILE, not want_ctx).reshape(b, s, d)

        if want_ctx:
            a_c = ctx_only("A", "pair")
            b_c = ctx_only("B", "pair")
            c_c = ctx_only("C", "diff", pw["diff"])
            d_c = ctx_only("D", "mla")
            xc1, h2c, gates_c = _out_router((a_c, b_c, c_c, d_c), xc, g1, sh2, sc2, ctx_row, pw, nctx)
            xc = _moe(h2c.reshape(b * nctx, d), gates_c.reshape(b * nctx, LANES),
                      xc1.reshape(b * nctx, d), g2, lambda i: b, pw, tm_ctx, False).reshape(b, nctx, d)
    return x
```

```python
import functools
import math

import numpy as np
import jax
import jax.numpy as jnp
from jax import lax
from jax.experimental import pallas as pl
from jax.experimental.pallas import tpu as pltpu

F32 = jnp.float32
BF16 = jnp.bfloat16

GRID_W = 64
EPS = 1e-6
ROPE_BASE = 10000.0
GQA_HEADS, GQA_KV_HEADS, GQA_DIM = 4, 2, 64
NA_HEADS, NA_DIM, NA_WIN_ROWS, NA_WIN_COLS = 4, 64, 8, 16
DIFF_HEADS, DIFF_QK_DIM, DIFF_V_DIM = 4, 32, 64
MLA_HEADS, MLA_NOPE, MLA_ROPE, MLA_V, MLA_Q_LORA, MLA_KV_LORA = 4, 64, 32, 64, 192, 128
N_EXPERTS, MOE_GROUPS, D_EXPERT = 16, 4, 256
EXPERTS_PER_GROUP = N_EXPERTS // MOE_GROUPS

LANES = 128
SUBLANES = 8
VMEM_LIMIT = 56 * 1024 * 1024

TOK_TILE = 512
Q_TILE = 1024
NA_TILE = 256
MOE_TILE = 1024
MOD_TILE = 1024
NEG = -1e30
LOG2E = math.log2(math.e)

SEG_W = 256
OFF_AQ, OFF_AK, OFF_AV = 0, 256, 512
OFF_BQ, OFF_BK, OFF_BV = 768, 1024, 1280
OFF_CQ, OFF_CK, OFF_CV = 1536, 1792, 2048
OFF_DQ, OFF_DKV, OFF_DPE = 2304, 2560, 2688
IN_PACKED = 2816


def _cparams(sem):
    return pltpu.CompilerParams(dimension_semantics=sem, vmem_limit_bytes=VMEM_LIMIT)


def _lane_iota(shape):
    return lax.broadcasted_iota(jnp.int32, shape, 1)


def _lane_group(shape, width):
    return lax.shift_right_logical(_lane_iota(shape), int(math.log2(width)))


def _sigmoid(x):
    return 1.0 / (1.0 + jnp.exp(-x))


def _mod_kernel(c_ref, w_ref, b_ref, o_ref):
    c = c_ref[...]
    s = c * _sigmoid(c)
    o_ref[0] = jnp.dot(s, w_ref[0], preferred_element_type=F32,
                       precision=lax.Precision.HIGHEST) + b_ref[0]


def _modulation(cc, w_mod, b_mod):
    depth, d, n = w_mod.shape
    rows = cc.shape[0]
    return pl.pallas_call(
        _mod_kernel,
        grid=(depth, n // MOD_TILE),
        in_specs=[
            pl.BlockSpec((rows, d), lambda l, j: (0, 0)),
            pl.BlockSpec((1, d, MOD_TILE), lambda l, j: (l, 0, j)),
            pl.BlockSpec((1, 1, MOD_TILE), lambda l, j: (l, 0, j)),
        ],
        out_specs=pl.BlockSpec((1, rows, MOD_TILE), lambda l, j: (l, 0, j)),
        out_shape=jax.ShapeDtypeStruct((depth, rows, n), F32),
        compiler_params=_cparams(("parallel", "parallel")),
        name="modulation",
    )(cc, w_mod, b_mod.reshape(depth, 1, n))


def _swap_halves(x, half):
    w = x.shape[1]
    lane = _lane_iota(x.shape)
    fwd = pltpu.roll(x, half, 1)
    bwd = pltpu.roll(x, w - half, 1)
    return jnp.where((lane & (2 * half - 1)) < half, bwd, fwd)


def _rope(x, cos, sin, half):
    reps = x.shape[1] // cos.shape[1]
    c = jnp.concatenate([cos] * reps, axis=1) if reps > 1 else cos
    s = jnp.concatenate([sin] * reps, axis=1) if reps > 1 else sin
    return x * c + _swap_halves(x, half) * s


def _group_rms(x, group, count):
    gid = _lane_group(x.shape, group)
    x2 = x * x
    inv = jnp.zeros_like(x)
    for g in range(x.shape[1] // group):
        msk = gid == g
        ms = jnp.sum(jnp.where(msk, x2, 0.0), axis=1, keepdims=True) * (1.0 / count)
        inv = jnp.where(msk, lax.rsqrt(ms + EPS), inv)
    return x * inv


LAT_OUTS = ("A.qT", "A.k", "A.vT", "B.qT", "B.k", "B.vT", "C.qT", "C.k", "C.vT", "D.qT", "D.k", "D.vT")
CTX_KV_OUTS = ("A.k", "A.vT", "B.k", "B.vT", "C.k", "C.vT", "D.k", "D.vT")
CTX_OUTS = CTX_KV_OUTS + ("A.qT", "B.qT", "C.qT", "D.qT")
GROUP_W = {"A": SEG_W, "B": SEG_W, "C": SEG_W, "D": 2 * SEG_W}


def _proj_kernel(*refs, rope, outs):
    (x_ref, sh_ref, sc_ref, gmix_ref, w_ref, wqu_ref, wkk_ref, wkv_ref,
     gq_ref, gk_ref, gcq_ref, gckv_ref) = refs[:12]
    pos = 12
    if rope:
        cg_ref, sg_ref, cd_ref, sd_ref, cm_ref, sm_ref = refs[pos:pos + 6]
        pos += 6
    out = dict(zip(outs, refs[pos:pos + len(outs)]))

    def want(g, t):
        return (g + "." + t) in out or (g + "." + t + "T") in out

    def put(g, t, val):
        if g + "." + t in out:
            out[g + "." + t][0] = val.astype(BF16)
        if g + "." + t + "T" in out:
            ref = out[g + "." + t + "T"]
            vt = val.T.astype(BF16)
            cw = ref.shape[3]
            for c in range(ref.shape[1]):
                ref[0, c] = vt[:, c * cw:(c + 1) * cw]

    x = x_ref[0]
    ms = jnp.mean(x * x, axis=-1, keepdims=True)
    h = x * lax.rsqrt(ms + EPS) * gmix_ref[...]
    h = h * (1.0 + sc_ref[0]) + sh_ref[0]
    hb = h.astype(BF16)

    def seg(off, width=SEG_W):
        return jnp.dot(hb, w_ref[:, off:off + width], preferred_element_type=F32)

    if want("A", "q"):
        q = _group_rms(seg(OFF_AQ), GQA_DIM, GQA_DIM) * gq_ref[...]
        if rope:
            q = _rope(q, cg_ref[...], sg_ref[...], GQA_DIM // 4)
        put("A", "q", q)
    k = _group_rms(seg(OFF_AK), GQA_DIM, GQA_DIM) * gk_ref[...]
    if rope:
        k = _rope(k, cg_ref[...], sg_ref[...], GQA_DIM // 4)
    put("A", "k", k)
    put("A", "v", seg(OFF_AV))

    if want("B", "q"):
        put("B", "q", seg(OFF_BQ) * (NA_DIM ** -0.5 * LOG2E))
    put("B", "k", seg(OFF_BK))
    put("B", "v", seg(OFF_BV))

    if want("C", "q"):
        q = seg(OFF_CQ) * (DIFF_QK_DIM ** -0.5 * LOG2E)
        if rope:
            q = _rope(q, cd_ref[...], sd_ref[...], DIFF_QK_DIM // 4)
        put("C", "q", q)
    k = seg(OFF_CK)
    if rope:
        k = _rope(k, cd_ref[...], sd_ref[...], DIFF_QK_DIM // 4)
    put("C", "k", k)
    put("C", "v", seg(OFF_CV))

    if want("D", "q"):
        cq = _group_rms(seg(OFF_DQ), SEG_W, MLA_Q_LORA) * gcq_ref[...]
        q = jnp.dot(cq.astype(BF16), wqu_ref[...], preferred_element_type=F32)
        q = q * ((MLA_NOPE + MLA_ROPE) ** -0.5 * LOG2E)
        if rope:
            q = _rope(q, cm_ref[...], sm_ref[...], MLA_ROPE // 4)
        put("D", "q", q)
    ckv = _group_rms(seg(OFF_DKV, LANES), LANES, MLA_KV_LORA) * gckv_ref[...]
    ckvb = ckv.astype(BF16)
    k = jnp.dot(ckvb, wkk_ref[...], preferred_element_type=F32)
    kpe = seg(OFF_DPE, LANES)
    if rope:
        kpe = _rope(kpe, cm_ref[...], sm_ref[...], MLA_ROPE // 4)
    put("D", "k", k + jnp.concatenate([kpe] * MLA_HEADS, axis=1))
    put("D", "v", jnp.dot(ckvb, wkv_ref[...], preferred_element_type=F32))


def _project(x, sh, sc, row_of_batch, pw, tables, tm, outs):
    b, n, d = x.shape
    nt = n // tm
    rope = tables is not None
    full = lambda a: pl.BlockSpec(a.shape, lambda bi, i: (0,) * a.ndim)
    mod_spec = pl.BlockSpec((1, 1, d), lambda bi, i: (row_of_batch(bi), 0, 0))
    ins = [x, sh, sc, pw["gmix"], pw["w_in"], pw["wq_up"], pw["wkv_k"], pw["wkv_v"],
           pw["gq"], pw["gk"], pw["gcq"], pw["gckv"]]
    in_specs = [pl.BlockSpec((1, tm, d), lambda bi, i: (bi, i, 0)), mod_spec, mod_spec]
    in_specs += [full(a) for a in ins[3:]]
    if rope:
        ins += list(tables)
        in_specs += [pl.BlockSpec((tm, LANES), lambda bi, i: (i, 0)) for _ in tables]
    shapes, specs = [], []
    for name in outs:
        g, t = name.split(".")
        w = SEG_W if t[0] == "v" else GROUP_W[g]
        if t.endswith("T"):
            cw = min(tm, NA_TILE) if g == "B" else tm
            shapes.append(jax.ShapeDtypeStruct((b, n // cw, w, cw), BF16))
            specs.append(pl.BlockSpec((1, tm // cw, w, cw), lambda bi, i: (bi, i, 0, 0)))
        else:
            shapes.append(jax.ShapeDtypeStruct((b, n, w), BF16))
            specs.append(pl.BlockSpec((1, tm, w), lambda bi, i: (bi, i, 0)))
    res = pl.pallas_call(
        functools.partial(_proj_kernel, rope=rope, outs=tuple(outs)),
        grid=(b, nt),
        in_specs=in_specs,
        out_specs=specs,
        out_shape=shapes,
        compiler_params=_cparams(("parallel", "parallel")),
        name="proj_rope" if rope else "proj_ctx",
    )(*ins)
    return dict(zip(outs, res))


HEAD_V = 64


def _tile_lanes(x, width):
    reps = width // x.shape[1]
    return jnp.tile(x, (1, reps)) if reps > 1 else x


def with_ones(vt):
    rows = lax.broadcasted_iota(jnp.int32, (LANES - HEAD_V, vt.shape[1]), 0)
    ones = jnp.where(rows == 0, 1.0, 0.0).astype(BF16)
    return jnp.concatenate([vt, ones], axis=0)


def _flash_kernel(*refs, kind, has_latent, lam_init):
    qt_ref, kc_ref, vct_ref = refs[:3]
    pos = 3
    if has_latent:
        k_ref, vt_ref = refs[pos:pos + 2]
        pos += 2
    if kind == "diff":
        lq1_ref, lk1_ref, lq2_ref, lk2_ref, sub_ref = refs[pos:pos + 5]
        pos += 5
    o_ref = refs[pos]
    qs_ref, acc_ref = refs[pos + 1:pos + 3]
    if has_latent:
        s_ref = refs[pos + 3]

    nm = qs_ref.shape[0]

    qv = jnp.concatenate([qt_ref[0, c] for c in range(qt_ref.shape[1])], axis=1)
    for m in range(nm):
        if kind == "mla":
            qs_ref[m] = qv[m * LANES:(m + 1) * LANES, :]
        else:
            rows = lax.broadcasted_iota(jnp.int32, qv.shape, 0)
            own = lax.shift_right_logical(rows, int(math.log2(LANES // nm))) == m
            qs_ref[m] = jnp.where(own, qv, jnp.zeros_like(qv))

    def k_lanes(m):
        return slice(m * LANES, (m + 1) * LANES) if kind == "mla" else slice(None)

    def v_rows(m):
        hv = m // 2 if kind == "diff" else m
        return slice(hv * HEAD_V, (hv + 1) * HEAD_V)

    sts = [jnp.dot(kc_ref[0, :, k_lanes(m)], qs_ref[m], preferred_element_type=F32)
           for m in range(nm)]
    ms = [jnp.max(st, axis=0, keepdims=True) for st in sts]

    def init_from_context():
        for m in range(nm):
            p = jnp.exp2(sts[m] - ms[m]).astype(BF16)
            acc_ref[m] = jnp.dot(with_ones(vct_ref[0, 0, v_rows(m), :]), p,
                                 preferred_element_type=F32)

    if not has_latent:
        init_from_context()

    if has_latent:
        nk, tk = vt_ref.shape[1], vt_ref.shape[3]

        def produce(slot, j, m):
            rows = pl.ds(pl.multiple_of(j * tk, tk), tk)
            st = jnp.dot(k_ref[0, rows, k_lanes(m)], qs_ref[m], preferred_element_type=F32)
            s_ref[slot, m] = st
            return jnp.max(st, axis=0, keepdims=True)

        def consume(slot, j, m, m_prev, m_chunk):
            m_new = jnp.maximum(m_prev, m_chunk)
            alpha = jnp.exp2(m_prev - m_new)
            p = jnp.exp2(s_ref[slot, m] - m_new).astype(BF16)
            acc_ref[m] = alpha * acc_ref[m] + jnp.dot(
                with_ones(vt_ref[0, j, v_rows(m), :]), p, preferred_element_type=F32)
            return m_new

        def pair_of_chunks(i, carry, last):
            ms, mcs = carry
            j0 = 2 * i
            c1 = [produce(1, j0 + 1, m) for m in range(nm)]
            m1 = [consume(0, j0, m, ms[m], mcs[m]) for m in range(nm)]
            c2 = c1 if last else [produce(0, j0 + 2, m) for m in range(nm)]
            m2 = [consume(1, j0 + 1, m, m1[m], c1[m]) for m in range(nm)]
            return tuple(m2), tuple(c2)

        first = tuple(produce(0, 0, m) for m in range(nm))
        init_from_context()
        carry = (tuple(ms), first)
        carry = lax.fori_loop(0, nk // 2 - 1, functools.partial(pair_of_chunks, last=False), carry)
        pair_of_chunks(nk // 2 - 1, carry, True)

    outs = [acc_ref[m, 0:HEAD_V, :] / acc_ref[m, HEAD_V:HEAD_V + 1, :] for m in range(nm)]
    if kind == "diff":
        lam = (jnp.exp(jnp.sum(lq1_ref[...] * lk1_ref[...], axis=1, keepdims=True))
               - jnp.exp(jnp.sum(lq2_ref[...] * lk2_ref[...], axis=1, keepdims=True))
               + lam_init)
        heads = []
        for hd in range(2):
            dlt = outs[2 * hd] - lam * outs[2 * hd + 1]
            ms2 = jnp.mean(dlt * dlt, axis=0, keepdims=True)
            heads.append(dlt * lax.rsqrt(ms2 + EPS) * sub_ref[...] * (1.0 - lam_init))
        outs = heads
    o_ref[0] = jnp.concatenate(outs, axis=0).T.astype(o_ref.dtype)


def _flash(qt, kc, vct, k, vt, kind, extras=(), lam_init=0.0):
    b, nqc, wtot, cw = qt.shape
    qchunks = max(1, min(Q_TILE // cw, nqc))
    tq, nq = qchunks * cw, nqc // qchunks
    wq = 2 * LANES if kind == "mla" else LANES
    npair = wtot // wq
    nm = 4 if kind == "diff" else 2
    c = kc.shape[1]
    has_latent = k is not None
    ins = [qt, kc, vct]
    in_specs = [
        pl.BlockSpec((1, qchunks, wq, cw), lambda bi, p, i: (bi, i, p, 0)),
        pl.BlockSpec((1, c, wq), lambda bi, p, i: (bi, 0, p)),
        pl.BlockSpec((1, 1, LANES, c), lambda bi, p, i: (bi, 0, p, 0)),
    ]
    if has_latent:
        nk, tk = vt.shape[1], vt.shape[3]
        assert nk % 2 == 0, "key chunks are processed in pairs"
        ins += [k, vt]
        in_specs += [
            pl.BlockSpec((1, k.shape[1], wq), lambda bi, p, i: (bi, 0, p)),
            pl.BlockSpec((1, nk, LANES, tk), lambda bi, p, i: (bi, 0, p, 0)),
        ]
    for e in extras:
        ins.append(e)
        in_specs.append(pl.BlockSpec(e.shape, lambda bi, p, i: (0, 0)))
    return pl.pallas_call(
        functools.partial(_flash_kernel, kind=kind, has_latent=has_latent, lam_init=lam_init),
        grid=(b, npair, nq),
        in_specs=in_specs,
        out_specs=pl.BlockSpec((1, tq, LANES), lambda bi, p, i: (bi, i, p)),
        out_shape=jax.ShapeDtypeStruct((b, nq * tq, npair * LANES), BF16),
        scratch_shapes=[
            pltpu.VMEM((nm, LANES, tq), BF16),
            pltpu.VMEM((nm, LANES, tq), F32),
        ] + ([pltpu.VMEM((2, nm, tk, tq), F32)] if has_latent else []),
        compiler_params=_cparams(("parallel", "parallel", "parallel")),
        name="flash_" + kind + ("" if has_latent else "_ctx"),
    )(*ins)


NA_QCHUNKS = 2
NA_CHUNKS = NA_QCHUNKS + 2


def _na_kernel(qt_ref, k_ref, vt_ref, kc_ref, vct_ref, bias_ref, o_ref):
    nkc, tk = vt_ref.shape[1], vt_ref.shape[3]
    i = pl.program_id(2)
    c0 = jnp.clip(i * NA_QCHUNKS - 1, 0, nkc - NA_CHUNKS)
    qv = jnp.concatenate([qt_ref[0, c] for c in range(NA_QCHUNKS)], axis=1)
    rows = lax.broadcasted_iota(jnp.int32, qv.shape, 0)
    outs = []
    for m in range(2):
        own = lax.shift_right_logical(rows, int(math.log2(LANES // 2))) == m
        qm = jnp.where(own, qv, jnp.zeros_like(qv))
        v_rows = slice(m * HEAD_V, (m + 1) * HEAD_V)
        ss = []
        for j in range(NA_CHUNKS):
            kj = k_ref[0, pl.ds(pl.multiple_of((c0 + j) * tk, tk), tk), :]
            ss.append(jnp.dot(kj, qm, preferred_element_type=F32) + bias_ref[0, m, j])
        sc = jnp.dot(kc_ref[0], qm, preferred_element_type=F32)
        mx = jnp.max(sc, axis=0, keepdims=True)
        for st in ss:
            mx = jnp.maximum(mx, jnp.max(st, axis=0, keepdims=True))
        acc = jnp.dot(with_ones(vct_ref[0, 0, v_rows, :]), jnp.exp2(sc - mx).astype(BF16),
                      preferred_element_type=F32)
        for j, st in enumerate(ss):
            acc = acc + jnp.dot(with_ones(vt_ref[0, c0 + j, v_rows, :]),
                                jnp.exp2(st - mx).astype(BF16), preferred_element_type=F32)
        outs.append(acc[0:HEAD_V] / acc[HEAD_V:HEAD_V + 1])
    o_ref[0] = jnp.concatenate(outs, axis=0).T.astype(o_ref.dtype)


def _na_bias_tiles(rel_bias, rows, tk):
    qrows = NA_QCHUNKS * tk // GRID_W
    nblk = rows // qrows
    nk = rows * GRID_W // tk
    krows = NA_CHUNKS * tk // GRID_W
    kh = min(NA_WIN_ROWS, rows)
    kw = NA_WIN_COLS
    blocks = np.array([0, 1, nblk - 1])
    c0 = np.clip(blocks * NA_QCHUNKS - 1, 0, nk - NA_CHUNKS)
    r = blocks[:, None] * qrows + np.arange(qrows)[None, :]
    kr = c0[:, None] * (tk // GRID_W) + np.arange(krows)[None, :]
    r0 = np.clip(r - kh // 2, 0, rows - kh)
    dr = kr[:, None, :] - r[:, :, None]
    valid_r = (kr[:, None, :] >= r0[:, :, None]) & (kr[:, None, :] < r0[:, :, None] + kh)
    cols = np.arange(GRID_W)
    cstart = np.clip(cols - kw // 2, 0, GRID_W - kw)
    dc = cols[None, :] - cols[:, None]
    valid_c = (cols[None, :] >= cstart[:, None]) & (cols[None, :] < cstart[:, None] + kw)
    ndc = 2 * NA_WIN_COLS - 1
    onehot = ((dc[None] + NA_WIN_COLS - 1) == np.arange(ndc)[:, None, None]) & valid_c[None]
    dr_idx = np.clip(dr + NA_WIN_ROWS - 1, 0, 2 * NA_WIN_ROWS - 2)
    g1 = rel_bias[:, dr_idx, :]
    tiles = jnp.einsum("htabd,dcx->thbxac", g1, jnp.asarray(onehot, F32),
                       precision=lax.Precision.HIGHEST)
    valid = valid_r.transpose(0, 2, 1)[:, None, :, None, :, None] & valid_c.T[None, None, None, :, None, :]
    tiles = jnp.where(jnp.asarray(valid), tiles * LOG2E, NEG)
    return tiles.reshape(3, rel_bias.shape[0], NA_CHUNKS, tk, qrows * GRID_W)


def _na_attention(qt, kc, vct, k, vt, bias):
    b, nqc, wtot, cw = qt.shape
    nblk, tq = nqc // NA_QCHUNKS, NA_QCHUNKS * cw
    npair = wtot // LANES
    nkc, tk = vt.shape[1], vt.shape[3]
    c = kc.shape[1]
    n = k.shape[1]

    def bias_map(bi, p, i):
        return (jnp.where(i == 0, 0, jnp.where(i == nblk - 1, 2, 1)), p, 0, 0, 0)

    return pl.pallas_call(
        _na_kernel,
        grid=(b, npair, nblk),
        in_specs=[
            pl.BlockSpec((1, NA_QCHUNKS, LANES, cw), lambda bi, p, i: (bi, i, p, 0)),
            pl.BlockSpec((1, n, LANES), lambda bi, p, i: (bi, 0, p)),
            pl.BlockSpec((1, nkc, LANES, tk), lambda bi, p, i: (bi, 0, p, 0)),
            pl.BlockSpec((1, c, LANES), lambda bi, p, i: (bi, 0, p)),
            pl.BlockSpec((1, 1, LANES, c), lambda bi, p, i: (bi, 0, p, 0)),
            pl.BlockSpec((1, 2, NA_CHUNKS, tk, tq), bias_map),
        ],
        out_specs=pl.BlockSpec((1, tq, LANES), lambda bi, p, i: (bi, i, p)),
        out_shape=jax.ShapeDtypeStruct((b, nblk * tq, wtot), BF16),
        compiler_params=_cparams(("parallel", "parallel", "parallel")),
        name="na_attention",
    )(qt, k, vt, kc, vct, bias)


def _first_index_of_max(vals, row, big):
    mx = jnp.max(vals, axis=0, keepdims=True)
    idx = jnp.min(jnp.where(vals == mx, row, big), axis=0, keepdims=True)
    return mx, idx


def _router_gates(scores_t, sel_t):
    irow = lax.broadcasted_iota(jnp.int32, sel_t.shape, 0)
    row = irow.astype(F32)
    grp = lax.shift_right_logical(irow, int(math.log2(EXPERTS_PER_GROUP))).astype(F32)
    big = float(N_EXPERTS)
    best = None
    for g in range(MOE_GROUPS):
        vals = jnp.where(grp == float(g), sel_t, -jnp.inf)
        m1, i1 = _first_index_of_max(vals, row, big)
        m2 = jnp.max(jnp.where(row == i1, -jnp.inf, vals), axis=0, keepdims=True)
        gs = m1 + m2
        if best is None:
            best, bi = gs, jnp.zeros(gs.shape, F32)
        else:
            better = gs > best
            bi = jnp.where(better, float(g), bi)
            best = jnp.where(better, gs, best)
    msel = jnp.where(grp == bi, sel_t, -jnp.inf)
    _, i1 = _first_index_of_max(msel, row, big)
    msel2 = jnp.where(row == i1, -jnp.inf, msel)
    _, i2 = _first_index_of_max(msel2, row, big)
    w = jnp.where((row == i1) | (row == i2), scores_t, 0.0)
    return w / jnp.sum(w, axis=0, keepdims=True)


def _out_kernel(a_ref, b_ref, c_ref, d_ref, x_ref, g1_ref, sh_ref, sc_ref, gffn_ref,
                w_ref, rwh_ref, rwl_ref, rb_ref, xo_ref, h2_ref, gate_ref):
    y = None
    for gi, r in enumerate((a_ref, b_ref, c_ref, d_ref)):
        part = jnp.dot(r[0], w_ref[gi * SEG_W:(gi + 1) * SEG_W, :], preferred_element_type=F32)
        y = part if y is None else y + part
    xn = x_ref[0] + g1_ref[0] * y
    xo_ref[0] = xn
    ms = jnp.mean(xn * xn, axis=-1, keepdims=True)
    h2 = xn * lax.rsqrt(ms + EPS) * gffn_ref[...]
    h2 = h2 * (1.0 + sc_ref[0]) + sh_ref[0]
    hi = h2.astype(BF16)
    h2_ref[0] = hi
    lo = (h2 - hi.astype(F32)).astype(BF16)
    logits = (jnp.dot(hi, rwh_ref[...], preferred_element_type=F32)
              + jnp.dot(lo, rwh_ref[...], preferred_element_type=F32)
              + jnp.dot(hi, rwl_ref[...], preferred_element_type=F32))
    tm = logits.shape[0]
    scores_t = _sigmoid(logits).T[:N_EXPERTS]
    sel_t = scores_t + _tile_lanes(rb_ref[...], tm)
    gates_t = _router_gates(scores_t, sel_t)
    pad = jnp.zeros((LANES - N_EXPERTS, tm), F32)
    gate_ref[0] = jnp.concatenate([gates_t, pad], axis=0).T


def _out_router(parts, x, g1, sh2, sc2, row_of_batch, pw, tm):
    b, n, d = x.shape
    full = lambda a: pl.BlockSpec(a.shape, lambda bi, i: (0,) * a.ndim)
    tok = lambda w: pl.BlockSpec((1, tm, w), lambda bi, i: (bi, i, 0))
    mod_spec = pl.BlockSpec((1, 1, d), lambda bi, i: (row_of_batch(bi), 0, 0))
    consts = [pw["gffn"], pw["w_out"], pw["rw_hi"], pw["rw_lo"], pw["rb"]]
    return pl.pallas_call(
        _out_kernel,
        grid=(b, n // tm),
        in_specs=[tok(SEG_W)] * 4 + [tok(d), mod_spec, mod_spec, mod_spec] + [full(a) for a in consts],
        out_specs=[tok(d), tok(d), tok(LANES)],
        out_shape=[jax.ShapeDtypeStruct((b, n, d), F32), jax.ShapeDtypeStruct((b, n, d), BF16),
                   jax.ShapeDtypeStruct((b, n, LANES), F32)],
        compiler_params=_cparams(("parallel", "parallel")),
        name="out_router",
    )(*parts, x, g1, sh2, sc2, *consts)


MOE_EXPERTS_PER_STEP = 2


def _swiglu_act(gu, scale=None):
    de = gu.shape[1] // 2
    g, u = gu[:, :de], gu[:, de:]
    act = (g * _sigmoid(g)) * u
    return (act if scale is None else act * scale).astype(BF16)


def _moe_kernel(h_ref, gate_ref, x_ref, g2_ref, wgu_ref, wd_ref, sgu_ref, sd_ref, fin_ref,
                o_ref, acc_ref, *, final_norm):
    step = pl.program_id(1)
    k = wgu_ref.shape[0]
    hb = h_ref[...]
    gates = gate_ref[...]
    lane = _lane_iota(gates.shape)
    acts = []
    for i in range(k):
        gu = jnp.dot(hb, wgu_ref[i], preferred_element_type=F32)
        col = jnp.sum(jnp.where(lane == step * k + i, gates, 0.0), axis=1, keepdims=True)
        acts.append(_swiglu_act(gu, col))
    wd = wd_ref[...]
    contrib = jnp.dot(jnp.concatenate(acts, axis=1), wd.reshape(k * wd.shape[1], wd.shape[2]),
                      preferred_element_type=F32)

    @pl.when(step == 0)
    def _():
        shared = _swiglu_act(jnp.dot(hb, sgu_ref[...], preferred_element_type=F32))
        acc_ref[...] = contrib + jnp.dot(shared, sd_ref[...], preferred_element_type=F32)

    @pl.when(step > 0)
    def _():
        acc_ref[...] += contrib

    @pl.when(step == pl.num_programs(1) - 1)
    def _():
        y = x_ref[...] + g2_ref[0] * acc_ref[...]
        if final_norm:
            ms = jnp.mean(y * y, axis=-1, keepdims=True)
            y = y * lax.rsqrt(ms + EPS) * fin_ref[...]
        o_ref[...] = y


def _moe(h2, gates, x, g2, row_of_tile, pw, tm, final_norm):
    t, d = x.shape
    ne, _, de2 = pw["wgu"].shape
    k = MOE_EXPERTS_PER_STEP
    full = lambda a: pl.BlockSpec(a.shape, lambda i, e: (0,) * a.ndim)
    return pl.pallas_call(
        functools.partial(_moe_kernel, final_norm=final_norm),
        grid=(t // tm, ne // k),
        in_specs=[
            pl.BlockSpec((tm, d), lambda i, e: (i, 0)),
            pl.BlockSpec((tm, LANES), lambda i, e: (i, 0)),
            pl.BlockSpec((tm, d), lambda i, e: (i, 0)),
            pl.BlockSpec((1, 1, d), lambda i, e: (row_of_tile(i), 0, 0)),
            pl.BlockSpec((k, d, de2), lambda i, e: (e, 0, 0)),
            pl.BlockSpec((k, de2 // 2, d), lambda i, e: (e, 0, 0)),
            full(pw["sgu"]), full(pw["sd"]), full(pw["fin"]),
        ],
        out_specs=pl.BlockSpec((tm, d), lambda i, e: (i, 0)),
        out_shape=jax.ShapeDtypeStruct((t, d), F32),
        scratch_shapes=[pltpu.VMEM((tm, d), F32)],
        compiler_params=_cparams(("parallel", "arbitrary")),
        name="moe_final" if final_norm else "moe",
    )(h2, gates, x, g2, pw["wgu"], pw["wd"], pw["sgu"], pw["sd"], pw["fin"])


def _axial_angles(n_tok, rot_dim):
    half = rot_dim // 2
    freqs = ROPE_BASE ** (-jnp.arange(0, half, 2, dtype=F32) / half)
    t = jnp.arange(n_tok, dtype=jnp.int32)
    row = (t // GRID_W).astype(F32)
    col = (t % GRID_W).astype(F32)
    return row[:, None] * freqs, col[:, None] * freqs


def _rope_table(n_tok, rot_dim):
    ar, ac = _axial_angles(n_tok, rot_dim)
    cos = jnp.concatenate([jnp.cos(ar)] * 2 + [jnp.cos(ac)] * 2, axis=1)
    sin = jnp.concatenate([-jnp.sin(ar), jnp.sin(ar), -jnp.sin(ac), jnp.sin(ac)], axis=1)
    return cos, sin


def _rope_tables(n_tok):
    cg, sg = _rope_table(n_tok, GQA_DIM)
    cd, sd = _rope_table(n_tok, DIFF_QK_DIM)
    cm32, sm32 = _rope_table(n_tok, MLA_ROPE)
    ones = jnp.ones((n_tok, MLA_NOPE), F32)
    zeros = jnp.zeros((n_tok, MLA_NOPE), F32)
    pad = LANES - MLA_NOPE - MLA_ROPE
    cm = jnp.concatenate([ones, cm32, ones[:, :pad]], axis=1)
    sm = jnp.concatenate([zeros, sm32, zeros[:, :pad]], axis=1)
    return (jnp.tile(cg, (1, LANES // GQA_DIM)), jnp.tile(sg, (1, LANES // GQA_DIM)),
            jnp.tile(cd, (1, LANES // DIFF_QK_DIM)), jnp.tile(sd, (1, LANES // DIFF_QK_DIM)),
            cm, sm)


def _pack_layer(l, p):
    d = p["w_in"].shape[1]
    w = p["w_in"][l]
    o = np.cumsum([0, 256, 128, 128, 256, 256, 256, 256, 256, 256, MLA_Q_LORA, MLA_KV_LORA, MLA_ROPE])
    col = lambda i: w[:, int(o[i]):int(o[i + 1])]
    dup = lambda a: jnp.concatenate([a[:, :GQA_DIM], a[:, :GQA_DIM], a[:, GQA_DIM:], a[:, GQA_DIM:]], axis=1)
    z = lambda n: jnp.zeros((d, n), F32)
    w_in = jnp.concatenate(
        [col(0), dup(col(1)), dup(col(2)), col(3), col(4), col(5), col(6), col(7), col(8),
         col(9), z(SEG_W - MLA_Q_LORA), col(10),
         z(MLA_NOPE), col(11), z(LANES - MLA_NOPE - MLA_ROPE)], axis=1).astype(BF16)
    dq = MLA_NOPE + MLA_ROPE
    wqu = p["mla_w_q_up"][l].reshape(MLA_Q_LORA, MLA_HEADS, dq)
    wqu = jnp.pad(wqu, ((0, SEG_W - MLA_Q_LORA), (0, 0), (0, LANES - dq)))
    wkv = p["mla_w_kv_up"][l].reshape(MLA_KV_LORA, MLA_HEADS, MLA_NOPE + MLA_V)
    wkk = jnp.pad(wkv[:, :, :MLA_NOPE], ((0, 0), (0, 0), (0, LANES - MLA_NOPE)))
    e = p["moe_w_gate"].shape[1]
    rw = jnp.pad(p["router_w"], ((0, 0), (0, LANES - e)))
    rw_hi = rw.astype(BF16)
    rw_lo = (rw - rw_hi.astype(F32)).astype(BF16)
    return {
        "gmix": p["norm_mix"][l][None], "gffn": p["norm_ffn"][l][None],
        "w_in": w_in,
        "wq_up": wqu.reshape(SEG_W, MLA_HEADS * LANES).astype(BF16),
        "wkv_k": wkk.reshape(MLA_KV_LORA, MLA_HEADS * LANES).astype(BF16),
        "wkv_v": wkv[:, :, MLA_NOPE:].reshape(MLA_KV_LORA, MLA_HEADS * MLA_V).astype(BF16),
        "gq": (jnp.tile(p["gqa_q_norm"][l], GQA_HEADS) * (GQA_DIM ** -0.5 * LOG2E))[None],
        "gk": jnp.tile(p["gqa_k_norm"][l], GQA_HEADS)[None],
        "gcq": jnp.pad(p["mla_q_norm"][l], (0, SEG_W - MLA_Q_LORA))[None],
        "gckv": p["mla_kv_norm"][l][None],
        "w_out": p["w_out"][l].astype(BF16),
        "rw_hi": rw_hi, "rw_lo": rw_lo,
        "rb": jnp.broadcast_to(p["router_b"][:, None], (e, LANES)),
        "wgu": jnp.concatenate([p["moe_w_gate"][l], p["moe_w_up"][l]], axis=2).astype(BF16),
        "wd": p["moe_w_down"][l].astype(BF16),
        "sgu": jnp.concatenate([p["shared_w_gate"][l], p["shared_w_up"][l]], axis=1).astype(BF16),
        "sd": p["shared_w_down"][l].astype(BF16),
        "fin": p["final_norm"][None],
        "diff": (p["diff_lq1"][l][None], p["diff_lk1"][l][None], p["diff_lq2"][l][None],
                 p["diff_lk2"][l][None], p["diff_subln"][l][:, None]),
    }


def kernel(x, c, ctx, c_ctx, w_mod, b_mod, norm_mix, norm_ffn, w_in, w_out, gqa_q_norm, gqa_k_norm,
           na_rel_bias, diff_lq1, diff_lk1, diff_lq2, diff_lk2, diff_subln, mla_q_norm, mla_w_q_up,
           mla_kv_norm, mla_w_kv_up, router_w, router_b, moe_w_gate, moe_w_up, moe_w_down,
           shared_w_gate, shared_w_up, shared_w_down, final_norm):
    p = dict(w_in=w_in, norm_mix=norm_mix, norm_ffn=norm_ffn, w_out=w_out, gqa_q_norm=gqa_q_norm,
             gqa_k_norm=gqa_k_norm, diff_lq1=diff_lq1, diff_lk1=diff_lk1, diff_lq2=diff_lq2,
             diff_lk2=diff_lk2, diff_subln=diff_subln, mla_q_norm=mla_q_norm, mla_w_q_up=mla_w_q_up,
             mla_kv_norm=mla_kv_norm, mla_w_kv_up=mla_w_kv_up, router_w=router_w, router_b=router_b,
             moe_w_gate=moe_w_gate, moe_w_up=moe_w_up, moe_w_down=moe_w_down,
             shared_w_gate=shared_w_gate, shared_w_up=shared_w_up, shared_w_down=shared_w_down,
             final_norm=final_norm)
    b, s, d = x.shape
    nctx = ctx.shape[1]
    depth = w_mod.shape[0]
    rows = s // GRID_W
    assert s % Q_TILE == 0 and s % TOK_TILE == 0 and s // NA_TILE >= NA_CHUNKS + NA_QCHUNKS
    assert nctx % LANES == 0 and (b * s) % MOE_TILE == 0 and s % MOE_TILE == 0

    mrows = -(-(b + 1) // SUBLANES) * SUBLANES
    cc = jnp.zeros((mrows, d), F32).at[:b].set(c).at[b].set(c_ctx)
    mod = _modulation(cc, w_mod, b_mod)
    tables = _rope_tables(s)
    lat_row = lambda bi: bi
    ctx_row = lambda bi: b
    tm_ctx = b * nctx if b * nctx <= MOE_TILE else nctx

    xc = ctx
    for l in range(depth):
        want_ctx = l < depth - 1
        lam_init = 0.8 - 0.6 * math.exp(-0.3 * l)
        pw = _pack_layer(l, p)
        m6 = mod[l].reshape(mrows, 6, 1, d)
        sh1, sc1, g1, sh2, sc2, g2 = (m6[:, k] for k in range(6))

        lat = _project(x, sh1, sc1, lat_row, pw, tables, TOK_TILE, LAT_OUTS)
        cx = _project(xc, sh1, sc1, ctx_row, pw, None, nctx, CTX_OUTS if want_ctx else CTX_KV_OUTS)

        def dense(g, kind, extras=()):
            return _flash(lat[g + ".qT"], cx[g + ".k"], cx[g + ".vT"], lat[g + ".k"], lat[g + ".vT"],
                          kind, extras, lam_init)

        def ctx_only(g, kind, extras=()):
            return _flash(cx[g + ".qT"], cx[g + ".k"], cx[g + ".vT"], None, None, kind, extras, lam_init)

        a_lat = dense("A", "pair")
        bias = _na_bias_tiles(na_rel_bias[l], rows, NA_TILE)
        b_lat = _na_attention(lat["B.qT"], cx["B.k"], cx["B.vT"], lat["B.k"], lat["B.vT"], bias)
        c_lat = dense("C", "diff", pw["diff"])
        d_lat = dense("D", "mla")
        x1, h2, gates = _out_router((a_lat, b_lat, c_lat, d_lat), x, g1, sh2, sc2, lat_row, pw, TOK_TILE)
        tiles_per_batch = s // MOE_TILE
        x = _moe(h2.reshape(b * s, d), gates.reshape(b * s, LANES), x1.reshape(b * s, d), g2,
                 lambda i: i // tiles_per_batch, pw, MOE_TILE, not want_ctx).reshape(b, s, d)

        if want_ctx:
            a_c = ctx_only("A", "pair")
            b_c = ctx_only("B", "pair")
            c_c = ctx_only("C", "diff", pw["diff"])
            d_c = ctx_only("D", "mla")
            xc1, h2c, gates_c = _out_router((a_c, b_c, c_c, d_c), xc, g1, sh2, sc2, ctx_row, pw, nctx)
            xc = _moe(h2c.reshape(b * nctx, d), gates_c.reshape(b * nctx, LANES),
                      xc1.reshape(b * nctx, d), g2, lambda i: b, pw, tm_ctx, False).reshape(b, nctx, d)
    return x
```

```python
import functools
import math

import numpy as np
import jax
import jax.numpy as jnp
from jax import lax
from jax.experimental import pallas as pl
from jax.experimental.pallas import tpu as pltpu

F32 = jnp.float32
BF16 = jnp.bfloat16

GRID_W = 64
EPS = 1e-6
ROPE_BASE = 10000.0
GQA_HEADS, GQA_KV_HEADS, GQA_DIM = 4, 2, 64
NA_HEADS, NA_DIM, NA_WIN_ROWS, NA_WIN_COLS = 4, 64, 8, 16
DIFF_HEADS, DIFF_QK_DIM, DIFF_V_DIM = 4, 32, 64
MLA_HEADS, MLA_NOPE, MLA_ROPE, MLA_V, MLA_Q_LORA, MLA_KV_LORA = 4, 64, 32, 64, 192, 128
N_EXPERTS, MOE_GROUPS, D_EXPERT = 16, 4, 256
EXPERTS_PER_GROUP = N_EXPERTS // MOE_GROUPS

LANES = 128
SUBLANES = 8
VMEM_LIMIT = 56 * 1024 * 1024

TOK_TILE = 512
NA_TILE = 256
MOE_TILE = 1024
MOD_TILE = 1024
NEG = -1e30
LOG2E = math.log2(math.e)

SEG_W = 256
OFF_AQ, OFF_AK, OFF_AV = 0, 256, 512
OFF_BQ, OFF_BK, OFF_BV = 768, 1024, 1280
OFF_CQ, OFF_CK, OFF_CV = 1536, 1792, 2048
OFF_DQ, OFF_DKV, OFF_DPE = 2304, 2560, 2688
IN_PACKED = 2816


def _cparams(sem):
    return pltpu.CompilerParams(dimension_semantics=sem, vmem_limit_bytes=VMEM_LIMIT)


def _lane_iota(shape):
    return lax.broadcasted_iota(jnp.int32, shape, 1)


def _lane_group(shape, width):
    return lax.shift_right_logical(_lane_iota(shape), int(math.log2(width)))


def _sigmoid(x):
    return 1.0 / (1.0 + jnp.exp(-x))


def _mod_kernel(c_ref, w_ref, b_ref, o_ref):
    c = c_ref[...]
    s = c * _sigmoid(c)
    o_ref[0] = jnp.dot(s, w_ref[0], preferred_element_type=F32,
                       precision=lax.Precision.HIGHEST) + b_ref[0]


def _modulation(cc, w_mod, b_mod):
    depth, d, n = w_mod.shape
    rows = cc.shape[0]
    return pl.pallas_call(
        _mod_kernel,
        grid=(depth, n // MOD_TILE),
        in_specs=[
            pl.BlockSpec((rows, d), lambda l, j: (0, 0)),
            pl.BlockSpec((1, d, MOD_TILE), lambda l, j: (l, 0, j)),
            pl.BlockSpec((1, 1, MOD_TILE), lambda l, j: (l, 0, j)),
        ],
        out_specs=pl.BlockSpec((1, rows, MOD_TILE), lambda l, j: (l, 0, j)),
        out_shape=jax.ShapeDtypeStruct((depth, rows, n), F32),
        compiler_params=_cparams(("parallel", "parallel")),
        name="modulation",
    )(cc, w_mod, b_mod.reshape(depth, 1, n))


def _swap_halves(x, half):
    w = x.shape[1]
    lane = _lane_iota(x.shape)
    fwd = pltpu.roll(x, half, 1)
    bwd = pltpu.roll(x, w - half, 1)
    return jnp.where((lane & (2 * half - 1)) < half, bwd, fwd)


def _rope(x, cos, sin, half):
    reps = x.shape[1] // cos.shape[1]
    c = jnp.concatenate([cos] * reps, axis=1) if reps > 1 else cos
    s = jnp.concatenate([sin] * reps, axis=1) if reps > 1 else sin
    return x * c + _swap_halves(x, half) * s


def _group_rms(x, group, count):
    gid = _lane_group(x.shape, group)
    x2 = x * x
    inv = jnp.zeros_like(x)
    for g in range(x.shape[1] // group):
        msk = gid == g
        ms = jnp.sum(jnp.where(msk, x2, 0.0), axis=1, keepdims=True) * (1.0 / count)
        inv = jnp.where(msk, lax.rsqrt(ms + EPS), inv)
    return x * inv


LAT_OUTS = ("A.qT", "A.k", "A.vT", "B.qT", "B.k", "B.vT", "C.qT", "C.k", "C.vT", "D.qT", "D.k", "D.vT")
CTX_KV_OUTS = ("A.k", "A.vT", "B.k", "B.vT", "C.k", "C.vT", "D.k", "D.vT")
CTX_OUTS = CTX_KV_OUTS + ("A.qT", "B.qT", "C.qT", "D.qT")
GROUP_W = {"A": SEG_W, "B": SEG_W, "C": SEG_W, "D": 2 * SEG_W}


def _proj_kernel(*refs, rope, outs):
    (x_ref, sh_ref, sc_ref, gmix_ref, w_ref, wqu_ref, wkk_ref, wkv_ref,
     gq_ref, gk_ref, gcq_ref, gckv_ref) = refs[:12]
    pos = 12
    if rope:
        cg_ref, sg_ref, cd_ref, sd_ref, cm_ref, sm_ref = refs[pos:pos + 6]
        pos += 6
    out = dict(zip(outs, refs[pos:pos + len(outs)]))

    def want(g, t):
        return (g + "." + t) in out or (g + "." + t + "T") in out

    def put(g, t, val):
        if g + "." + t in out:
            out[g + "." + t][0] = val.astype(BF16)
        if g + "." + t + "T" in out:
            ref = out[g + "." + t + "T"]
            vt = val.T.astype(BF16)
            cw = ref.shape[3]
            for c in range(ref.shape[1]):
                ref[0, c] = vt[:, c * cw:(c + 1) * cw]

    x = x_ref[0]
    ms = jnp.mean(x * x, axis=-1, keepdims=True)
    h = x * lax.rsqrt(ms + EPS) * gmix_ref[...]
    h = h * (1.0 + sc_ref[0]) + sh_ref[0]
    hb = h.astype(BF16)

    def seg(off, width=SEG_W):
        return jnp.dot(hb, w_ref[:, off:off + width], preferred_element_type=F32)

    if want("A", "q"):
        q = _group_rms(seg(OFF_AQ), GQA_DIM, GQA_DIM) * gq_ref[...]
        if rope:
            q = _rope(q, cg_ref[...], sg_ref[...], GQA_DIM // 4)
        put("A", "q", q)
    k = _group_rms(seg(OFF_AK), GQA_DIM, GQA_DIM) * gk_ref[...]
    if rope:
        k = _rope(k, cg_ref[...], sg_ref[...], GQA_DIM // 4)
    put("A", "k", k)
    put("A", "v", seg(OFF_AV))

    if want("B", "q"):
        put("B", "q", seg(OFF_BQ) * (NA_DIM ** -0.5 * LOG2E))
    put("B", "k", seg(OFF_BK))
    put("B", "v", seg(OFF_BV))

    if want("C", "q"):
        q = seg(OFF_CQ) * (DIFF_QK_DIM ** -0.5 * LOG2E)
        if rope:
            q = _rope(q, cd_ref[...], sd_ref[...], DIFF_QK_DIM // 4)
        put("C", "q", q)
    k = seg(OFF_CK)
    if rope:
        k = _rope(k, cd_ref[...], sd_ref[...], DIFF_QK_DIM // 4)
    put("C", "k", k)
    put("C", "v", seg(OFF_CV))

    if want("D", "q"):
        cq = _group_rms(seg(OFF_DQ), SEG_W, MLA_Q_LORA) * gcq_ref[...]
        q = jnp.dot(cq.astype(BF16), wqu_ref[...], preferred_element_type=F32)
        q = q * ((MLA_NOPE + MLA_ROPE) ** -0.5 * LOG2E)
        if rope:
            q = _rope(q, cm_ref[...], sm_ref[...], MLA_ROPE // 4)
        put("D", "q", q)
    ckv = _group_rms(seg(OFF_DKV, LANES), LANES, MLA_KV_LORA) * gckv_ref[...]
    ckvb = ckv.astype(BF16)
    k = jnp.dot(ckvb, wkk_ref[...], preferred_element_type=F32)
    kpe = seg(OFF_DPE, LANES)
    if rope:
        kpe = _rope(kpe, cm_ref[...], sm_ref[...], MLA_ROPE // 4)
    put("D", "k", k + jnp.concatenate([kpe] * MLA_HEADS, axis=1))
    put("D", "v", jnp.dot(ckvb, wkv_ref[...], preferred_element_type=F32))


def _project(x, sh, sc, row_of_batch, pw, tables, tm, outs):
    b, n, d = x.shape
    nt = n // tm
    rope = tables is not None
    full = lambda a: pl.BlockSpec(a.shape, lambda bi, i: (0,) * a.ndim)
    mod_spec = pl.BlockSpec((1, 1, d), lambda bi, i: (row_of_batch(bi), 0, 0))
    ins = [x, sh, sc, pw["gmix"], pw["w_in"], pw["wq_up"], pw["wkv_k"], pw["wkv_v"],
           pw["gq"], pw["gk"], pw["gcq"], pw["gckv"]]
    in_specs = [pl.BlockSpec((1, tm, d), lambda bi, i: (bi, i, 0)), mod_spec, mod_spec]
    in_specs += [full(a) for a in ins[3:]]
    if rope:
        ins += list(tables)
        in_specs += [pl.BlockSpec((tm, LANES), lambda bi, i: (i, 0)) for _ in tables]
    shapes, specs = [], []
    for name in outs:
        g, t = name.split(".")
        w = SEG_W if t[0] == "v" else GROUP_W[g]
        if t.endswith("T"):
            cw = min(tm, NA_TILE) if g == "B" else tm
            shapes.append(jax.ShapeDtypeStruct((b, n // cw, w, cw), BF16))
            specs.append(pl.BlockSpec((1, tm // cw, w, cw), lambda bi, i: (bi, i, 0, 0)))
        else:
            shapes.append(jax.ShapeDtypeStruct((b, n, w), BF16))
            specs.append(pl.BlockSpec((1, tm, w), lambda bi, i: (bi, i, 0)))
    res = pl.pallas_call(
        functools.partial(_proj_kernel, rope=rope, outs=tuple(outs)),
        grid=(b, nt),
        in_specs=in_specs,
        out_specs=specs,
        out_shape=shapes,
        compiler_params=_cparams(("parallel", "parallel")),
        name="proj_rope" if rope else "proj_ctx",
    )(*ins)
    return dict(zip(outs, res))


HEAD_V = 64


def _tile_lanes(x, width):
    reps = width // x.shape[1]
    return jnp.tile(x, (1, reps)) if reps > 1 else x


def with_ones(vt):
    rows = lax.broadcasted_iota(jnp.int32, (LANES - HEAD_V, vt.shape[1]), 0)
    ones = jnp.where(rows == 0, 1.0, 0.0).astype(BF16)
    return jnp.concatenate([vt, ones], axis=0)


def _map_slices(kind):
    def k_lanes(m):
        return slice(m * LANES, (m + 1) * LANES) if kind == "mla" else slice(None)

    def v_rows(m):
        hv = m // 2 if kind == "diff" else m
        return slice(hv * HEAD_V, (hv + 1) * HEAD_V)

    return k_lanes, v_rows


def _query_slabs(qt_ref, kind, nm):
    qv = jnp.concatenate([qt_ref[0, c] for c in range(qt_ref.shape[1])], axis=1)
    if kind == "mla":
        return [qv[m * LANES:(m + 1) * LANES, :] for m in range(nm)]
    rows = lax.broadcasted_iota(jnp.int32, qv.shape, 0)
    grp = lax.shift_right_logical(rows, int(math.log2(LANES // nm)))
    return [jnp.where(grp == m, qv, jnp.zeros_like(qv)) for m in range(nm)]


def _finish_heads(accs, kind, lam_init, diff_refs):
    outs = [a[0:HEAD_V] / a[HEAD_V:HEAD_V + 1] for a in accs]
    if kind == "diff":
        lq1_ref, lk1_ref, lq2_ref, lk2_ref, sub_ref = diff_refs
        lam = (jnp.exp(jnp.sum(lq1_ref[...] * lk1_ref[...], axis=1, keepdims=True))
               - jnp.exp(jnp.sum(lq2_ref[...] * lk2_ref[...], axis=1, keepdims=True))
               + lam_init)
        heads = []
        for hd in range(2):
            dlt = outs[2 * hd] - lam * outs[2 * hd + 1]
            ms2 = jnp.mean(dlt * dlt, axis=0, keepdims=True)
            heads.append(dlt * lax.rsqrt(ms2 + EPS) * sub_ref[...] * (1.0 - lam_init))
        outs = heads
    return jnp.concatenate(outs, axis=0).T


def _flash_ctx_kernel(*refs, kind, lam_init):
    qt_ref, kc_ref, vct_ref = refs[:3]
    nd = 5 if kind == "diff" else 0
    diff_refs, o_ref = refs[3:3 + nd], refs[3 + nd]
    nm = 4 if kind == "diff" else 2
    k_lanes, v_rows = _map_slices(kind)
    accs = []
    for m, qm in enumerate(_query_slabs(qt_ref, kind, nm)):
        st = jnp.dot(kc_ref[0, :, k_lanes(m)], qm, preferred_element_type=F32)
        p = jnp.exp2(st - jnp.max(st, axis=0, keepdims=True)).astype(BF16)
        accs.append(jnp.dot(with_ones(vct_ref[0, 0, v_rows(m), :]), p, preferred_element_type=F32))
    o_ref[0] = _finish_heads(accs, kind, lam_init, diff_refs).astype(o_ref.dtype)


def _flash_lat_kernel(*refs, kind, lam_init):
    qt_ref, kc_ref, vct_ref, k_ref, vt_ref = refs[:5]
    nd = 5 if kind == "diff" else 0
    diff_refs, o_ref = refs[5:5 + nd], refs[5 + nd]
    qs_e, acc_e, s_e, qs_l, acc_l, s_l, m_ref, mc_ref = refs[6 + nd:14 + nd]
    early = (qs_e, acc_e, s_e, 0, False)
    late = (qs_l, acc_l, s_l, vt_ref.shape[1] // 2, True)

    nm = qs_e.shape[0]
    nk, tk = vt_ref.shape[1], vt_ref.shape[3]
    half = nk // 2
    iters = half // 2
    g = pl.program_id(2)
    nq = pl.num_programs(2) - 1
    k_lanes, v_rows = _map_slices(kind)

    def produce(tile, slot, j, m):
        qs_ref, _, s_ref = tile[:3]
        rows = pl.ds(pl.multiple_of(j * tk, tk), tk)
        st = jnp.dot(k_ref[0, rows, k_lanes(m)], qs_ref[m], preferred_element_type=F32)
        s_ref[slot, m] = st
        return jnp.max(st, axis=0, keepdims=True)

    def consume(tile, slot, j, m, m_prev, m_chunk):
        _, acc_ref, s_ref = tile[:3]
        m_new = jnp.maximum(m_prev, m_chunk)
        alpha = jnp.exp2(m_prev - m_new)
        p = jnp.exp2(s_ref[slot, m] - m_new).astype(BF16)
        acc_ref[m] = alpha * acc_ref[m] + jnp.dot(
            with_ones(vt_ref[0, j, v_rows(m), :]), p, preferred_element_type=F32)
        return m_new

    def iteration(i, carries, tiles, last):
        maps = range(nm)
        j0s = [tile[3] + 2 * i for tile in tiles]
        c1 = [[produce(tile, 1, j0 + 1, m) for m in maps] for tile, j0 in zip(tiles, j0s)]
        m1 = [[consume(tile, 0, j0, m, c[0][m], c[1][m]) for m in maps]
              for tile, j0, c in zip(tiles, j0s, carries)]
        c2 = [c1[t] if (last and tile[4]) else [produce(tile, 0, j0 + 2, m) for m in maps]
              for t, (tile, j0) in enumerate(zip(tiles, j0s))]
        m2 = [[consume(tile, 1, j0 + 1, m, m1[t][m], c1[t][m]) for m in maps]
              for t, (tile, j0) in enumerate(zip(tiles, j0s))]
        return tuple((tuple(a), tuple(b)) for a, b in zip(m2, c2))

    def run(carries, tiles):
        carries = iteration(0, carries, tiles, iters == 1)
        if iters > 2:
            carries = lax.fori_loop(1, iters - 1, lambda i, c: iteration(i, c, tiles, False), carries)
        if iters > 1:
            carries = iteration(iters - 1, carries, tiles, True)
        return carries

    def start_early():
        for m, qm in enumerate(_query_slabs(qt_ref, kind, nm)):
            qs_e[m] = qm
        sts = [jnp.dot(kc_ref[0, :, k_lanes(m)], qs_e[m], preferred_element_type=F32)
               for m in range(nm)]
        ms = tuple(jnp.max(st, axis=0, keepdims=True) for st in sts)
        first = tuple(produce(early, 0, 0, m) for m in range(nm))

        def init_from_context():
            for m in range(nm):
                p = jnp.exp2(sts[m] - ms[m]).astype(BF16)
                acc_e[m] = jnp.dot(with_ones(vct_ref[0, 0, v_rows(m), :]), p,
                                   preferred_element_type=F32)

        return (ms, first), init_from_context

    def load_late():
        return (tuple(m_ref[m] for m in range(nm)), tuple(mc_ref[m] for m in range(nm)))

    def hand_over(carry):
        qs_l[...] = qs_e[...]
        acc_l[...] = acc_e[...]
        s_l[0] = s_e[0]
        for m in range(nm):
            m_ref[m] = carry[0][m]
            mc_ref[m] = carry[1][m]

    def finish_late():
        accs = [acc_l[m] for m in range(nm)]
        o_ref[0] = _finish_heads(accs, kind, lam_init, diff_refs).astype(o_ref.dtype)

    @pl.when(g == 0)
    def _():
        carry, init_from_context = start_early()
        init_from_context()
        hand_over(run((carry,), (early,))[0])

    @pl.when((g > 0) & (g < nq))
    def _():
        carry_l = load_late()
        carry_e, init_from_context = start_early()
        init_from_context()
        carry_l, carry_e = run((carry_l, carry_e), (late, early))
        finish_late()
        hand_over(carry_e)

    @pl.when(g == nq)
    def _():
        run((load_late(),), (late,))
        finish_late()


def _flash_specs(qt, kc, vct, kind, extras):
    b, nqc, wtot, cw = qt.shape
    wq = 2 * LANES if kind == "mla" else LANES
    c = kc.shape[1]
    extra_specs = [pl.BlockSpec(e.shape, lambda bi, p, i: (0, 0)) for e in extras]
    kv_specs = [pl.BlockSpec((1, c, wq), lambda bi, p, i: (bi, 0, p)),
                pl.BlockSpec((1, 1, LANES, c), lambda bi, p, i: (bi, 0, p, 0))]
    return b, nqc, wtot // wq, wq, cw, kv_specs, extra_specs


def _flash_ctx(qt, kc, vct, kind, extras=(), lam_init=0.0):
    b, nqc, npair, wq, cw, kv_specs, extra_specs = _flash_specs(qt, kc, vct, kind, extras)
    return pl.pallas_call(
        functools.partial(_flash_ctx_kernel, kind=kind, lam_init=lam_init),
        grid=(b, npair, nqc),
        in_specs=[pl.BlockSpec((1, 1, wq, cw), lambda bi, p, i: (bi, i, p, 0))] + kv_specs + extra_specs,
        out_specs=pl.BlockSpec((1, cw, LANES), lambda bi, p, i: (bi, i, p)),
        out_shape=jax.ShapeDtypeStruct((b, nqc * cw, npair * LANES), BF16),
        compiler_params=_cparams(("parallel", "parallel", "parallel")),
        name="flash_" + kind + "_ctx",
    )(qt, kc, vct, *extras)


def _flash_lat(qt, kc, vct, k, vt, kind, extras=(), lam_init=0.0):
    b, nq, npair, wq, tq, kv_specs, extra_specs = _flash_specs(qt, kc, vct, kind, extras)
    nm = 4 if kind == "diff" else 2
    nk, tk = vt.shape[1], vt.shape[3]
    assert nk % 4 == 0, "two phases of key chunks, processed in pairs"
    return pl.pallas_call(
        functools.partial(_flash_lat_kernel, kind=kind, lam_init=lam_init),
        grid=(b, npair, nq + 1),
        in_specs=[pl.BlockSpec((1, 1, wq, tq), lambda bi, p, g: (bi, jnp.minimum(g, nq - 1), p, 0))]
        + kv_specs
        + [pl.BlockSpec((1, k.shape[1], wq), lambda bi, p, g: (bi, 0, p)),
           pl.BlockSpec((1, nk, LANES, tk), lambda bi, p, g: (bi, 0, p, 0))]
        + extra_specs,
        out_specs=pl.BlockSpec((1, tq, LANES), lambda bi, p, g: (bi, jnp.maximum(g - 1, 0), p)),
        out_shape=jax.ShapeDtypeStruct((b, nq * tq, npair * LANES), BF16),
        scratch_shapes=2 * [
            pltpu.VMEM((nm, LANES, tq), BF16),
            pltpu.VMEM((nm, LANES, tq), F32),
            pltpu.VMEM((2, nm, tk, tq), F32),
        ] + 2 * [pltpu.VMEM((nm, 1, tq), F32)],
        compiler_params=_cparams(("parallel", "parallel", "arbitrary")),
        name="flash_" + kind,
    )(qt, kc, vct, k, vt, *extras)


NA_QCHUNKS = 2
NA_CHUNKS = NA_QCHUNKS + 2


def _na_kernel(qt_ref, k_ref, vt_ref, kc_ref, vct_ref, bias_ref, o_ref):
    nkc, tk = vt_ref.shape[1], vt_ref.shape[3]
    i = pl.program_id(2)
    c0 = jnp.clip(i * NA_QCHUNKS - 1, 0, nkc - NA_CHUNKS)
    qv = jnp.concatenate([qt_ref[0, c] for c in range(NA_QCHUNKS)], axis=1)
    rows = lax.broadcasted_iota(jnp.int32, qv.shape, 0)
    outs = []
    for m in range(2):
        own = lax.shift_right_logical(rows, int(math.log2(LANES // 2))) == m
        qm = jnp.where(own, qv, jnp.zeros_like(qv))
        v_rows = slice(m * HEAD_V, (m + 1) * HEAD_V)
        ss = []
        for j in range(NA_CHUNKS):
            kj = k_ref[0, pl.ds(pl.multiple_of((c0 + j) * tk, tk), tk), :]
            ss.append(jnp.dot(kj, qm, preferred_element_type=F32) + bias_ref[0, m, j])
        sc = jnp.dot(kc_ref[0], qm, preferred_element_type=F32)
        mx = jnp.max(sc, axis=0, keepdims=True)
        for st in ss:
            mx = jnp.maximum(mx, jnp.max(st, axis=0, keepdims=True))
        acc = jnp.dot(with_ones(vct_ref[0, 0, v_rows, :]), jnp.exp2(sc - mx).astype(BF16),
                      preferred_element_type=F32)
        for j, st in enumerate(ss):
            acc = acc + jnp.dot(with_ones(vt_ref[0, c0 + j, v_rows, :]),
                                jnp.exp2(st - mx).astype(BF16), preferred_element_type=F32)
        outs.append(acc[0:HEAD_V] / acc[HEAD_V:HEAD_V + 1])
    o_ref[0] = jnp.concatenate(outs, axis=0).T.astype(o_ref.dtype)


def _na_bias_tiles(rel_bias, rows, tk):
    qrows = NA_QCHUNKS * tk // GRID_W
    nblk = rows // qrows
    nk = rows * GRID_W // tk
    krows = NA_CHUNKS * tk // GRID_W
    kh = min(NA_WIN_ROWS, rows)
    kw = NA_WIN_COLS
    blocks = np.array([0, 1, nblk - 1])
    c0 = np.clip(blocks * NA_QCHUNKS - 1, 0, nk - NA_CHUNKS)
    r = blocks[:, None] * qrows + np.arange(qrows)[None, :]
    kr = c0[:, None] * (tk // GRID_W) + np.arange(krows)[None, :]
    r0 = np.clip(r - kh // 2, 0, rows - kh)
    dr = kr[:, None, :] - r[:, :, None]
    valid_r = (kr[:, None, :] >= r0[:, :, None]) & (kr[:, None, :] < r0[:, :, None] + kh)
    cols = np.arange(GRID_W)
    cstart = np.clip(cols - kw // 2, 0, GRID_W - kw)
    dc = cols[None, :] - cols[:, None]
    valid_c = (cols[None, :] >= cstart[:, None]) & (cols[None, :] < cstart[:, None] + kw)
    ndc = 2 * NA_WIN_COLS - 1
    onehot = ((dc[None] + NA_WIN_COLS - 1) == np.arange(ndc)[:, None, None]) & valid_c[None]
    dr_idx = np.clip(dr + NA_WIN_ROWS - 1, 0, 2 * NA_WIN_ROWS - 2)
    g1 = rel_bias[:, dr_idx, :]
    tiles = jnp.einsum("htabd,dcx->thbxac", g1, jnp.asarray(onehot, F32),
                       precision=lax.Precision.HIGHEST)
    valid = valid_r.transpose(0, 2, 1)[:, None, :, None, :, None] & valid_c.T[None, None, None, :, None, :]
    tiles = jnp.where(jnp.asarray(valid), tiles * LOG2E, NEG)
    return tiles.reshape(3, rel_bias.shape[0], NA_CHUNKS, tk, qrows * GRID_W)


def _na_attention(qt, kc, vct, k, vt, bias):
    b, nqc, wtot, cw = qt.shape
    nblk, tq = nqc // NA_QCHUNKS, NA_QCHUNKS * cw
    npair = wtot // LANES
    nkc, tk = vt.shape[1], vt.shape[3]
    c = kc.shape[1]
    n = k.shape[1]

    def bias_map(bi, p, i):
        return (jnp.where(i == 0, 0, jnp.where(i == nblk - 1, 2, 1)), p, 0, 0, 0)

    return pl.pallas_call(
        _na_kernel,
        grid=(b, npair, nblk),
        in_specs=[
            pl.BlockSpec((1, NA_QCHUNKS, LANES, cw), lambda bi, p, i: (bi, i, p, 0)),
            pl.BlockSpec((1, n, LANES), lambda bi, p, i: (bi, 0, p)),
            pl.BlockSpec((1, nkc, LANES, tk), lambda bi, p, i: (bi, 0, p, 0)),
            pl.BlockSpec((1, c, LANES), lambda bi, p, i: (bi, 0, p)),
            pl.BlockSpec((1, 1, LANES, c), lambda bi, p, i: (bi, 0, p, 0)),
            pl.BlockSpec((1, 2, NA_CHUNKS, tk, tq), bias_map),
        ],
        out_specs=pl.BlockSpec((1, tq, LANES), lambda bi, p, i: (bi, i, p)),
        out_shape=jax.ShapeDtypeStruct((b, nblk * tq, wtot), BF16),
        compiler_params=_cparams(("parallel", "parallel", "parallel")),
        name="na_attention",
    )(qt, k, vt, kc, vct, bias)


def _first_index_of_max(vals, row, big):
    mx = jnp.max(vals, axis=0, keepdims=True)
    idx = jnp.min(jnp.where(vals == mx, row, big), axis=0, keepdims=True)
    return mx, idx


def _router_gates(scores_t, sel_t):
    irow = lax.broadcasted_iota(jnp.int32, sel_t.shape, 0)
    row = irow.astype(F32)
    grp = lax.shift_right_logical(irow, int(math.log2(EXPERTS_PER_GROUP))).astype(F32)
    big = float(N_EXPERTS)
    best = None
    for g in range(MOE_GROUPS):
        vals = jnp.where(grp == float(g), sel_t, -jnp.inf)
        m1, i1 = _first_index_of_max(vals, row, big)
        m2 = jnp.max(jnp.where(row == i1, -jnp.inf, vals), axis=0, keepdims=True)
        gs = m1 + m2
        if best is None:
            best, bi = gs, jnp.zeros(gs.shape, F32)
        else:
            better = gs > best
            bi = jnp.where(better, float(g), bi)
            best = jnp.where(better, gs, best)
    msel = jnp.where(grp == bi, sel_t, -jnp.inf)
    _, i1 = _first_index_of_max(msel, row, big)
    msel2 = jnp.where(row == i1, -jnp.inf, msel)
    _, i2 = _first_index_of_max(msel2, row, big)
    w = jnp.where((row == i1) | (row == i2), scores_t, 0.0)
    return w / jnp.sum(w, axis=0, keepdims=True)


def _out_kernel(a_ref, b_ref, c_ref, d_ref, x_ref, g1_ref, sh_ref, sc_ref, gffn_ref,
                w_ref, rwh_ref, rwl_ref, rb_ref, xo_ref, h2_ref, gate_ref):
    y = None
    for gi, r in enumerate((a_ref, b_ref, c_ref, d_ref)):
        part = jnp.dot(r[0], w_ref[gi * SEG_W:(gi + 1) * SEG_W, :], preferred_element_type=F32)
        y = part if y is None else y + part
    xn = x_ref[0] + g1_ref[0] * y
    xo_ref[0] = xn
    ms = jnp.mean(xn * xn, axis=-1, keepdims=True)
    h2 = xn * lax.rsqrt(ms + EPS) * gffn_ref[...]
    h2 = h2 * (1.0 + sc_ref[0]) + sh_ref[0]
    hi = h2.astype(BF16)
    h2_ref[0] = hi
    lo = (h2 - hi.astype(F32)).astype(BF16)
    logits = (jnp.dot(hi, rwh_ref[...], preferred_element_type=F32)
              + jnp.dot(lo, rwh_ref[...], preferred_element_type=F32)
              + jnp.dot(hi, rwl_ref[...], preferred_element_type=F32))
    tm = logits.shape[0]
    scores_t = _sigmoid(logits).T[:N_EXPERTS]
    sel_t = scores_t + _tile_lanes(rb_ref[...], tm)
    gates_t = _router_gates(scores_t, sel_t)
    pad = jnp.zeros((LANES - N_EXPERTS, tm), F32)
    gate_ref[0] = jnp.concatenate([gates_t, pad], axis=0).T


def _out_router(parts, x, g1, sh2, sc2, row_of_batch, pw, tm):
    b, n, d = x.shape
    full = lambda a: pl.BlockSpec(a.shape, lambda bi, i: (0,) * a.ndim)
    tok = lambda w: pl.BlockSpec((1, tm, w), lambda bi, i: (bi, i, 0))
    mod_spec = pl.BlockSpec((1, 1, d), lambda bi, i: (row_of_batch(bi), 0, 0))
    consts = [pw["gffn"], pw["w_out"], pw["rw_hi"], pw["rw_lo"], pw["rb"]]
    return pl.pallas_call(
        _out_kernel,
        grid=(b, n // tm),
        in_specs=[tok(SEG_W)] * 4 + [tok(d), mod_spec, mod_spec, mod_spec] + [full(a) for a in consts],
        out_specs=[tok(d), tok(d), tok(LANES)],
        out_shape=[jax.ShapeDtypeStruct((b, n, d), F32), jax.ShapeDtypeStruct((b, n, d), BF16),
                   jax.ShapeDtypeStruct((b, n, LANES), F32)],
        compiler_params=_cparams(("parallel", "parallel")),
        name="out_router",
    )(*parts, x, g1, sh2, sc2, *consts)


MOE_EXPERTS_PER_STEP = 2


def _swiglu_act(gu, scale=None):
    de = gu.shape[1] // 2
    g, u = gu[:, :de], gu[:, de:]
    act = (g * _sigmoid(g)) * u
    return (act if scale is None else act * scale).astype(BF16)


def _moe_kernel(h_ref, gate_ref, x_ref, g2_ref, wgu_ref, wd_ref, sgu_ref, sd_ref, fin_ref,
                o_ref, acc_ref, *, final_norm):
    step = pl.program_id(1)
    k = wgu_ref.shape[0]
    hb = h_ref[...]
    gates = gate_ref[...]
    lane = _lane_iota(gates.shape)
    acts = []
    for i in range(k):
        gu = jnp.dot(hb, wgu_ref[i], preferred_element_type=F32)
        col = jnp.sum(jnp.where(lane == step * k + i, gates, 0.0), axis=1, keepdims=True)
        acts.append(_swiglu_act(gu, col))
    wd = wd_ref[...]
    contrib = jnp.dot(jnp.concatenate(acts, axis=1), wd.reshape(k * wd.shape[1], wd.shape[2]),
                      preferred_element_type=F32)

    @pl.when(step == 0)
    def _():
        shared = _swiglu_act(jnp.dot(hb, sgu_ref[...], preferred_element_type=F32))
        acc_ref[...] = contrib + jnp.dot(shared, sd_ref[...], preferred_element_type=F32)

    @pl.when(step > 0)
    def _():
        acc_ref[...] += contrib

    @pl.when(step == pl.num_programs(1) - 1)
    def _():
        y = x_ref[...] + g2_ref[0] * acc_ref[...]
        if final_norm:
            ms = jnp.mean(y * y, axis=-1, keepdims=True)
            y = y * lax.rsqrt(ms + EPS) * fin_ref[...]
        o_ref[...] = y


def _moe(h2, gates, x, g2, row_of_tile, pw, tm, final_norm):
    t, d = x.shape
    ne, _, de2 = pw["wgu"].shape
    k = MOE_EXPERTS_PER_STEP
    full = lambda a: pl.BlockSpec(a.shape, lambda i, e: (0,) * a.ndim)
    return pl.pallas_call(
        functools.partial(_moe_kernel, final_norm=final_norm),
        grid=(t // tm, ne // k),
        in_specs=[
            pl.BlockSpec((tm, d), lambda i, e: (i, 0)),
            pl.BlockSpec((tm, LANES), lambda i, e: (i, 0)),
            pl.BlockSpec((tm, d), lambda i, e: (i, 0)),
            pl.BlockSpec((1, 1, d), lambda i, e: (row_of_tile(i), 0, 0)),
            pl.BlockSpec((k, d, de2), lambda i, e: (e, 0, 0)),
            pl.BlockSpec((k, de2 // 2, d), lambda i, e: (e, 0, 0)),
            full(pw["sgu"]), full(pw["sd"]), full(pw["fin"]),
        ],
        out_specs=pl.BlockSpec((tm, d), lambda i, e: (i, 0)),
        out_shape=jax.ShapeDtypeStruct((t, d), F32),
        scratch_shapes=[pltpu.VMEM((tm, d), F32)],
        compiler_params=_cparams(("parallel", "arbitrary")),
        name="moe_final" if final_norm else "moe",
    )(h2, gates, x, g2, pw["wgu"], pw["wd"], pw["sgu"], pw["sd"], pw["fin"])


def _axial_angles(n_tok, rot_dim):
    half = rot_dim // 2
    freqs = ROPE_BASE ** (-jnp.arange(0, half, 2, dtype=F32) / half)
    t = jnp.arange(n_tok, dtype=jnp.int32)
    row = (t // GRID_W).astype(F32)
    col = (t % GRID_W).astype(F32)
    return row[:, None] * freqs, col[:, None] * freqs


def _rope_table(n_tok, rot_dim):
    ar, ac = _axial_angles(n_tok, rot_dim)
    cos = jnp.concatenate([jnp.cos(ar)] * 2 + [jnp.cos(ac)] * 2, axis=1)
    sin = jnp.concatenate([-jnp.sin(ar), jnp.sin(ar), -jnp.sin(ac), jnp.sin(ac)], axis=1)
    return cos, sin


def _rope_tables(n_tok):
    cg, sg = _rope_table(n_tok, GQA_DIM)
    cd, sd = _rope_table(n_tok, DIFF_QK_DIM)
    cm32, sm32 = _rope_table(n_tok, MLA_ROPE)
    ones = jnp.ones((n_tok, MLA_NOPE), F32)
    zeros = jnp.zeros((n_tok, MLA_NOPE), F32)
    pad = LANES - MLA_NOPE - MLA_ROPE
    cm = jnp.concatenate([ones, cm32, ones[:, :pad]], axis=1)
    sm = jnp.concatenate([zeros, sm32, zeros[:, :pad]], axis=1)
    return (jnp.tile(cg, (1, LANES // GQA_DIM)), jnp.tile(sg, (1, LANES // GQA_DIM)),
            jnp.tile(cd, (1, LANES // DIFF_QK_DIM)), jnp.tile(sd, (1, LANES // DIFF_QK_DIM)),
            cm, sm)


def _pack_layer(l, p):
    d = p["w_in"].shape[1]
    w = p["w_in"][l]
    o = np.cumsum([0, 256, 128, 128, 256, 256, 256, 256, 256, 256, MLA_Q_LORA, MLA_KV_LORA, MLA_ROPE])
    col = lambda i: w[:, int(o[i]):int(o[i + 1])]
    dup = lambda a: jnp.concatenate([a[:, :GQA_DIM], a[:, :GQA_DIM], a[:, GQA_DIM:], a[:, GQA_DIM:]], axis=1)
    z = lambda n: jnp.zeros((d, n), F32)
    w_in = jnp.concatenate(
        [col(0), dup(col(1)), dup(col(2)), col(3), col(4), col(5), col(6), col(7), col(8),
         col(9), z(SEG_W - MLA_Q_LORA), col(10),
         z(MLA_NOPE), col(11), z(LANES - MLA_NOPE - MLA_ROPE)], axis=1).astype(BF16)
    dq = MLA_NOPE + MLA_ROPE
    wqu = p["mla_w_q_up"][l].reshape(MLA_Q_LORA, MLA_HEADS, dq)
    wqu = jnp.pad(wqu, ((0, SEG_W - MLA_Q_LORA), (0, 0), (0, LANES - dq)))
    wkv = p["mla_w_kv_up"][l].reshape(MLA_KV_LORA, MLA_HEADS, MLA_NOPE + MLA_V)
    wkk = jnp.pad(wkv[:, :, :MLA_NOPE], ((0, 0), (0, 0), (0, LANES - MLA_NOPE)))
    e = p["moe_w_gate"].shape[1]
    rw = jnp.pad(p["router_w"], ((0, 0), (0, LANES - e)))
    rw_hi = rw.astype(BF16)
    rw_lo = (rw - rw_hi.astype(F32)).astype(BF16)
    return {
        "gmix": p["norm_mix"][l][None], "gffn": p["norm_ffn"][l][None],
        "w_in": w_in,
        "wq_up": wqu.reshape(SEG_W, MLA_HEADS * LANES).astype(BF16),
        "wkv_k": wkk.reshape(MLA_KV_LORA, MLA_HEADS * LANES).astype(BF16),
        "wkv_v": wkv[:, :, MLA_NOPE:].reshape(MLA_KV_LORA, MLA_HEADS * MLA_V).astype(BF16),
        "gq": (jnp.tile(p["gqa_q_norm"][l], GQA_HEADS) * (GQA_DIM ** -0.5 * LOG2E))[None],
        "gk": jnp.tile(p["gqa_k_norm"][l], GQA_HEADS)[None],
        "gcq": jnp.pad(p["mla_q_norm"][l], (0, SEG_W - MLA_Q_LORA))[None],
        "gckv": p["mla_kv_norm"][l][None],
        "w_out": p["w_out"][l].astype(BF16),
        "rw_hi": rw_hi, "rw_lo": rw_lo,
        "rb": jnp.broadcast_to(p["router_b"][:, None], (e, LANES)),
        "wgu": jnp.concatenate([p["moe_w_gate"][l], p["moe_w_up"][l]], axis=2).astype(BF16),
        "wd": p["moe_w_down"][l].astype(BF16),
        "sgu": jnp.concatenate([p["shared_w_gate"][l], p["shared_w_up"][l]], axis=1).astype(BF16),
        "sd": p["shared_w_down"][l].astype(BF16),
        "fin": p["final_norm"][None],
        "diff": (p["diff_lq1"][l][None], p["diff_lk1"][l][None], p["diff_lq2"][l][None],
                 p["diff_lk2"][l][None], p["diff_subln"][l][:, None]),
    }


def kernel(x, c, ctx, c_ctx, w_mod, b_mod, norm_mix, norm_ffn, w_in, w_out, gqa_q_norm, gqa_k_norm,
           na_rel_bias, diff_lq1, diff_lk1, diff_lq2, diff_lk2, diff_subln, mla_q_norm, mla_w_q_up,
           mla_kv_norm, mla_w_kv_up, router_w, router_b, moe_w_gate, moe_w_up, moe_w_down,
           shared_w_gate, shared_w_up, shared_w_down, final_norm):
    p = dict(w_in=w_in, norm_mix=norm_mix, norm_ffn=norm_ffn, w_out=w_out, gqa_q_norm=gqa_q_norm,
             gqa_k_norm=gqa_k_norm, diff_lq1=diff_lq1, diff_lk1=diff_lk1, diff_lq2=diff_lq2,
             diff_lk2=diff_lk2, diff_subln=diff_subln, mla_q_norm=mla_q_norm, mla_w_q_up=mla_w_q_up,
             mla_kv_norm=mla_kv_norm, mla_w_kv_up=mla_w_kv_up, router_w=router_w, router_b=router_b,
             moe_w_gate=moe_w_gate, moe_w_up=moe_w_up, moe_w_down=moe_w_down,
             shared_w_gate=shared_w_gate, shared_w_up=shared_w_up, shared_w_down=shared_w_down,
             final_norm=final_norm)
    b, s, d = x.shape
    nctx = ctx.shape[1]
    depth = w_mod.shape[0]
    rows = s // GRID_W
    assert s % TOK_TILE == 0 and s // NA_TILE >= NA_CHUNKS + NA_QCHUNKS
    assert nctx % LANES == 0 and (b * s) % MOE_TILE == 0 and s % MOE_TILE == 0

    mrows = -(-(b + 1) // SUBLANES) * SUBLANES
    cc = jnp.zeros((mrows, d), F32).at[:b].set(c).at[b].set(c_ctx)
    mod = _modulation(cc, w_mod, b_mod)
    tables = _rope_tables(s)
    lat_row = lambda bi: bi
    ctx_row = lambda bi: b
    tm_ctx = b * nctx if b * nctx <= MOE_TILE else nctx

    xc = ctx
    for l in range(depth):
        want_ctx = l < depth - 1
        lam_init = 0.8 - 0.6 * math.exp(-0.3 * l)
        pw = _pack_layer(l, p)
        m6 = mod[l].reshape(mrows, 6, 1, d)
        sh1, sc1, g1, sh2, sc2, g2 = (m6[:, k] for k in range(6))

        lat = _project(x, sh1, sc1, lat_row, pw, tables, TOK_TILE, LAT_OUTS)
        cx = _project(xc, sh1, sc1, ctx_row, pw, None, nctx, CTX_OUTS if want_ctx else CTX_KV_OUTS)

        def dense(g, kind, extras=()):
            return _flash_lat(lat[g + ".qT"], cx[g + ".k"], cx[g + ".vT"], lat[g + ".k"], lat[g + ".vT"],
                              kind, extras, lam_init)

        def ctx_only(g, kind, extras=()):
            return _flash_ctx(cx[g + ".qT"], cx[g + ".k"], cx[g + ".vT"], kind, extras, lam_init)

        a_lat = dense("A", "pair")
        bias = _na_bias_tiles(na_rel_bias[l], rows, NA_TILE)
        b_lat = _na_attention(lat["B.qT"], cx["B.k"], cx["B.vT"], lat["B.k"], lat["B.vT"], bias)
        c_lat = dense("C", "diff", pw["diff"])
        d_lat = dense("D", "mla")
        x1, h2, gates = _out_router((a_lat, b_lat, c_lat, d_lat), x, g1, sh2, sc2, lat_row, pw, TOK_TILE)
        tiles_per_batch = s // MOE_TILE
        x = _moe(h2.reshape(b * s, d), gates.reshape(b * s, LANES), x1.reshape(b * s, d), g2,
                 lambda i: i // tiles_per_batch, pw, MOE_TILE, not want_ctx).reshape(b, s, d)

        if want_ctx:
            a_c = ctx_only("A", "pair")
            b_c = ctx_only("B", "pair")
            c_c = ctx_only("C", "diff", pw["diff"])
            d_c = ctx_only("D", "mla")
            xc1, h2c, gates_c = _out_router((a_c, b_c, c_c, d_c), xc, g1, sh2, sc2, ctx_row, pw, nctx)
            xc = _moe(h2c.reshape(b * nctx, d), gates_c.reshape(b * nctx, LANES),
                      xc1.reshape(b * nctx, d), g2, lambda i: b, pw, tm_ctx, False).reshape(b, nctx, d)
    return x
```

```python
import functools
import math

import numpy as np
import jax
import jax.numpy as jnp
from jax import lax
from jax.experimental import pallas as pl
from jax.experimental.pallas import tpu as pltpu

F32 = jnp.float32
BF16 = jnp.bfloat16

GRID_W = 64
EPS = 1e-6
ROPE_BASE = 10000.0
GQA_HEADS, GQA_KV_HEADS, GQA_DIM = 4, 2, 64
NA_HEADS, NA_DIM, NA_WIN_ROWS, NA_WIN_COLS = 4, 64, 8, 16
DIFF_HEADS, DIFF_QK_DIM, DIFF_V_DIM = 4, 32, 64
MLA_HEADS, MLA_NOPE, MLA_ROPE, MLA_V, MLA_Q_LORA, MLA_KV_LORA = 4, 64, 32, 64, 192, 128
N_EXPERTS, MOE_GROUPS, D_EXPERT = 16, 4, 256
EXPERTS_PER_GROUP = N_EXPERTS // MOE_GROUPS

LANES = 128
SUBLANES = 8
VMEM_LIMIT = 56 * 1024 * 1024

TOK_TILE = 512
NA_TILE = 256
MOE_TILE = 1024
MOD_TILE = 1024
NEG = -1e30
LOG2E = math.log2(math.e)

SEG_W = 256
OFF_AQ, OFF_AK, OFF_AV = 0, 256, 512
OFF_BQ, OFF_BK, OFF_BV = 768, 1024, 1280
OFF_CQ, OFF_CK, OFF_CV = 1536, 1792, 2048
OFF_DQ, OFF_DKV, OFF_DPE = 2304, 2560, 2688
IN_PACKED = 2816


def _cparams(sem):
    return pltpu.CompilerParams(dimension_semantics=sem, vmem_limit_bytes=VMEM_LIMIT)


def _lane_iota(shape):
    return lax.broadcasted_iota(jnp.int32, shape, 1)


def _lane_group(shape, width):
    return lax.shift_right_logical(_lane_iota(shape), int(math.log2(width)))


def _sigmoid(x):
    return 1.0 / (1.0 + jnp.exp(-x))


def _mod_kernel(c_ref, w_ref, b_ref, o_ref):
    c = c_ref[...]
    s = c * _sigmoid(c)
    o_ref[0] = jnp.dot(s, w_ref[0], preferred_element_type=F32,
                       precision=lax.Precision.HIGHEST) + b_ref[0]


def _modulation(cc, w_mod, b_mod):
    depth, d, n = w_mod.shape
    rows = cc.shape[0]
    return pl.pallas_call(
        _mod_kernel,
        grid=(depth, n // MOD_TILE),
        in_specs=[
            pl.BlockSpec((rows, d), lambda l, j: (0, 0)),
            pl.BlockSpec((1, d, MOD_TILE), lambda l, j: (l, 0, j)),
            pl.BlockSpec((1, 1, MOD_TILE), lambda l, j: (l, 0, j)),
        ],
        out_specs=pl.BlockSpec((1, rows, MOD_TILE), lambda l, j: (l, 0, j)),
        out_shape=jax.ShapeDtypeStruct((depth, rows, n), F32),
        compiler_params=_cparams(("parallel", "parallel")),
        name="modulation",
    )(cc, w_mod, b_mod.reshape(depth, 1, n))


def _swap_halves(x, half):
    w = x.shape[1]
    lane = _lane_iota(x.shape)
    fwd = pltpu.roll(x, half, 1)
    bwd = pltpu.roll(x, w - half, 1)
    return jnp.where((lane & (2 * half - 1)) < half, bwd, fwd)


def _rope(x, cos, sin, half):
    reps = x.shape[1] // cos.shape[1]
    c = jnp.concatenate([cos] * reps, axis=1) if reps > 1 else cos
    s = jnp.concatenate([sin] * reps, axis=1) if reps > 1 else sin
    return x * c + _swap_halves(x, half) * s


def _group_rms(x, group, count):
    gid = _lane_group(x.shape, group)
    x2 = x * x
    inv = jnp.zeros_like(x)
    for g in range(x.shape[1] // group):
        msk = gid == g
        ms = jnp.sum(jnp.where(msk, x2, 0.0), axis=1, keepdims=True) * (1.0 / count)
        inv = jnp.where(msk, lax.rsqrt(ms + EPS), inv)
    return x * inv


LAT_OUTS = ("A.qT", "A.k", "A.vT", "B.qT", "B.k", "B.vT", "C.qT", "C.k", "C.vT", "D.qT", "D.k", "D.vT")
CTX_KV_OUTS = ("A.k", "A.vT", "B.k", "B.vT", "C.k", "C.vT", "D.k", "D.vT")
CTX_OUTS = CTX_KV_OUTS + ("A.qT", "B.qT", "C.qT", "D.qT")
GROUP_W = {"A": SEG_W, "B": SEG_W, "C": SEG_W, "D": 2 * SEG_W}


def _proj_kernel(*refs, rope, outs):
    (x_ref, sh_ref, sc_ref, gmix_ref, w_ref, wqu_ref, wkk_ref, wkv_ref,
     gq_ref, gk_ref, gcq_ref, gckv_ref) = refs[:12]
    pos = 12
    if rope:
        cg_ref, sg_ref, cd_ref, sd_ref, cm_ref, sm_ref = refs[pos:pos + 6]
        pos += 6
    out = dict(zip(outs, refs[pos:pos + len(outs)]))

    def want(g, t):
        return (g + "." + t) in out or (g + "." + t + "T") in out

    def put(g, t, val):
        if g + "." + t in out:
            out[g + "." + t][0] = val.astype(BF16)
        if g + "." + t + "T" in out:
            ref = out[g + "." + t + "T"]
            vt = val.T.astype(BF16)
            cw = ref.shape[3]
            for c in range(ref.shape[1]):
                ref[0, c] = vt[:, c * cw:(c + 1) * cw]

    x = x_ref[0]
    ms = jnp.mean(x * x, axis=-1, keepdims=True)
    h = x * lax.rsqrt(ms + EPS) * gmix_ref[...]
    h = h * (1.0 + sc_ref[0]) + sh_ref[0]
    hb = h.astype(BF16)

    def seg(off, width=SEG_W):
        return jnp.dot(hb, w_ref[:, off:off + width], preferred_element_type=F32)

    if want("A", "q"):
        q = _group_rms(seg(OFF_AQ), GQA_DIM, GQA_DIM) * gq_ref[...]
        if rope:
            q = _rope(q, cg_ref[...], sg_ref[...], GQA_DIM // 4)
        put("A", "q", q)
    k = _group_rms(seg(OFF_AK), GQA_DIM, GQA_DIM) * gk_ref[...]
    if rope:
        k = _rope(k, cg_ref[...], sg_ref[...], GQA_DIM // 4)
    put("A", "k", k)
    put("A", "v", seg(OFF_AV))

    if want("B", "q"):
        put("B", "q", seg(OFF_BQ) * (NA_DIM ** -0.5 * LOG2E))
    put("B", "k", seg(OFF_BK))
    put("B", "v", seg(OFF_BV))

    if want("C", "q"):
        q = seg(OFF_CQ) * (DIFF_QK_DIM ** -0.5 * LOG2E)
        if rope:
            q = _rope(q, cd_ref[...], sd_ref[...], DIFF_QK_DIM // 4)
        put("C", "q", q)
    k = seg(OFF_CK)
    if rope:
        k = _rope(k, cd_ref[...], sd_ref[...], DIFF_QK_DIM // 4)
    put("C", "k", k)
    put("C", "v", seg(OFF_CV))

    if want("D", "q"):
        cq = _group_rms(seg(OFF_DQ), SEG_W, MLA_Q_LORA) * gcq_ref[...]
        q = jnp.dot(cq.astype(BF16), wqu_ref[...], preferred_element_type=F32)
        q = q * ((MLA_NOPE + MLA_ROPE) ** -0.5 * LOG2E)
        if rope:
            q = _rope(q, cm_ref[...], sm_ref[...], MLA_ROPE // 4)
        put("D", "q", q)
    ckv = _group_rms(seg(OFF_DKV, LANES), LANES, MLA_KV_LORA) * gckv_ref[...]
    ckvb = ckv.astype(BF16)
    k = jnp.dot(ckvb, wkk_ref[...], preferred_element_type=F32)
    kpe = seg(OFF_DPE, LANES)
    if rope:
        kpe = _rope(kpe, cm_ref[...], sm_ref[...], MLA_ROPE // 4)
    put("D", "k", k + jnp.concatenate([kpe] * MLA_HEADS, axis=1))
    put("D", "v", jnp.dot(ckvb, wkv_ref[...], preferred_element_type=F32))


def _project(x, sh, sc, row_of_batch, pw, tables, tm, outs):
    b, n, d = x.shape
    nt = n // tm
    rope = tables is not None
    full = lambda a: pl.BlockSpec(a.shape, lambda bi, i: (0,) * a.ndim)
    mod_spec = pl.BlockSpec((1, 1, d), lambda bi, i: (row_of_batch(bi), 0, 0))
    ins = [x, sh, sc, pw["gmix"], pw["w_in"], pw["wq_up"], pw["wkv_k"], pw["wkv_v"],
           pw["gq"], pw["gk"], pw["gcq"], pw["gckv"]]
    in_specs = [pl.BlockSpec((1, tm, d), lambda bi, i: (bi, i, 0)), mod_spec, mod_spec]
    in_specs += [full(a) for a in ins[3:]]
    if rope:
        ins += list(tables)
        in_specs += [pl.BlockSpec((tm, LANES), lambda bi, i: (i, 0)) for _ in tables]
    shapes, specs = [], []
    for name in outs:
        g, t = name.split(".")
        w = SEG_W if t[0] == "v" else GROUP_W[g]
        if t.endswith("T"):
            cw = min(tm, NA_TILE) if g == "B" else tm
            shapes.append(jax.ShapeDtypeStruct((b, n // cw, w, cw), BF16))
            specs.append(pl.BlockSpec((1, tm // cw, w, cw), lambda bi, i: (bi, i, 0, 0)))
        else:
            shapes.append(jax.ShapeDtypeStruct((b, n, w), BF16))
            specs.append(pl.BlockSpec((1, tm, w), lambda bi, i: (bi, i, 0)))
    res = pl.pallas_call(
        functools.partial(_proj_kernel, rope=rope, outs=tuple(outs)),
        grid=(b, nt),
        in_specs=in_specs,
        out_specs=specs,
        out_shape=shapes,
        compiler_params=_cparams(("parallel", "parallel")),
        name="proj_rope" if rope else "proj_ctx",
    )(*ins)
    return dict(zip(outs, res))


HEAD_V = 64


def _tile_lanes(x, width):
    reps = width // x.shape[1]
    return jnp.tile(x, (1, reps)) if reps > 1 else x


def with_ones(vt):
    rows = lax.broadcasted_iota(jnp.int32, (LANES - HEAD_V, vt.shape[1]), 0)
    ones = jnp.where(rows == 0, 1.0, 0.0).astype(BF16)
    return jnp.concatenate([vt, ones], axis=0)


def _map_slices(kind):
    def k_lanes(m):
        return slice(m * LANES, (m + 1) * LANES) if kind == "mla" else slice(None)

    def v_rows(m):
        hv = m // 2 if kind == "diff" else m
        return slice(hv * HEAD_V, (hv + 1) * HEAD_V)

    return k_lanes, v_rows


def _query_slabs(qt_ref, kind, nm):
    qv = jnp.concatenate([qt_ref[0, c] for c in range(qt_ref.shape[1])], axis=1)
    if kind == "mla":
        return [qv[m * LANES:(m + 1) * LANES, :] for m in range(nm)]
    rows = lax.broadcasted_iota(jnp.int32, qv.shape, 0)
    grp = lax.shift_right_logical(rows, int(math.log2(LANES // nm)))
    return [jnp.where(grp == m, qv, jnp.zeros_like(qv)) for m in range(nm)]


def _finish_heads(accs, kind, lam_init, diff_refs):
    outs = [a[0:HEAD_V] / a[HEAD_V:HEAD_V + 1] for a in accs]
    if kind == "diff":
        lq1_ref, lk1_ref, lq2_ref, lk2_ref, sub_ref = diff_refs
        lam = (jnp.exp(jnp.sum(lq1_ref[...] * lk1_ref[...], axis=1, keepdims=True))
               - jnp.exp(jnp.sum(lq2_ref[...] * lk2_ref[...], axis=1, keepdims=True))
               + lam_init)
        heads = []
        for hd in range(2):
            dlt = outs[2 * hd] - lam * outs[2 * hd + 1]
            ms2 = jnp.mean(dlt * dlt, axis=0, keepdims=True)
            heads.append(dlt * lax.rsqrt(ms2 + EPS) * sub_ref[...] * (1.0 - lam_init))
        outs = heads
    return jnp.concatenate(outs, axis=0).T


def _flash_ctx_kernel(*refs, kind, lam_init):
    qt_ref, kc_ref, vct_ref = refs[:3]
    nd = 5 if kind == "diff" else 0
    diff_refs, o_ref = refs[3:3 + nd], refs[3 + nd]
    nm = 4 if kind == "diff" else 2
    k_lanes, v_rows = _map_slices(kind)
    accs = []
    for m, qm in enumerate(_query_slabs(qt_ref, kind, nm)):
        st = jnp.dot(kc_ref[0, :, k_lanes(m)], qm, preferred_element_type=F32)
        p = jnp.exp2(st - jnp.max(st, axis=0, keepdims=True)).astype(BF16)
        accs.append(jnp.dot(with_ones(vct_ref[0, 0, v_rows(m), :]), p, preferred_element_type=F32))
    o_ref[0] = _finish_heads(accs, kind, lam_init, diff_refs).astype(o_ref.dtype)


def _flash_lat_kernel(*refs, kind, lam_init):
    qt_ref, kc_ref, vct_ref, k_ref, vt_ref = refs[:5]
    nd = 5 if kind == "diff" else 0
    diff_refs, o_ref = refs[5:5 + nd], refs[5 + nd]
    qs_e, acc_e, s_e, qs_l, acc_l, s_l, m_ref, mc_ref = refs[6 + nd:14 + nd]
    early = (qs_e, acc_e, s_e, 0, False)
    late = (qs_l, acc_l, s_l, vt_ref.shape[1] // 2, True)

    nm = qs_e.shape[0]
    nk, tk = vt_ref.shape[1], vt_ref.shape[3]
    half = nk // 2
    iters = half // 2
    g = pl.program_id(2)
    nq = pl.num_programs(2) - 1
    k_lanes, v_rows = _map_slices(kind)

    def produce(tile, slot, j, m):
        qs_ref, _, s_ref = tile[:3]
        rows = pl.ds(pl.multiple_of(j * tk, tk), tk)
        st = jnp.dot(k_ref[0, rows, k_lanes(m)], qs_ref[m], preferred_element_type=F32)
        s_ref[slot, m] = st
        return jnp.max(st, axis=0, keepdims=True)

    def consume(tile, slot, j, m, m_prev, m_chunk):
        _, acc_ref, s_ref = tile[:3]
        m_new = jnp.maximum(m_prev, m_chunk)
        alpha = jnp.exp2(m_prev - m_new)
        p = jnp.exp2(s_ref[slot, m] - m_new).astype(BF16)
        acc_ref[m] = alpha * acc_ref[m] + jnp.dot(
            with_ones(vt_ref[0, j, v_rows(m), :]), p, preferred_element_type=F32)
        return m_new

    def iteration(i, carries, tiles, last):
        maps = range(nm)
        j0s = [tile[3] + 2 * i for tile in tiles]
        c1 = [[produce(tile, 1, j0 + 1, m) for m in maps] for tile, j0 in zip(tiles, j0s)]
        m1 = [[consume(tile, 0, j0, m, c[0][m], c[1][m]) for m in maps]
              for tile, j0, c in zip(tiles, j0s, carries)]
        c2 = [c1[t] if (last and tile[4]) else [produce(tile, 0, j0 + 2, m) for m in maps]
              for t, (tile, j0) in enumerate(zip(tiles, j0s))]
        m2 = [[consume(tile, 1, j0 + 1, m, m1[t][m], c1[t][m]) for m in maps]
              for t, (tile, j0) in enumerate(zip(tiles, j0s))]
        return tuple((tuple(a), tuple(b)) for a, b in zip(m2, c2))

    def run(carries, tiles):
        carries = iteration(0, carries, tiles, iters == 1)
        if iters > 2:
            carries = lax.fori_loop(1, iters - 1, lambda i, c: iteration(i, c, tiles, False), carries)
        if iters > 1:
            carries = iteration(iters - 1, carries, tiles, True)
        return carries

    def start_early():
        for m, qm in enumerate(_query_slabs(qt_ref, kind, nm)):
            qs_e[m] = qm
        sts = [jnp.dot(kc_ref[0, :, k_lanes(m)], qs_e[m], preferred_element_type=F32)
               for m in range(nm)]
        ms = tuple(jnp.max(st, axis=0, keepdims=True) for st in sts)
        first = tuple(produce(early, 0, 0, m) for m in range(nm))

        def init_from_context():
            for m in range(nm):
                p = jnp.exp2(sts[m] - ms[m]).astype(BF16)
                acc_e[m] = jnp.dot(with_ones(vct_ref[0, 0, v_rows(m), :]), p,
                                   preferred_element_type=F32)

        return (ms, first), init_from_context

    def load_late():
        return (tuple(m_ref[m] for m in range(nm)), tuple(mc_ref[m] for m in range(nm)))

    def hand_over(carry):
        qs_l[...] = qs_e[...]
        acc_l[...] = acc_e[...]
        s_l[0] = s_e[0]
        for m in range(nm):
            m_ref[m] = carry[0][m]
            mc_ref[m] = carry[1][m]

    def finish_late():
        accs = [acc_l[m] for m in range(nm)]
        o_ref[0] = _finish_heads(accs, kind, lam_init, diff_refs).astype(o_ref.dtype)

    @pl.when(g == 0)
    def _():
        carry, init_from_context = start_early()
        init_from_context()
        hand_over(run((carry,), (early,))[0])

    @pl.when((g > 0) & (g < nq))
    def _():
        carry_l = load_late()
        carry_e, init_from_context = start_early()
        init_from_context()
        carry_l, carry_e = run((carry_l, carry_e), (late, early))
        finish_late()
        hand_over(carry_e)

    @pl.when(g == nq)
    def _():
        run((load_late(),), (late,))
        finish_late()


def _flash_specs(qt, kc, vct, kind, extras):
    b, nqc, wtot, cw = qt.shape
    wq = 2 * LANES if kind == "mla" else LANES
    c = kc.shape[1]
    extra_specs = [pl.BlockSpec(e.shape, lambda bi, p, i: (0, 0)) for e in extras]
    kv_specs = [pl.BlockSpec((1, c, wq), lambda bi, p, i: (bi, 0, p)),
                pl.BlockSpec((1, 1, LANES, c), lambda bi, p, i: (bi, 0, p, 0))]
    return b, nqc, wtot // wq, wq, cw, kv_specs, extra_specs


def _flash_ctx(qt, kc, vct, kind, extras=(), lam_init=0.0):
    b, nqc, npair, wq, cw, kv_specs, extra_specs = _flash_specs(qt, kc, vct, kind, extras)
    return pl.pallas_call(
        functools.partial(_flash_ctx_kernel, kind=kind, lam_init=lam_init),
        grid=(b, npair, nqc),
        in_specs=[pl.BlockSpec((1, 1, wq, cw), lambda bi, p, i: (bi, i, p, 0))] + kv_specs + extra_specs,
        out_specs=pl.BlockSpec((1, cw, LANES), lambda bi, p, i: (bi, i, p)),
        out_shape=jax.ShapeDtypeStruct((b, nqc * cw, npair * LANES), BF16),
        compiler_params=_cparams(("parallel", "parallel", "parallel")),
        name="flash_" + kind + "_ctx",
    )(qt, kc, vct, *extras)


def _flash_lat(qt, kc, vct, k, vt, kind, extras=(), lam_init=0.0):
    b, nq, npair, wq, tq, kv_specs, extra_specs = _flash_specs(qt, kc, vct, kind, extras)
    nm = 4 if kind == "diff" else 2
    nk, tk = vt.shape[1], vt.shape[3]
    assert nk % 4 == 0, "two phases of key chunks, processed in pairs"
    return pl.pallas_call(
        functools.partial(_flash_lat_kernel, kind=kind, lam_init=lam_init),
        grid=(b, npair, nq + 1),
        in_specs=[pl.BlockSpec((1, 1, wq, tq), lambda bi, p, g: (bi, jnp.minimum(g, nq - 1), p, 0))]
        + kv_specs
        + [pl.BlockSpec((1, k.shape[1], wq), lambda bi, p, g: (bi, 0, p)),
           pl.BlockSpec((1, nk, LANES, tk), lambda bi, p, g: (bi, 0, p, 0))]
        + extra_specs,
        out_specs=pl.BlockSpec((1, tq, LANES), lambda bi, p, g: (bi, jnp.maximum(g - 1, 0), p)),
        out_shape=jax.ShapeDtypeStruct((b, nq * tq, npair * LANES), BF16),
        scratch_shapes=2 * [
            pltpu.VMEM((nm, LANES, tq), BF16),
            pltpu.VMEM((nm, LANES, tq), F32),
            pltpu.VMEM((2, nm, tk, tq), F32),
        ] + 2 * [pltpu.VMEM((nm, 1, tq), F32)],
        compiler_params=_cparams(("parallel", "parallel", "arbitrary")),
        name="flash_" + kind,
    )(qt, kc, vct, k, vt, *extras)


NA_QCHUNKS = 2
NA_CHUNKS = NA_QCHUNKS + 2


def _na_kernel(qt_ref, k_ref, vt_ref, kc_ref, vct_ref, bias_ref, o_ref):
    nkc, tk = vt_ref.shape[1], vt_ref.shape[3]
    i = pl.program_id(2)
    c0 = jnp.clip(i * NA_QCHUNKS - 1, 0, nkc - NA_CHUNKS)
    qv = jnp.concatenate([qt_ref[0, c] for c in range(NA_QCHUNKS)], axis=1)
    rows = lax.broadcasted_iota(jnp.int32, qv.shape, 0)
    outs = []
    for m in range(2):
        own = lax.shift_right_logical(rows, int(math.log2(LANES // 2))) == m
        qm = jnp.where(own, qv, jnp.zeros_like(qv))
        v_rows = slice(m * HEAD_V, (m + 1) * HEAD_V)
        ss = []
        for j in range(NA_CHUNKS):
            kj = k_ref[0, pl.ds(pl.multiple_of((c0 + j) * tk, tk), tk), :]
            ss.append(jnp.dot(kj, qm, preferred_element_type=F32) + bias_ref[0, m, j])
        sc = jnp.dot(kc_ref[0], qm, preferred_element_type=F32)
        mx = jnp.max(sc, axis=0, keepdims=True)
        for st in ss:
            mx = jnp.maximum(mx, jnp.max(st, axis=0, keepdims=True))
        acc = jnp.dot(with_ones(vct_ref[0, 0, v_rows, :]), jnp.exp2(sc - mx).astype(BF16),
                      preferred_element_type=F32)
        for j, st in enumerate(ss):
            acc = acc + jnp.dot(with_ones(vt_ref[0, c0 + j, v_rows, :]),
                                jnp.exp2(st - mx).astype(BF16), preferred_element_type=F32)
        outs.append(acc[0:HEAD_V] / acc[HEAD_V:HEAD_V + 1])
    o_ref[0] = jnp.concatenate(outs, axis=0).T.astype(o_ref.dtype)


def _na_bias_tiles(rel_bias, rows, tk):
    qrows = NA_QCHUNKS * tk // GRID_W
    nblk = rows // qrows
    nk = rows * GRID_W // tk
    krows = NA_CHUNKS * tk // GRID_W
    kh = min(NA_WIN_ROWS, rows)
    kw = NA_WIN_COLS
    blocks = np.array([0, 1, nblk - 1])
    c0 = np.clip(blocks * NA_QCHUNKS - 1, 0, nk - NA_CHUNKS)
    r = blocks[:, None] * qrows + np.arange(qrows)[None, :]
    kr = c0[:, None] * (tk // GRID_W) + np.arange(krows)[None, :]
    r0 = np.clip(r - kh // 2, 0, rows - kh)
    dr = kr[:, None, :] - r[:, :, None]
    valid_r = (kr[:, None, :] >= r0[:, :, None]) & (kr[:, None, :] < r0[:, :, None] + kh)
    cols = np.arange(GRID_W)
    cstart = np.clip(cols - kw // 2, 0, GRID_W - kw)
    dc = cols[None, :] - cols[:, None]
    valid_c = (cols[None, :] >= cstart[:, None]) & (cols[None, :] < cstart[:, None] + kw)
    ndc = 2 * NA_WIN_COLS - 1
    onehot = ((dc[None] + NA_WIN_COLS - 1) == np.arange(ndc)[:, None, None]) & valid_c[None]
    dr_idx = np.clip(dr + NA_WIN_ROWS - 1, 0, 2 * NA_WIN_ROWS - 2)
    g1 = rel_bias[:, dr_idx, :]
    tiles = jnp.einsum("htabd,dcx->thbxac", g1, jnp.asarray(onehot, F32),
                       precision=lax.Precision.HIGHEST)
    valid = valid_r.transpose(0, 2, 1)[:, None, :, None, :, None] & valid_c.T[None, None, None, :, None, :]
    tiles = jnp.where(jnp.asarray(valid), tiles * LOG2E, NEG)
    return tiles.reshape(3, rel_bias.shape[0], NA_CHUNKS, tk, qrows * GRID_W)


def _na_attention(qt, kc, vct, k, vt, bias):
    b, nqc, wtot, cw = qt.shape
    nblk, tq = nqc // NA_QCHUNKS, NA_QCHUNKS * cw
    npair = wtot // LANES
    nkc, tk = vt.shape[1], vt.shape[3]
    c = kc.shape[1]
    n = k.shape[1]

    def bias_map(bi, p, i):
        return (jnp.where(i == 0, 0, jnp.where(i == nblk - 1, 2, 1)), p, 0, 0, 0)

    return pl.pallas_call(
        _na_kernel,
        grid=(b, npair, nblk),
        in_specs=[
            pl.BlockSpec((1, NA_QCHUNKS, LANES, cw), lambda bi, p, i: (bi, i, p, 0)),
            pl.BlockSpec((1, n, LANES), lambda bi, p, i: (bi, 0, p)),
            pl.BlockSpec((1, nkc, LANES, tk), lambda bi, p, i: (bi, 0, p, 0)),
            pl.BlockSpec((1, c, LANES), lambda bi, p, i: (bi, 0, p)),
            pl.BlockSpec((1, 1, LANES, c), lambda bi, p, i: (bi, 0, p, 0)),
            pl.BlockSpec((1, 2, NA_CHUNKS, tk, tq), bias_map),
        ],
        out_specs=pl.BlockSpec((1, tq, LANES), lambda bi, p, i: (bi, i, p)),
        out_shape=jax.ShapeDtypeStruct((b, nblk * tq, wtot), BF16),
        compiler_params=_cparams(("parallel", "parallel", "parallel")),
        name="na_attention",
    )(qt, k, vt, kc, vct, bias)


def _first_index_of_max(vals, row, big):
    mx = jnp.max(vals, axis=0, keepdims=True)
    idx = jnp.min(jnp.where(vals == mx, row, big), axis=0, keepdims=True)
    return mx, idx


def _router_gates(scores_t, sel_t):
    irow = lax.broadcasted_iota(jnp.int32, sel_t.shape, 0)
    row = irow.astype(F32)
    grp = lax.shift_right_logical(irow, int(math.log2(EXPERTS_PER_GROUP))).astype(F32)
    big = float(N_EXPERTS)
    best = None
    for g in range(MOE_GROUPS):
        vals = jnp.where(grp == float(g), sel_t, -jnp.inf)
        m1, i1 = _first_index_of_max(vals, row, big)
        m2 = jnp.max(jnp.where(row == i1, -jnp.inf, vals), axis=0, keepdims=True)
        gs = m1 + m2
        if best is None:
            best, bi = gs, jnp.zeros(gs.shape, F32)
        else:
            better = gs > best
            bi = jnp.where(better, float(g), bi)
            best = jnp.where(better, gs, best)
    msel = jnp.where(grp == bi, sel_t, -jnp.inf)
    _, i1 = _first_index_of_max(msel, row, big)
    msel2 = jnp.where(row == i1, -jnp.inf, msel)
    _, i2 = _first_index_of_max(msel2, row, big)
    w = jnp.where((row == i1) | (row == i2), scores_t, 0.0)
    return w / jnp.sum(w, axis=0, keepdims=True)


def _out_kernel(a_ref, b_ref, c_ref, d_ref, x_ref, g1_ref, sh_ref, sc_ref, gffn_ref,
                w_ref, rwh_ref, rwl_ref, rb_ref, xo_ref, h2_ref, gate_ref):
    y = None
    for gi, r in enumerate((a_ref, b_ref, c_ref, d_ref)):
        part = jnp.dot(r[0], w_ref[gi * SEG_W:(gi + 1) * SEG_W, :], preferred_element_type=F32)
        y = part if y is None else y + part
    xn = x_ref[0] + g1_ref[0] * y
    xo_ref[0] = xn
    ms = jnp.mean(xn * xn, axis=-1, keepdims=True)
    h2 = xn * lax.rsqrt(ms + EPS) * gffn_ref[...]
    h2 = h2 * (1.0 + sc_ref[0]) + sh_ref[0]
    hi = h2.astype(BF16)
    h2_ref[0] = hi
    lo = (h2 - hi.astype(F32)).astype(BF16)
    logits = (jnp.dot(hi, rwh_ref[...], preferred_element_type=F32)
              + jnp.dot(lo, rwh_ref[...], preferred_element_type=F32)
              + jnp.dot(hi, rwl_ref[...], preferred_element_type=F32))
    tm = logits.shape[0]
    scores_t = _sigmoid(logits).T[:N_EXPERTS]
    sel_t = scores_t + _tile_lanes(rb_ref[...], tm)
    gates_t = _router_gates(scores_t, sel_t)
    pad = jnp.zeros((LANES - N_EXPERTS, tm), F32)
    gate_ref[0] = jnp.concatenate([gates_t, pad], axis=0).T


def _out_router(parts, x, g1, sh2, sc2, row_of_batch, pw, tm):
    b, n, d = x.shape
    full = lambda a: pl.BlockSpec(a.shape, lambda bi, i: (0,) * a.ndim)
    tok = lambda w: pl.BlockSpec((1, tm, w), lambda bi, i: (bi, i, 0))
    mod_spec = pl.BlockSpec((1, 1, d), lambda bi, i: (row_of_batch(bi), 0, 0))
    consts = [pw["gffn"], pw["w_out"], pw["rw_hi"], pw["rw_lo"], pw["rb"]]
    return pl.pallas_call(
        _out_kernel,
        grid=(b, n // tm),
        in_specs=[tok(SEG_W)] * 4 + [tok(d), mod_spec, mod_spec, mod_spec] + [full(a) for a in consts],
        out_specs=[tok(d), tok(d), tok(LANES)],
        out_shape=[jax.ShapeDtypeStruct((b, n, d), F32), jax.ShapeDtypeStruct((b, n, d), BF16),
                   jax.ShapeDtypeStruct((b, n, LANES), F32)],
        compiler_params=_cparams(("parallel", "parallel")),
        name="out_router",
    )(*parts, x, g1, sh2, sc2, *consts)


MOE_EXPERTS_PER_STEP = 4


def _swiglu_act(gu, scale=None):
    de = gu.shape[1] // 2
    g, u = gu[:, :de], gu[:, de:]
    act = (g * _sigmoid(g)) * u
    return (act if scale is None else act * scale).astype(BF16)


def _moe_kernel(h_ref, gate_ref, x_ref, g2_ref, wgu_ref, wd_ref, sgu_ref, sd_ref, fin_ref,
                o_ref, acc_ref, *, final_norm):
    step = pl.program_id(1)
    k = wgu_ref.shape[0]
    hb = h_ref[...]
    gates = gate_ref[...]
    lane = _lane_iota(gates.shape)
    acts = []
    for i in range(k):
        gu = jnp.dot(hb, wgu_ref[i], preferred_element_type=F32)
        col = jnp.sum(jnp.where(lane == step * k + i, gates, 0.0), axis=1, keepdims=True)
        acts.append(_swiglu_act(gu, col))
    wd = wd_ref[...]
    contrib = jnp.dot(jnp.concatenate(acts, axis=1), wd.reshape(k * wd.shape[1], wd.shape[2]),
                      preferred_element_type=F32)

    @pl.when(step == 0)
    def _():
        shared = _swiglu_act(jnp.dot(hb, sgu_ref[...], preferred_element_type=F32))
        acc_ref[...] = contrib + jnp.dot(shared, sd_ref[...], preferred_element_type=F32)

    @pl.when(step > 0)
    def _():
        acc_ref[...] += contrib

    @pl.when(step == pl.num_programs(1) - 1)
    def _():
        y = x_ref[...] + g2_ref[0] * acc_ref[...]
        if final_norm:
            ms = jnp.mean(y * y, axis=-1, keepdims=True)
            y = y * lax.rsqrt(ms + EPS) * fin_ref[...]
        o_ref[...] = y


def _moe(h2, gates, x, g2, row_of_tile, pw, tm, final_norm):
    t, d = x.shape
    ne, _, de2 = pw["wgu"].shape
    k = MOE_EXPERTS_PER_STEP
    full = lambda a: pl.BlockSpec(a.shape, lambda i, e: (0,) * a.ndim)
    return pl.pallas_call(
        functools.partial(_moe_kernel, final_norm=final_norm),
        grid=(t // tm, ne // k),
        in_specs=[
            pl.BlockSpec((tm, d), lambda i, e: (i, 0)),
            pl.BlockSpec((tm, LANES), lambda i, e: (i, 0)),
            pl.BlockSpec((tm, d), lambda i, e: (i, 0)),
            pl.BlockSpec((1, 1, d), lambda i, e: (row_of_tile(i), 0, 0)),
            pl.BlockSpec((k, d, de2), lambda i, e: (e, 0, 0)),
            pl.BlockSpec((k, de2 // 2, d), lambda i, e: (e, 0, 0)),
            full(pw["sgu"]), full(pw["sd"]), full(pw["fin"]),
        ],
        out_specs=pl.BlockSpec((tm, d), lambda i, e: (i, 0)),
        out_shape=jax.ShapeDtypeStruct((t, d), F32),
        scratch_shapes=[pltpu.VMEM((tm, d), F32)],
        compiler_params=_cparams(("parallel", "arbitrary")),
        name="moe_final" if final_norm else "moe",
    )(h2, gates, x, g2, pw["wgu"], pw["wd"], pw["sgu"], pw["sd"], pw["fin"])


def _axial_angles(n_tok, rot_dim):
    half = rot_dim // 2
    freqs = ROPE_BASE ** (-jnp.arange(0, half, 2, dtype=F32) / half)
    t = jnp.arange(n_tok, dtype=jnp.int32)
    row = (t // GRID_W).astype(F32)
    col = (t % GRID_W).astype(F32)
    return row[:, None] * freqs, col[:, None] * freqs


def _rope_table(n_tok, rot_dim):
    ar, ac = _axial_angles(n_tok, rot_dim)
    cos = jnp.concatenate([jnp.cos(ar)] * 2 + [jnp.cos(ac)] * 2, axis=1)
    sin = jnp.concatenate([-jnp.sin(ar), jnp.sin(ar), -jnp.sin(ac), jnp.sin(ac)], axis=1)
    return cos, sin


def _rope_tables(n_tok):
    cg, sg = _rope_table(n_tok, GQA_DIM)
    cd, sd = _rope_table(n_tok, DIFF_QK_DIM)
    cm32, sm32 = _rope_table(n_tok, MLA_ROPE)
    ones = jnp.ones((n_tok, MLA_NOPE), F32)
    zeros = jnp.zeros((n_tok, MLA_NOPE), F32)
    pad = LANES - MLA_NOPE - MLA_ROPE
    cm = jnp.concatenate([ones, cm32, ones[:, :pad]], axis=1)
    sm = jnp.concatenate([zeros, sm32, zeros[:, :pad]], axis=1)
    return (jnp.tile(cg, (1, LANES // GQA_DIM)), jnp.tile(sg, (1, LANES // GQA_DIM)),
            jnp.tile(cd, (1, LANES // DIFF_QK_DIM)), jnp.tile(sd, (1, LANES // DIFF_QK_DIM)),
            cm, sm)


def _pack_layer(l, p):
    d = p["w_in"].shape[1]
    w = p["w_in"][l]
    o = np.cumsum([0, 256, 128, 128, 256, 256, 256, 256, 256, 256, MLA_Q_LORA, MLA_KV_LORA, MLA_ROPE])
    col = lambda i: w[:, int(o[i]):int(o[i + 1])]
    dup = lambda a: jnp.concatenate([a[:, :GQA_DIM], a[:, :GQA_DIM], a[:, GQA_DIM:], a[:, GQA_DIM:]], axis=1)
    z = lambda n: jnp.zeros((d, n), F32)
    w_in = jnp.concatenate(
        [col(0), dup(col(1)), dup(col(2)), col(3), col(4), col(5), col(6), col(7), col(8),
         col(9), z(SEG_W - MLA_Q_LORA), col(10),
         z(MLA_NOPE), col(11), z(LANES - MLA_NOPE - MLA_ROPE)], axis=1).astype(BF16)
    dq = MLA_NOPE + MLA_ROPE
    wqu = p["mla_w_q_up"][l].reshape(MLA_Q_LORA, MLA_HEADS, dq)
    wqu = jnp.pad(wqu, ((0, SEG_W - MLA_Q_LORA), (0, 0), (0, LANES - dq)))
    wkv = p["mla_w_kv_up"][l].reshape(MLA_KV_LORA, MLA_HEADS, MLA_NOPE + MLA_V)
    wkk = jnp.pad(wkv[:, :, :MLA_NOPE], ((0, 0), (0, 0), (0, LANES - MLA_NOPE)))
    e = p["moe_w_gate"].shape[1]
    rw = jnp.pad(p["router_w"], ((0, 0), (0, LANES - e)))
    rw_hi = rw.astype(BF16)
    rw_lo = (rw - rw_hi.astype(F32)).astype(BF16)
    return {
        "gmix": p["norm_mix"][l][None], "gffn": p["norm_ffn"][l][None],
        "w_in": w_in,
        "wq_up": wqu.reshape(SEG_W, MLA_HEADS * LANES).astype(BF16),
        "wkv_k": wkk.reshape(MLA_KV_LORA, MLA_HEADS * LANES).astype(BF16),
        "wkv_v": wkv[:, :, MLA_NOPE:].reshape(MLA_KV_LORA, MLA_HEADS * MLA_V).astype(BF16),
        "gq": (jnp.tile(p["gqa_q_norm"][l], GQA_HEADS) * (GQA_DIM ** -0.5 * LOG2E))[None],
        "gk": jnp.tile(p["gqa_k_norm"][l], GQA_HEADS)[None],
        "gcq": jnp.pad(p["mla_q_norm"][l], (0, SEG_W - MLA_Q_LORA))[None],
        "gckv": p["mla_kv_norm"][l][None],
        "w_out": p["w_out"][l].astype(BF16),
        "rw_hi": rw_hi, "rw_lo": rw_lo,
        "rb": jnp.broadcast_to(p["router_b"][:, None], (e, LANES)),
        "wgu": jnp.concatenate([p["moe_w_gate"][l], p["moe_w_up"][l]], axis=2).astype(BF16),
        "wd": p["moe_w_down"][l].astype(BF16),
        "sgu": jnp.concatenate([p["shared_w_gate"][l], p["shared_w_up"][l]], axis=1).astype(BF16),
        "sd": p["shared_w_down"][l].astype(BF16),
        "fin": p["final_norm"][None],
        "diff": (p["diff_lq1"][l][None], p["diff_lk1"][l][None], p["diff_lq2"][l][None],
                 p["diff_lk2"][l][None], p["diff_subln"][l][:, None]),
    }


def kernel(x, c, ctx, c_ctx, w_mod, b_mod, norm_mix, norm_ffn, w_in, w_out, gqa_q_norm, gqa_k_norm,
           na_rel_bias, diff_lq1, diff_lk1, diff_lq2, diff_lk2, diff_subln, mla_q_norm, mla_w_q_up,
           mla_kv_norm, mla_w_kv_up, router_w, router_b, moe_w_gate, moe_w_up, moe_w_down,
           shared_w_gate, shared_w_up, shared_w_down, final_norm):
    p = dict(w_in=w_in, norm_mix=norm_mix, norm_ffn=norm_ffn, w_out=w_out, gqa_q_norm=gqa_q_norm,
             gqa_k_norm=gqa_k_norm, diff_lq1=diff_lq1, diff_lk1=diff_lk1, diff_lq2=diff_lq2,
             diff_lk2=diff_lk2, diff_subln=diff_subln, mla_q_norm=mla_q_norm, mla_w_q_up=mla_w_q_up,
             mla_kv_norm=mla_kv_norm, mla_w_kv_up=mla_w_kv_up, router_w=router_w, router_b=router_b,
             moe_w_gate=moe_w_gate, moe_w_up=moe_w_up, moe_w_down=moe_w_down,
             shared_w_gate=shared_w_gate, shared_w_up=shared_w_up, shared_w_down=shared_w_down,
             final_norm=final_norm)
    b, s, d = x.shape
    nctx = ctx.shape[1]
    depth = w_mod.shape[0]
    rows = s // GRID_W
    assert s % TOK_TILE == 0 and s // NA_TILE >= NA_CHUNKS + NA_QCHUNKS
    assert nctx % LANES == 0 and (b * s) % MOE_TILE == 0 and s % MOE_TILE == 0

    mrows = -(-(b + 1) // SUBLANES) * SUBLANES
    cc = jnp.zeros((mrows, d), F32).at[:b].set(c).at[b].set(c_ctx)
    mod = _modulation(cc, w_mod, b_mod)
    tables = _rope_tables(s)
    lat_row = lambda bi: bi
    ctx_row = lambda bi: b
    tm_ctx = b * nctx if b * nctx <= MOE_TILE else nctx

    xc = ctx
    for l in range(depth):
        want_ctx = l < depth - 1
        lam_init = 0.8 - 0.6 * math.exp(-0.3 * l)
        pw = _pack_layer(l, p)
        m6 = mod[l].reshape(mrows, 6, 1, d)
        sh1, sc1, g1, sh2, sc2, g2 = (m6[:, k] for k in range(6))

        lat = _project(x, sh1, sc1, lat_row, pw, tables, TOK_TILE, LAT_OUTS)
        cx = _project(xc, sh1, sc1, ctx_row, pw, None, nctx, CTX_OUTS if want_ctx else CTX_KV_OUTS)

        def dense(g, kind, extras=()):
            return _flash_lat(lat[g + ".qT"], cx[g + ".k"], cx[g + ".vT"], lat[g + ".k"], lat[g + ".vT"],
                              kind, extras, lam_init)

        def ctx_only(g, kind, extras=()):
            return _flash_ctx(cx[g + ".qT"], cx[g + ".k"], cx[g + ".vT"], kind, extras, lam_init)

        a_lat = dense("A", "pair")
        bias = _na_bias_tiles(na_rel_bias[l], rows, NA_TILE)
        b_lat = _na_attention(lat["B.qT"], cx["B.k"], cx["B.vT"], lat["B.k"], lat["B.vT"], bias)
        c_lat = dense("C", "diff", pw["diff"])
        d_lat = dense("D", "mla")
        x1, h2, gates = _out_router((a_lat, b_lat, c_lat, d_lat), x, g1, sh2, sc2, lat_row, pw, TOK_TILE)
        tiles_per_batch = s // MOE_TILE
        x = _moe(h2.reshape(b * s, d), gates.reshape(b * s, LANES), x1.reshape(b * s, d), g2,
                 lambda i: i // tiles_per_batch, pw, MOE_TILE, not want_ctx).reshape(b, s, d)

        if want_ctx:
            a_c = ctx_only("A", "pair")
            b_c = ctx_only("B", "pair")
            c_c = ctx_only("C", "diff", pw["diff"])
            d_c = ctx_only("D", "mla")
            xc1, h2c, gates_c = _out_router((a_c, b_c, c_c, d_c), xc, g1, sh2, sc2, ctx_row, pw, nctx)
            xc = _moe(h2c.reshape(b * nctx, d), gates_c.reshape(b * nctx, LANES),
                      xc1.reshape(b * nctx, d), g2, lambda i: b, pw, tm_ctx, False).reshape(b, nctx, d)
    return x
```

```python
import functools
import math

import numpy as np
import jax
import jax.numpy as jnp
from jax import lax
from jax.experimental import pallas as pl
from jax.experimental.pallas import tpu as pltpu

F32 = jnp.float32
BF16 = jnp.bfloat16

GRID_W = 64
EPS = 1e-6
ROPE_BASE = 10000.0
GQA_HEADS, GQA_KV_HEADS, GQA_DIM = 4, 2, 64
NA_HEADS, NA_DIM, NA_WIN_ROWS, NA_WIN_COLS = 4, 64, 8, 16
DIFF_HEADS, DIFF_QK_DIM, DIFF_V_DIM = 4, 32, 64
MLA_HEADS, MLA_NOPE, MLA_ROPE, MLA_V, MLA_Q_LORA, MLA_KV_LORA = 4, 64, 32, 64, 192, 128
N_EXPERTS, MOE_GROUPS, D_EXPERT = 16, 4, 256
EXPERTS_PER_GROUP = N_EXPERTS // MOE_GROUPS

LANES = 128
SUBLANES = 8
VMEM_LIMIT = 56 * 1024 * 1024

TOK_TILE = 512
NA_TILE = 256
MOE_TILE = 1024
MOD_TILE = 1024
PROJ_SUBTILES = 2
OUT_SUBTILES = 2
NEG = -1e30
LOG2E = math.log2(math.e)

SEG_W = 256
OFF_AQ, OFF_AK, OFF_AV = 0, 256, 512
OFF_BQ, OFF_BK, OFF_BV = 768, 1024, 1280
OFF_CQ, OFF_CK, OFF_CV = 1536, 1792, 2048
OFF_DQ, OFF_DKV, OFF_DPE = 2304, 2560, 2688
IN_PACKED = 2816


def _cparams(sem):
    return pltpu.CompilerParams(dimension_semantics=sem, vmem_limit_bytes=VMEM_LIMIT)


def _lane_iota(shape):
    return lax.broadcasted_iota(jnp.int32, shape, 1)


def _lane_group(shape, width):
    return lax.shift_right_logical(_lane_iota(shape), int(math.log2(width)))


def _sigmoid(x):
    return 1.0 / (1.0 + jnp.exp(-x))


def _mod_kernel(c_ref, w_ref, b_ref, o_ref):
    c = c_ref[...]
    s = c * _sigmoid(c)
    o_ref[0] = jnp.dot(s, w_ref[0], preferred_element_type=F32,
                       precision=lax.Precision.HIGHEST) + b_ref[0]


def _modulation(cc, w_mod, b_mod):
    depth, d, n = w_mod.shape
    rows = cc.shape[0]
    return pl.pallas_call(
        _mod_kernel,
        grid=(depth, n // MOD_TILE),
        in_specs=[
            pl.BlockSpec((rows, d), lambda l, j: (0, 0)),
            pl.BlockSpec((1, d, MOD_TILE), lambda l, j: (l, 0, j)),
            pl.BlockSpec((1, 1, MOD_TILE), lambda l, j: (l, 0, j)),
        ],
        out_specs=pl.BlockSpec((1, rows, MOD_TILE), lambda l, j: (l, 0, j)),
        out_shape=jax.ShapeDtypeStruct((depth, rows, n), F32),
        compiler_params=_cparams(("parallel", "parallel")),
        name="modulation",
    )(cc, w_mod, b_mod.reshape(depth, 1, n))


def _swap_halves(x, half):
    w = x.shape[1]
    lane = _lane_iota(x.shape)
    fwd = pltpu.roll(x, half, 1)
    bwd = pltpu.roll(x, w - half, 1)
    return jnp.where((lane & (2 * half - 1)) < half, bwd, fwd)


def _rope(x, cos, sin, half):
    reps = x.shape[1] // cos.shape[1]
    c = jnp.concatenate([cos] * reps, axis=1) if reps > 1 else cos
    s = jnp.concatenate([sin] * reps, axis=1) if reps > 1 else sin
    return x * c + _swap_halves(x, half) * s


def _group_rms(x, group, count):
    gid = _lane_group(x.shape, group)
    x2 = x * x
    inv = jnp.zeros_like(x)
    for g in range(x.shape[1] // group):
        msk = gid == g
        ms = jnp.sum(jnp.where(msk, x2, 0.0), axis=1, keepdims=True) * (1.0 / count)
        inv = jnp.where(msk, lax.rsqrt(ms + EPS), inv)
    return x * inv


LAT_OUTS = ("A.qT", "A.k", "A.vT", "B.qT", "B.k", "B.vT", "C.qT", "C.k", "C.vT", "D.qT", "D.k", "D.vT")
CTX_KV_OUTS = ("A.k", "A.vT", "B.k", "B.vT", "C.k", "C.vT", "D.k", "D.vT")
CTX_OUTS = CTX_KV_OUTS + ("A.qT", "B.qT", "C.qT", "D.qT")
GROUP_W = {"A": SEG_W, "B": SEG_W, "C": SEG_W, "D": 2 * SEG_W}


def _proj_kernel(*refs, rope, outs):
    (x_ref, sh_ref, sc_ref, gmix_ref, w_ref, wqu_ref, wkk_ref, wkv_ref,
     gq_ref, gk_ref, gcq_ref, gckv_ref) = refs[:12]
    pos = 12
    tabs = refs[pos:pos + 6] if rope else ()
    pos += len(tabs)
    out = dict(zip(outs, refs[pos:pos + len(outs)]))
    tm = x_ref.shape[1]
    parts = PROJ_SUBTILES if tm % (PROJ_SUBTILES * 2 * LANES) == 0 else 1
    sub = tm // parts
    for h in range(parts):
        _proj_subtile(slice(h * sub, (h + 1) * sub), x_ref, sh_ref, sc_ref, gmix_ref, w_ref, wqu_ref,
                      wkk_ref, wkv_ref, gq_ref, gk_ref, gcq_ref, gckv_ref, tabs, out, rope)


def _proj_subtile(rows, x_ref, sh_ref, sc_ref, gmix_ref, w_ref, wqu_ref, wkk_ref, wkv_ref,
                  gq_ref, gk_ref, gcq_ref, gckv_ref, tabs, out, rope):
    r0, sub = rows.start, rows.stop - rows.start
    if rope:
        cg, sg, cd, sd, cm, sm = (t[rows, :] for t in tabs)

    def want(g, t):
        return (g + "." + t) in out or (g + "." + t + "T") in out

    def put(g, t, val):
        if g + "." + t in out:
            out[g + "." + t][0, rows, :] = val.astype(BF16)
        if g + "." + t + "T" in out:
            ref = out[g + "." + t + "T"]
            vt = val.T.astype(BF16)
            cw = ref.shape[3]
            if cw >= sub:
                ref[0, r0 // cw, :, r0 % cw:r0 % cw + sub] = vt
            else:
                for c in range(sub // cw):
                    ref[0, r0 // cw + c] = vt[:, c * cw:(c + 1) * cw]

    x = x_ref[0, rows, :]
    ms = jnp.mean(x * x, axis=-1, keepdims=True)
    h = x * lax.rsqrt(ms + EPS) * gmix_ref[...]
    h = h * (1.0 + sc_ref[0]) + sh_ref[0]
    hb = h.astype(BF16)

    def seg(off, width=SEG_W):
        return jnp.dot(hb, w_ref[:, off:off + width], preferred_element_type=F32)

    if want("A", "q"):
        q = _group_rms(seg(OFF_AQ), GQA_DIM, GQA_DIM) * gq_ref[...]
        if rope:
            q = _rope(q, cg, sg, GQA_DIM // 4)
        put("A", "q", q)
    k = _group_rms(seg(OFF_AK), GQA_DIM, GQA_DIM) * gk_ref[...]
    if rope:
        k = _rope(k, cg, sg, GQA_DIM // 4)
    put("A", "k", k)
    put("A", "v", seg(OFF_AV))

    if want("B", "q"):
        put("B", "q", seg(OFF_BQ) * (NA_DIM ** -0.5 * LOG2E))
    put("B", "k", seg(OFF_BK))
    put("B", "v", seg(OFF_BV))

    if want("C", "q"):
        q = seg(OFF_CQ) * (DIFF_QK_DIM ** -0.5 * LOG2E)
        if rope:
            q = _rope(q, cd, sd, DIFF_QK_DIM // 4)
        put("C", "q", q)
    k = seg(OFF_CK)
    if rope:
        k = _rope(k, cd, sd, DIFF_QK_DIM // 4)
    put("C", "k", k)
    put("C", "v", seg(OFF_CV))

    if want("D", "q"):
        cq = _group_rms(seg(OFF_DQ), SEG_W, MLA_Q_LORA) * gcq_ref[...]
        q = jnp.dot(cq.astype(BF16), wqu_ref[...], preferred_element_type=F32)
        q = q * ((MLA_NOPE + MLA_ROPE) ** -0.5 * LOG2E)
        if rope:
            q = _rope(q, cm, sm, MLA_ROPE // 4)
        put("D", "q", q)
    ckv = _group_rms(seg(OFF_DKV, LANES), LANES, MLA_KV_LORA) * gckv_ref[...]
    ckvb = ckv.astype(BF16)
    k = jnp.dot(ckvb, wkk_ref[...], preferred_element_type=F32)
    kpe = seg(OFF_DPE, LANES)
    if rope:
        kpe = _rope(kpe, cm, sm, MLA_ROPE // 4)
    put("D", "k", k + jnp.concatenate([kpe] * MLA_HEADS, axis=1))
    put("D", "v", jnp.dot(ckvb, wkv_ref[...], preferred_element_type=F32))


def _project(x, sh, sc, row_of_batch, pw, tables, tm, outs):
    b, n, d = x.shape
    nt = n // tm
    rope = tables is not None
    full = lambda a: pl.BlockSpec(a.shape, lambda bi, i: (0,) * a.ndim)
    mod_spec = pl.BlockSpec((1, 1, d), lambda bi, i: (row_of_batch(bi), 0, 0))
    ins = [x, sh, sc, pw["gmix"], pw["w_in"], pw["wq_up"], pw["wkv_k"], pw["wkv_v"],
           pw["gq"], pw["gk"], pw["gcq"], pw["gckv"]]
    in_specs = [pl.BlockSpec((1, tm, d), lambda bi, i: (bi, i, 0)), mod_spec, mod_spec]
    in_specs += [full(a) for a in ins[3:]]
    if rope:
        ins += list(tables)
        in_specs += [pl.BlockSpec((tm, LANES), lambda bi, i: (i, 0)) for _ in tables]
    shapes, specs = [], []
    for name in outs:
        g, t = name.split(".")
        w = SEG_W if t[0] == "v" else GROUP_W[g]
        if t.endswith("T"):
            cw = min(tm, NA_TILE) if g == "B" else tm
            shapes.append(jax.ShapeDtypeStruct((b, n // cw, w, cw), BF16))
            specs.append(pl.BlockSpec((1, tm // cw, w, cw), lambda bi, i: (bi, i, 0, 0)))
        else:
            shapes.append(jax.ShapeDtypeStruct((b, n, w), BF16))
            specs.append(pl.BlockSpec((1, tm, w), lambda bi, i: (bi, i, 0)))
    res = pl.pallas_call(
        functools.partial(_proj_kernel, rope=rope, outs=tuple(outs)),
        grid=(b, nt),
        in_specs=in_specs,
        out_specs=specs,
        out_shape=shapes,
        compiler_params=_cparams(("parallel", "parallel")),
        name="proj_rope" if rope else "proj_ctx",
    )(*ins)
    return dict(zip(outs, res))


HEAD_V = 64


def _tile_lanes(x, width):
    reps = width // x.shape[1]
    return jnp.tile(x, (1, reps)) if reps > 1 else x


def with_ones(vt):
    rows = lax.broadcasted_iota(jnp.int32, (LANES - HEAD_V, vt.shape[1]), 0)
    ones = jnp.where(rows == 0, 1.0, 0.0).astype(BF16)
    return jnp.concatenate([vt, ones], axis=0)


def _map_slices(kind):
    def k_lanes(m):
        return slice(m * LANES, (m + 1) * LANES) if kind == "mla" else slice(None)

    def v_rows(m):
        hv = m // 2 if kind == "diff" else m
        return slice(hv * HEAD_V, (hv + 1) * HEAD_V)

    return k_lanes, v_rows


def _query_slabs(qt_ref, kind, nm):
    qv = jnp.concatenate([qt_ref[0, c] for c in range(qt_ref.shape[1])], axis=1)
    if kind == "mla":
        return [qv[m * LANES:(m + 1) * LANES, :] for m in range(nm)]
    rows = lax.broadcasted_iota(jnp.int32, qv.shape, 0)
    grp = lax.shift_right_logical(rows, int(math.log2(LANES // nm)))
    return [jnp.where(grp == m, qv, jnp.zeros_like(qv)) for m in range(nm)]


def _finish_heads(accs, kind, lam_init, diff_refs):
    outs = [a[0:HEAD_V] / a[HEAD_V:HEAD_V + 1] for a in accs]
    if kind == "diff":
        lq1_ref, lk1_ref, lq2_ref, lk2_ref, sub_ref = diff_refs
        lam = (jnp.exp(jnp.sum(lq1_ref[...] * lk1_ref[...], axis=1, keepdims=True))
               - jnp.exp(jnp.sum(lq2_ref[...] * lk2_ref[...], axis=1, keepdims=True))
               + lam_init)
        heads = []
        for hd in range(2):
            dlt = outs[2 * hd] - lam * outs[2 * hd + 1]
            ms2 = jnp.mean(dlt * dlt, axis=0, keepdims=True)
            heads.append(dlt * lax.rsqrt(ms2 + EPS) * sub_ref[...] * (1.0 - lam_init))
        outs = heads
    return jnp.concatenate(outs, axis=0).T


def _flash_ctx_kernel(*refs, kind, lam_init):
    qt_ref, kc_ref, vct_ref = refs[:3]
    nd = 5 if kind == "diff" else 0
    diff_refs, o_ref = refs[3:3 + nd], refs[3 + nd]
    nm = 4 if kind == "diff" else 2
    k_lanes, v_rows = _map_slices(kind)
    accs = []
    for m, qm in enumerate(_query_slabs(qt_ref, kind, nm)):
        st = jnp.dot(kc_ref[0, :, k_lanes(m)], qm, preferred_element_type=F32)
        p = jnp.exp2(st - jnp.max(st, axis=0, keepdims=True)).astype(BF16)
        accs.append(jnp.dot(with_ones(vct_ref[0, 0, v_rows(m), :]), p, preferred_element_type=F32))
    o_ref[0] = _finish_heads(accs, kind, lam_init, diff_refs).astype(o_ref.dtype)


def _flash_lat_kernel(*refs, kind, lam_init):
    qt_ref, kc_ref, vct_ref, k_ref, vt_ref = refs[:5]
    nd = 5 if kind == "diff" else 0
    diff_refs, o_ref = refs[5:5 + nd], refs[5 + nd]
    qs_e, acc_e, s_e, qs_l, acc_l, s_l, m_ref, mc_ref = refs[6 + nd:14 + nd]
    early = (qs_e, acc_e, s_e, 0, False)
    late = (qs_l, acc_l, s_l, vt_ref.shape[1] // 2, True)

    nm = qs_e.shape[0]
    nk, tk = vt_ref.shape[1], vt_ref.shape[3]
    half = nk // 2
    iters = half // 2
    g = pl.program_id(2)
    nq = pl.num_programs(2) - 1
    k_lanes, v_rows = _map_slices(kind)

    def produce(tile, slot, j, m):
        qs_ref, _, s_ref = tile[:3]
        rows = pl.ds(pl.multiple_of(j * tk, tk), tk)
        st = jnp.dot(k_ref[0, rows, k_lanes(m)], qs_ref[m], preferred_element_type=F32)
        s_ref[slot, m] = st
        return jnp.max(st, axis=0, keepdims=True)

    def consume(tile, slot, j, m, m_prev, m_chunk):
        _, acc_ref, s_ref = tile[:3]
        m_new = jnp.maximum(m_prev, m_chunk)
        alpha = jnp.exp2(m_prev - m_new)
        p = jnp.exp2(s_ref[slot, m] - m_new).astype(BF16)
        acc_ref[m] = alpha * acc_ref[m] + jnp.dot(
            with_ones(vt_ref[0, j, v_rows(m), :]), p, preferred_element_type=F32)
        return m_new

    def iteration(i, carries, tiles, last):
        maps = range(nm)
        j0s = [tile[3] + 2 * i for tile in tiles]
        c1 = [[produce(tile, 1, j0 + 1, m) for m in maps] for tile, j0 in zip(tiles, j0s)]
        m1 = [[consume(tile, 0, j0, m, c[0][m], c[1][m]) for m in maps]
              for tile, j0, c in zip(tiles, j0s, carries)]
        c2 = [c1[t] if (last and tile[4]) else [produce(tile, 0, j0 + 2, m) for m in maps]
              for t, (tile, j0) in enumerate(zip(tiles, j0s))]
        m2 = [[consume(tile, 1, j0 + 1, m, m1[t][m], c1[t][m]) for m in maps]
              for t, (tile, j0) in enumerate(zip(tiles, j0s))]
        return tuple((tuple(a), tuple(b)) for a, b in zip(m2, c2))

    def run(carries, tiles):
        carries = iteration(0, carries, tiles, iters == 1)
        if iters > 2:
            carries = lax.fori_loop(1, iters - 1, lambda i, c: iteration(i, c, tiles, False), carries)
        if iters > 1:
            carries = iteration(iters - 1, carries, tiles, True)
        return carries

    def start_early():
        for m, qm in enumerate(_query_slabs(qt_ref, kind, nm)):
            qs_e[m] = qm
        sts = [jnp.dot(kc_ref[0, :, k_lanes(m)], qs_e[m], preferred_element_type=F32)
               for m in range(nm)]
        ms = tuple(jnp.max(st, axis=0, keepdims=True) for st in sts)
        first = tuple(produce(early, 0, 0, m) for m in range(nm))

        def init_from_context():
            for m in range(nm):
                p = jnp.exp2(sts[m] - ms[m]).astype(BF16)
                acc_e[m] = jnp.dot(with_ones(vct_ref[0, 0, v_rows(m), :]), p,
                                   preferred_element_type=F32)

        return (ms, first), init_from_context

    def load_late():
        return (tuple(m_ref[m] for m in range(nm)), tuple(mc_ref[m] for m in range(nm)))

    def hand_over(carry):
        qs_l[...] = qs_e[...]
        acc_l[...] = acc_e[...]
        s_l[0] = s_e[0]
        for m in range(nm):
            m_ref[m] = carry[0][m]
            mc_ref[m] = carry[1][m]

    def finish_late():
        accs = [acc_l[m] for m in range(nm)]
        o_ref[0] = _finish_heads(accs, kind, lam_init, diff_refs).astype(o_ref.dtype)

    @pl.when(g == 0)
    def _():
        carry, init_from_context = start_early()
        init_from_context()
        hand_over(run((carry,), (early,))[0])

    @pl.when((g > 0) & (g < nq))
    def _():
        carry_l = load_late()
        carry_e, init_from_context = start_early()
        init_from_context()
        carry_l, carry_e = run((carry_l, carry_e), (late, early))
        finish_late()
        hand_over(carry_e)

    @pl.when(g == nq)
    def _():
        run((load_late(),), (late,))
        finish_late()


def _flash_specs(qt, kc, vct, kind, extras):
    b, nqc, wtot, cw = qt.shape
    wq = 2 * LANES if kind == "mla" else LANES
    c = kc.shape[1]
    extra_specs = [pl.BlockSpec(e.shape, lambda bi, p, i: (0, 0)) for e in extras]
    kv_specs = [pl.BlockSpec((1, c, wq), lambda bi, p, i: (bi, 0, p)),
                pl.BlockSpec((1, 1, LANES, c), lambda bi, p, i: (bi, 0, p, 0))]
    return b, nqc, wtot // wq, wq, cw, kv_specs, extra_specs


def _flash_ctx(qt, kc, vct, kind, extras=(), lam_init=0.0):
    b, nqc, npair, wq, cw, kv_specs, extra_specs = _flash_specs(qt, kc, vct, kind, extras)
    return pl.pallas_call(
        functools.partial(_flash_ctx_kernel, kind=kind, lam_init=lam_init),
        grid=(b, npair, nqc),
        in_specs=[pl.BlockSpec((1, 1, wq, cw), lambda bi, p, i: (bi, i, p, 0))] + kv_specs + extra_specs,
        out_specs=pl.BlockSpec((1, cw, LANES), lambda bi, p, i: (bi, i, p)),
        out_shape=jax.ShapeDtypeStruct((b, nqc * cw, npair * LANES), BF16),
        compiler_params=_cparams(("parallel", "parallel", "parallel")),
        name="flash_" + kind + "_ctx",
    )(qt, kc, vct, *extras)


def _flash_lat(qt, kc, vct, k, vt, kind, extras=(), lam_init=0.0):
    b, nq, npair, wq, tq, kv_specs, extra_specs = _flash_specs(qt, kc, vct, kind, extras)
    nm = 4 if kind == "diff" else 2
    nk, tk = vt.shape[1], vt.shape[3]
    assert nk % 4 == 0, "two phases of key chunks, processed in pairs"
    return pl.pallas_call(
        functools.partial(_flash_lat_kernel, kind=kind, lam_init=lam_init),
        grid=(b, npair, nq + 1),
        in_specs=[pl.BlockSpec((1, 1, wq, tq), lambda bi, p, g: (bi, jnp.minimum(g, nq - 1), p, 0))]
        + kv_specs
        + [pl.BlockSpec((1, k.shape[1], wq), lambda bi, p, g: (bi, 0, p)),
           pl.BlockSpec((1, nk, LANES, tk), lambda bi, p, g: (bi, 0, p, 0))]
        + extra_specs,
        out_specs=pl.BlockSpec((1, tq, LANES), lambda bi, p, g: (bi, jnp.maximum(g - 1, 0), p)),
        out_shape=jax.ShapeDtypeStruct((b, nq * tq, npair * LANES), BF16),
        scratch_shapes=2 * [
            pltpu.VMEM((nm, LANES, tq), BF16),
            pltpu.VMEM((nm, LANES, tq), F32),
            pltpu.VMEM((2, nm, tk, tq), F32),
        ] + 2 * [pltpu.VMEM((nm, 1, tq), F32)],
        compiler_params=_cparams(("parallel", "parallel", "arbitrary")),
        name="flash_" + kind,
    )(qt, kc, vct, k, vt, *extras)


NA_QCHUNKS = 2
NA_CHUNKS = NA_QCHUNKS + 2


def _na_kernel(qt_ref, k_ref, vt_ref, kc_ref, vct_ref, bias_ref, o_ref):
    nkc, tk = vt_ref.shape[1], vt_ref.shape[3]
    i = pl.program_id(2)
    c0 = jnp.clip(i * NA_QCHUNKS - 1, 0, nkc - NA_CHUNKS)
    qv = jnp.concatenate([qt_ref[0, c] for c in range(NA_QCHUNKS)], axis=1)
    rows = lax.broadcasted_iota(jnp.int32, qv.shape, 0)
    scores = []
    for m in range(2):
        own = lax.shift_right_logical(rows, int(math.log2(LANES // 2))) == m
        qm = jnp.where(own, qv, jnp.zeros_like(qv))
        ss = []
        for j in range(NA_CHUNKS):
            kj = k_ref[0, pl.ds(pl.multiple_of((c0 + j) * tk, tk), tk), :]
            ss.append(jnp.dot(kj, qm, preferred_element_type=F32) + bias_ref[0, m, j])
        scores.append((ss, jnp.dot(kc_ref[0], qm, preferred_element_type=F32)))
    outs = []
    for m, (ss, sc) in enumerate(scores):
        v_rows = slice(m * HEAD_V, (m + 1) * HEAD_V)
        mx = jnp.max(sc, axis=0, keepdims=True)
        for st in ss:
            mx = jnp.maximum(mx, jnp.max(st, axis=0, keepdims=True))
        acc = jnp.dot(with_ones(vct_ref[0, 0, v_rows, :]), jnp.exp2(sc - mx).astype(BF16),
                      preferred_element_type=F32)
        for j, st in enumerate(ss):
            acc = acc + jnp.dot(with_ones(vt_ref[0, c0 + j, v_rows, :]),
                                jnp.exp2(st - mx).astype(BF16), preferred_element_type=F32)
        outs.append(acc[0:HEAD_V] / acc[HEAD_V:HEAD_V + 1])
    o_ref[0] = jnp.concatenate(outs, axis=0).T.astype(o_ref.dtype)


def _na_bias_tiles(rel_bias, rows, tk):
    qrows = NA_QCHUNKS * tk // GRID_W
    nblk = rows // qrows
    nk = rows * GRID_W // tk
    krows = NA_CHUNKS * tk // GRID_W
    kh = min(NA_WIN_ROWS, rows)
    kw = NA_WIN_COLS
    blocks = np.array([0, 1, nblk - 1])
    c0 = np.clip(blocks * NA_QCHUNKS - 1, 0, nk - NA_CHUNKS)
    r = blocks[:, None] * qrows + np.arange(qrows)[None, :]
    kr = c0[:, None] * (tk // GRID_W) + np.arange(krows)[None, :]
    r0 = np.clip(r - kh // 2, 0, rows - kh)
    dr = kr[:, None, :] - r[:, :, None]
    valid_r = (kr[:, None, :] >= r0[:, :, None]) & (kr[:, None, :] < r0[:, :, None] + kh)
    cols = np.arange(GRID_W)
    cstart = np.clip(cols - kw // 2, 0, GRID_W - kw)
    dc = cols[None, :] - cols[:, None]
    valid_c = (cols[None, :] >= cstart[:, None]) & (cols[None, :] < cstart[:, None] + kw)
    ndc = 2 * NA_WIN_COLS - 1
    onehot = ((dc[None] + NA_WIN_COLS - 1) == np.arange(ndc)[:, None, None]) & valid_c[None]
    dr_idx = np.clip(dr + NA_WIN_ROWS - 1, 0, 2 * NA_WIN_ROWS - 2)
    g1 = rel_bias[:, dr_idx, :]
    tiles = jnp.einsum("htabd,dcx->thbxac", g1, jnp.asarray(onehot, F32),
                       precision=lax.Precision.HIGHEST)
    valid = valid_r.transpose(0, 2, 1)[:, None, :, None, :, None] & valid_c.T[None, None, None, :, None, :]
    tiles = jnp.where(jnp.asarray(valid), tiles * LOG2E, NEG)
    return tiles.reshape(3, rel_bias.shape[0], NA_CHUNKS, tk, qrows * GRID_W)


def _na_attention(qt, kc, vct, k, vt, bias):
    b, nqc, wtot, cw = qt.shape
    nblk, tq = nqc // NA_QCHUNKS, NA_QCHUNKS * cw
    npair = wtot // LANES
    nkc, tk = vt.shape[1], vt.shape[3]
    c = kc.shape[1]
    n = k.shape[1]

    def bias_map(bi, p, i):
        return (jnp.where(i == 0, 0, jnp.where(i == nblk - 1, 2, 1)), p, 0, 0, 0)

    return pl.pallas_call(
        _na_kernel,
        grid=(b, npair, nblk),
        in_specs=[
            pl.BlockSpec((1, NA_QCHUNKS, LANES, cw), lambda bi, p, i: (bi, i, p, 0)),
            pl.BlockSpec((1, n, LANES), lambda bi, p, i: (bi, 0, p)),
            pl.BlockSpec((1, nkc, LANES, tk), lambda bi, p, i: (bi, 0, p, 0)),
            pl.BlockSpec((1, c, LANES), lambda bi, p, i: (bi, 0, p)),
            pl.BlockSpec((1, 1, LANES, c), lambda bi, p, i: (bi, 0, p, 0)),
            pl.BlockSpec((1, 2, NA_CHUNKS, tk, tq), bias_map),
        ],
        out_specs=pl.BlockSpec((1, tq, LANES), lambda bi, p, i: (bi, i, p)),
        out_shape=jax.ShapeDtypeStruct((b, nblk * tq, wtot), BF16),
        compiler_params=_cparams(("parallel", "parallel", "parallel")),
        name="na_attention",
    )(qt, k, vt, kc, vct, bias)


def _first_index_of_max(vals, row, big):
    mx = jnp.max(vals, axis=0, keepdims=True)
    idx = jnp.min(jnp.where(vals == mx, row, big), axis=0, keepdims=True)
    return mx, idx


def _router_gates(scores_t, sel_t):
    irow = lax.broadcasted_iota(jnp.int32, sel_t.shape, 0)
    row = irow.astype(F32)
    grp = lax.shift_right_logical(irow, int(math.log2(EXPERTS_PER_GROUP))).astype(F32)
    big = float(N_EXPERTS)
    best = None
    for g in range(MOE_GROUPS):
        vals = jnp.where(grp == float(g), sel_t, -jnp.inf)
        m1, i1 = _first_index_of_max(vals, row, big)
        m2 = jnp.max(jnp.where(row == i1, -jnp.inf, vals), axis=0, keepdims=True)
        gs = m1 + m2
        if best is None:
            best, bi = gs, jnp.zeros(gs.shape, F32)
        else:
            better = gs > best
            bi = jnp.where(better, float(g), bi)
            best = jnp.where(better, gs, best)
    msel = jnp.where(grp == bi, sel_t, -jnp.inf)
    _, i1 = _first_index_of_max(msel, row, big)
    msel2 = jnp.where(row == i1, -jnp.inf, msel)
    _, i2 = _first_index_of_max(msel2, row, big)
    w = jnp.where((row == i1) | (row == i2), scores_t, 0.0)
    return w / jnp.sum(w, axis=0, keepdims=True)


def _out_kernel(a_ref, b_ref, c_ref, d_ref, x_ref, g1_ref, sh_ref, sc_ref, gffn_ref,
                w_ref, rwh_ref, rwl_ref, rb_ref, xo_ref, h2_ref, gate_ref):
    tm = x_ref.shape[1]
    parts = OUT_SUBTILES if tm % (OUT_SUBTILES * LANES) == 0 else 1
    sub = tm // parts
    ys = []
    for h in range(parts):
        rows = slice(h * sub, (h + 1) * sub)
        y = None
        for gi, r in enumerate((a_ref, b_ref, c_ref, d_ref)):
            part = jnp.dot(r[0, rows, :], w_ref[gi * SEG_W:(gi + 1) * SEG_W, :],
                           preferred_element_type=F32)
            y = part if y is None else y + part
        ys.append(y)
    for h in range(parts):
        rows = slice(h * sub, (h + 1) * sub)
        xn = x_ref[0, rows, :] + g1_ref[0] * ys[h]
        xo_ref[0, rows, :] = xn
        ms = jnp.mean(xn * xn, axis=-1, keepdims=True)
        h2 = xn * lax.rsqrt(ms + EPS) * gffn_ref[...]
        h2 = h2 * (1.0 + sc_ref[0]) + sh_ref[0]
        hi = h2.astype(BF16)
        h2_ref[0, rows, :] = hi
        lo = (h2 - hi.astype(F32)).astype(BF16)
        logits = (jnp.dot(hi, rwh_ref[...], preferred_element_type=F32)
                  + jnp.dot(lo, rwh_ref[...], preferred_element_type=F32)
                  + jnp.dot(hi, rwl_ref[...], preferred_element_type=F32))
        scores_t = _sigmoid(logits).T[:N_EXPERTS]
        sel_t = scores_t + _tile_lanes(rb_ref[...], sub)
        gates_t = _router_gates(scores_t, sel_t)
        pad = jnp.zeros((LANES - N_EXPERTS, sub), F32)
        gate_ref[0, rows, :] = jnp.concatenate([gates_t, pad], axis=0).T


def _out_router(parts, x, g1, sh2, sc2, row_of_batch, pw, tm):
    b, n, d = x.shape
    full = lambda a: pl.BlockSpec(a.shape, lambda bi, i: (0,) * a.ndim)
    tok = lambda w: pl.BlockSpec((1, tm, w), lambda bi, i: (bi, i, 0))
    mod_spec = pl.BlockSpec((1, 1, d), lambda bi, i: (row_of_batch(bi), 0, 0))
    consts = [pw["gffn"], pw["w_out"], pw["rw_hi"], pw["rw_lo"], pw["rb"]]
    return pl.pallas_call(
        _out_kernel,
        grid=(b, n // tm),
        in_specs=[tok(SEG_W)] * 4 + [tok(d), mod_spec, mod_spec, mod_spec] + [full(a) for a in consts],
        out_specs=[tok(d), tok(d), tok(LANES)],
        out_shape=[jax.ShapeDtypeStruct((b, n, d), F32), jax.ShapeDtypeStruct((b, n, d), BF16),
                   jax.ShapeDtypeStruct((b, n, LANES), F32)],
        compiler_params=_cparams(("parallel", "parallel")),
        name="out_router",
    )(*parts, x, g1, sh2, sc2, *consts)


MOE_EXPERTS_PER_STEP = 4


def _swiglu_act(gu, scale=None):
    de = gu.shape[1] // 2
    g, u = gu[:, :de], gu[:, de:]
    act = (g * _sigmoid(g)) * u
    return (act if scale is None else act * scale).astype(BF16)


def _moe_kernel(h_ref, gate_ref, x_ref, g2_ref, wgu_ref, wd_ref, sgu_ref, sd_ref, fin_ref,
                o_ref, acc_ref, *, final_norm):
    step = pl.program_id(1)
    k = wgu_ref.shape[0]
    hb = h_ref[...]
    gates = gate_ref[...]
    lane = _lane_iota(gates.shape)
    acts = []
    for i in range(k):
        gu = jnp.dot(hb, wgu_ref[i], preferred_element_type=F32)
        col = jnp.sum(jnp.where(lane == step * k + i, gates, 0.0), axis=1, keepdims=True)
        acts.append(_swiglu_act(gu, col))
    wd = wd_ref[...]
    contrib = jnp.dot(jnp.concatenate(acts, axis=1), wd.reshape(k * wd.shape[1], wd.shape[2]),
                      preferred_element_type=F32)

    @pl.when(step == 0)
    def _():
        shared = _swiglu_act(jnp.dot(hb, sgu_ref[...], preferred_element_type=F32))
        acc_ref[...] = contrib + jnp.dot(shared, sd_ref[...], preferred_element_type=F32)

    @pl.when(step > 0)
    def _():
        acc_ref[...] += contrib

    @pl.when(step == pl.num_programs(1) - 1)
    def _():
        y = x_ref[...] + g2_ref[0] * acc_ref[...]
        if final_norm:
            ms = jnp.mean(y * y, axis=-1, keepdims=True)
            y = y * lax.rsqrt(ms + EPS) * fin_ref[...]
        o_ref[...] = y


def _moe(h2, gates, x, g2, row_of_tile, pw, tm, final_norm):
    t, d = x.shape
    ne, _, de2 = pw["wgu"].shape
    k = MOE_EXPERTS_PER_STEP
    full = lambda a: pl.BlockSpec(a.shape, lambda i, e: (0,) * a.ndim)
    return pl.pallas_call(
        functools.partial(_moe_kernel, final_norm=final_norm),
        grid=(t // tm, ne // k),
        in_specs=[
            pl.BlockSpec((tm, d), lambda i, e: (i, 0)),
            pl.BlockSpec((tm, LANES), lambda i, e: (i, 0)),
            pl.BlockSpec((tm, d), lambda i, e: (i, 0)),
            pl.BlockSpec((1, 1, d), lambda i, e: (row_of_tile(i), 0, 0)),
            pl.BlockSpec((k, d, de2), lambda i, e: (e, 0, 0)),
            pl.BlockSpec((k, de2 // 2, d), lambda i, e: (e, 0, 0)),
            full(pw["sgu"]), full(pw["sd"]), full(pw["fin"]),
        ],
        out_specs=pl.BlockSpec((tm, d), lambda i, e: (i, 0)),
        out_shape=jax.ShapeDtypeStruct((t, d), F32),
        scratch_shapes=[pltpu.VMEM((tm, d), F32)],
        compiler_params=_cparams(("parallel", "arbitrary")),
        name="moe_final" if final_norm else "moe",
    )(h2, gates, x, g2, pw["wgu"], pw["wd"], pw["sgu"], pw["sd"], pw["fin"])


def _axial_angles(n_tok, rot_dim):
    half = rot_dim // 2
    freqs = ROPE_BASE ** (-jnp.arange(0, half, 2, dtype=F32) / half)
    t = jnp.arange(n_tok, dtype=jnp.int32)
    row = (t // GRID_W).astype(F32)
    col = (t % GRID_W).astype(F32)
    return row[:, None] * freqs, col[:, None] * freqs


def _rope_table(n_tok, rot_dim):
    ar, ac = _axial_angles(n_tok, rot_dim)
    cos = jnp.concatenate([jnp.cos(ar)] * 2 + [jnp.cos(ac)] * 2, axis=1)
    sin = jnp.concatenate([-jnp.sin(ar), jnp.sin(ar), -jnp.sin(ac), jnp.sin(ac)], axis=1)
    return cos, sin


def _rope_tables(n_tok):
    cg, sg = _rope_table(n_tok, GQA_DIM)
    cd, sd = _rope_table(n_tok, DIFF_QK_DIM)
    cm32, sm32 = _rope_table(n_tok, MLA_ROPE)
    ones = jnp.ones((n_tok, MLA_NOPE), F32)
    zeros = jnp.zeros((n_tok, MLA_NOPE), F32)
    pad = LANES - MLA_NOPE - MLA_ROPE
    cm = jnp.concatenate([ones, cm32, ones[:, :pad]], axis=1)
    sm = jnp.concatenate([zeros, sm32, zeros[:, :pad]], axis=1)
    return (jnp.tile(cg, (1, LANES // GQA_DIM)), jnp.tile(sg, (1, LANES // GQA_DIM)),
            jnp.tile(cd, (1, LANES // DIFF_QK_DIM)), jnp.tile(sd, (1, LANES // DIFF_QK_DIM)),
            cm, sm)


def _pack_layer(l, p):
    d = p["w_in"].shape[1]
    w = p["w_in"][l]
    o = np.cumsum([0, 256, 128, 128, 256, 256, 256, 256, 256, 256, MLA_Q_LORA, MLA_KV_LORA, MLA_ROPE])
    col = lambda i: w[:, int(o[i]):int(o[i + 1])]
    dup = lambda a: jnp.concatenate([a[:, :GQA_DIM], a[:, :GQA_DIM], a[:, GQA_DIM:], a[:, GQA_DIM:]], axis=1)
    z = lambda n: jnp.zeros((d, n), F32)
    w_in = jnp.concatenate(
        [col(0), dup(col(1)), dup(col(2)), col(3), col(4), col(5), col(6), col(7), col(8),
         col(9), z(SEG_W - MLA_Q_LORA), col(10),
         z(MLA_NOPE), col(11), z(LANES - MLA_NOPE - MLA_ROPE)], axis=1).astype(BF16)
    dq = MLA_NOPE + MLA_ROPE
    wqu = p["mla_w_q_up"][l].reshape(MLA_Q_LORA, MLA_HEADS, dq)
    wqu = jnp.pad(wqu, ((0, SEG_W - MLA_Q_LORA), (0, 0), (0, LANES - dq)))
    wkv = p["mla_w_kv_up"][l].reshape(MLA_KV_LORA, MLA_HEADS, MLA_NOPE + MLA_V)
    wkk = jnp.pad(wkv[:, :, :MLA_NOPE], ((0, 0), (0, 0), (0, LANES - MLA_NOPE)))
    e = p["moe_w_gate"].shape[1]
    rw = jnp.pad(p["router_w"], ((0, 0), (0, LANES - e)))
    rw_hi = rw.astype(BF16)
    rw_lo = (rw - rw_hi.astype(F32)).astype(BF16)
    return {
        "gmix": p["norm_mix"][l][None], "gffn": p["norm_ffn"][l][None],
        "w_in": w_in,
        "wq_up": wqu.reshape(SEG_W, MLA_HEADS * LANES).astype(BF16),
        "wkv_k": wkk.reshape(MLA_KV_LORA, MLA_HEADS * LANES).astype(BF16),
        "wkv_v": wkv[:, :, MLA_NOPE:].reshape(MLA_KV_LORA, MLA_HEADS * MLA_V).astype(BF16),
        "gq": (jnp.tile(p["gqa_q_norm"][l], GQA_HEADS) * (GQA_DIM ** -0.5 * LOG2E))[None],
        "gk": jnp.tile(p["gqa_k_norm"][l], GQA_HEADS)[None],
        "gcq": jnp.pad(p["mla_q_norm"][l], (0, SEG_W - MLA_Q_LORA))[None],
        "gckv": p["mla_kv_norm"][l][None],
        "w_out": p["w_out"][l].astype(BF16),
        "rw_hi": rw_hi, "rw_lo": rw_lo,
        "rb": jnp.broadcast_to(p["router_b"][:, None], (e, LANES)),
        "wgu": jnp.concatenate([p["moe_w_gate"][l], p["moe_w_up"][l]], axis=2).astype(BF16),
        "wd": p["moe_w_down"][l].astype(BF16),
        "sgu": jnp.concatenate([p["shared_w_gate"][l], p["shared_w_up"][l]], axis=1).astype(BF16),
        "sd": p["shared_w_down"][l].astype(BF16),
        "fin": p["final_norm"][None],
        "diff": (p["diff_lq1"][l][None], p["diff_lk1"][l][None], p["diff_lq2"][l][None],
                 p["diff_lk2"][l][None], p["diff_subln"][l][:, None]),
    }


def kernel(x, c, ctx, c_ctx, w_mod, b_mod, norm_mix, norm_ffn, w_in, w_out, gqa_q_norm, gqa_k_norm,
           na_rel_bias, diff_lq1, diff_lk1, diff_lq2, diff_lk2, diff_subln, mla_q_norm, mla_w_q_up,
           mla_kv_norm, mla_w_kv_up, router_w, router_b, moe_w_gate, moe_w_up, moe_w_down,
           shared_w_gate, shared_w_up, shared_w_down, final_norm):
    p = dict(w_in=w_in, norm_mix=norm_mix, norm_ffn=norm_ffn, w_out=w_out, gqa_q_norm=gqa_q_norm,
             gqa_k_norm=gqa_k_norm, diff_lq1=diff_lq1, diff_lk1=diff_lk1, diff_lq2=diff_lq2,
             diff_lk2=diff_lk2, diff_subln=diff_subln, mla_q_norm=mla_q_norm, mla_w_q_up=mla_w_q_up,
             mla_kv_norm=mla_kv_norm, mla_w_kv_up=mla_w_kv_up, router_w=router_w, router_b=router_b,
             moe_w_gate=moe_w_gate, moe_w_up=moe_w_up, moe_w_down=moe_w_down,
             shared_w_gate=shared_w_gate, shared_w_up=shared_w_up, shared_w_down=shared_w_down,
             final_norm=final_norm)
    b, s, d = x.shape
    nctx = ctx.shape[1]
    depth = w_mod.shape[0]
    rows = s // GRID_W
    assert s % TOK_TILE == 0 and s // NA_TILE >= NA_CHUNKS + NA_QCHUNKS
    assert nctx % LANES == 0 and (b * s) % MOE_TILE == 0 and s % MOE_TILE == 0

    mrows = -(-(b + 1) // SUBLANES) * SUBLANES
    cc = jnp.zeros((mrows, d), F32).at[:b].set(c).at[b].set(c_ctx)
    mod = _modulation(cc, w_mod, b_mod)
    tables = _rope_tables(s)
    lat_row = lambda bi: bi
    ctx_row = lambda bi: b
    tm_ctx = b * nctx if b * nctx <= MOE_TILE else nctx

    xc = ctx
    for l in range(depth):
        want_ctx = l < depth - 1
        lam_init = 0.8 - 0.6 * math.exp(-0.3 * l)
        pw = _pack_layer(l, p)
        m6 = mod[l].reshape(mrows, 6, 1, d)
        sh1, sc1, g1, sh2, sc2, g2 = (m6[:, k] for k in range(6))

        lat = _project(x, sh1, sc1, lat_row, pw, tables, TOK_TILE, LAT_OUTS)
        cx = _project(xc, sh1, sc1, ctx_row, pw, None, nctx, CTX_OUTS if want_ctx else CTX_KV_OUTS)

        def dense(g, kind, extras=()):
            return _flash_lat(lat[g + ".qT"], cx[g + ".k"], cx[g + ".vT"], lat[g + ".k"], lat[g + ".vT"],
                              kind, extras, lam_init)

        def ctx_only(g, kind, extras=()):
            return _flash_ctx(cx[g + ".qT"], cx[g + ".k"], cx[g + ".vT"], kind, extras, lam_init)

        a_lat = dense("A", "pair")
        bias = _na_bias_tiles(na_rel_bias[l], rows, NA_TILE)
        b_lat = _na_attention(lat["B.qT"], cx["B.k"], cx["B.vT"], lat["B.k"], lat["B.vT"], bias)
        c_lat = dense("C", "diff", pw["diff"])
        d_lat = dense("D", "mla")
        x1, h2, gates = _out_router((a_lat, b_lat, c_lat, d_lat), x, g1, sh2, sc2, lat_row, pw, TOK_TILE)
        tiles_per_batch = s // MOE_TILE
        x = _moe(h2.reshape(b * s, d), gates.reshape(b * s, LANES), x1.reshape(b * s, d), g2,
                 lambda i: i // tiles_per_batch, pw, MOE_TILE, not want_ctx).reshape(b, s, d)

        if want_ctx:
            a_c = ctx_only("A", "pair")
            b_c = ctx_only("B", "pair")
            c_c = ctx_only("C", "diff", pw["diff"])
            d_c = ctx_only("D", "mla")
            xc1, h2c, gates_c = _out_router((a_c, b_c, c_c, d_c), xc, g1, sh2, sc2, ctx_row, pw, nctx)
            xc = _moe(h2c.reshape(b * nctx, d), gates_c.reshape(b * nctx, LANES),
                      xc1.reshape(b * nctx, d), g2, lambda i: b, pw, tm_ctx, False).reshape(b, nctx, d)
    return x
```

```python
import functools
import math

import numpy as np
import jax
import jax.numpy as jnp
from jax import lax
from jax.experimental import pallas as pl
from jax.experimental.pallas import tpu as pltpu

F32 = jnp.float32
BF16 = jnp.bfloat16

GRID_W = 64
EPS = 1e-6
ROPE_BASE = 10000.0
GQA_HEADS, GQA_KV_HEADS, GQA_DIM = 4, 2, 64
NA_HEADS, NA_DIM, NA_WIN_ROWS, NA_WIN_COLS = 4, 64, 8, 16
DIFF_HEADS, DIFF_QK_DIM, DIFF_V_DIM = 4, 32, 64
MLA_HEADS, MLA_NOPE, MLA_ROPE, MLA_V, MLA_Q_LORA, MLA_KV_LORA = 4, 64, 32, 64, 192, 128
N_EXPERTS, MOE_GROUPS, D_EXPERT = 16, 4, 256
EXPERTS_PER_GROUP = N_EXPERTS // MOE_GROUPS

LANES = 128
SUBLANES = 8
VMEM_LIMIT = 56 * 1024 * 1024

TOK_TILE = 512
NA_TILE = 256
MOE_TILE = 1024
MOD_TILE = 1024
PROJ_SUBTILES = 2
OUT_SUBTILES = 2
NEG = -1e30
LOG2E = math.log2(math.e)

SEG_W = 256
OFF_AQ, OFF_AK, OFF_AV = 0, 256, 512
OFF_BQ, OFF_BK, OFF_BV = 768, 1024, 1280
OFF_CQ, OFF_CK, OFF_CV = 1536, 1792, 2048
OFF_DQ, OFF_DKV, OFF_DPE = 2304, 2560, 2688
IN_PACKED = 2816


def _cparams(sem):
    return pltpu.CompilerParams(dimension_semantics=sem, vmem_limit_bytes=VMEM_LIMIT)


def _lane_iota(shape):
    return lax.broadcasted_iota(jnp.int32, shape, 1)


def _lane_group(shape, width):
    return lax.shift_right_logical(_lane_iota(shape), int(math.log2(width)))


def _sigmoid(x):
    return 1.0 / (1.0 + jnp.exp(-x))


def _mod_kernel(c_ref, w_ref, b_ref, o_ref):
    c = c_ref[...]
    s = c * _sigmoid(c)
    o_ref[0] = jnp.dot(s, w_ref[0], preferred_element_type=F32,
                       precision=lax.Precision.HIGHEST) + b_ref[0]


def _modulation(cc, w_mod, b_mod):
    depth, d, n = w_mod.shape
    rows = cc.shape[0]
    return pl.pallas_call(
        _mod_kernel,
        grid=(depth, n // MOD_TILE),
        in_specs=[
            pl.BlockSpec((rows, d), lambda l, j: (0, 0)),
            pl.BlockSpec((1, d, MOD_TILE), lambda l, j: (l, 0, j)),
            pl.BlockSpec((1, 1, MOD_TILE), lambda l, j: (l, 0, j)),
        ],
        out_specs=pl.BlockSpec((1, rows, MOD_TILE), lambda l, j: (l, 0, j)),
        out_shape=jax.ShapeDtypeStruct((depth, rows, n), F32),
        compiler_params=_cparams(("parallel", "parallel")),
        name="modulation",
    )(cc, w_mod, b_mod.reshape(depth, 1, n))


def _swap_halves(x, half):
    w = x.shape[1]
    lane = _lane_iota(x.shape)
    fwd = pltpu.roll(x, half, 1)
    bwd = pltpu.roll(x, w - half, 1)
    return jnp.where((lane & (2 * half - 1)) < half, bwd, fwd)


def _rope(x, cos, sin, half):
    reps = x.shape[1] // cos.shape[1]
    c = jnp.concatenate([cos] * reps, axis=1) if reps > 1 else cos
    s = jnp.concatenate([sin] * reps, axis=1) if reps > 1 else sin
    return x * c + _swap_halves(x, half) * s


def _group_rms(x, group, count):
    gid = _lane_group(x.shape, group)
    x2 = x * x
    inv = jnp.zeros_like(x)
    for g in range(x.shape[1] // group):
        msk = gid == g
        ms = jnp.sum(jnp.where(msk, x2, 0.0), axis=1, keepdims=True) * (1.0 / count)
        inv = jnp.where(msk, lax.rsqrt(ms + EPS), inv)
    return x * inv


LAT_OUTS = ("A.qT", "A.k", "A.vT", "B.qT", "B.k", "B.vT", "C.qT", "C.k", "C.vT", "D.qT", "D.k", "D.vT")
CTX_KV_OUTS = ("A.k", "A.vT", "B.k", "B.vT", "C.k", "C.vT", "D.k", "D.vT")
CTX_OUTS = CTX_KV_OUTS + ("A.qT", "B.qT", "C.qT", "D.qT")
GROUP_W = {"A": SEG_W, "B": SEG_W, "C": SEG_W, "D": 2 * SEG_W}


def _proj_kernel(*refs, rope, outs):
    (x_ref, sh_ref, sc_ref, gmix_ref, w_ref, wqu_ref, wkk_ref, wkv_ref,
     gq_ref, gk_ref, gcq_ref, gckv_ref) = refs[:12]
    pos = 12
    tabs = refs[pos:pos + 6] if rope else ()
    pos += len(tabs)
    out = dict(zip(outs, refs[pos:pos + len(outs)]))
    tm = x_ref.shape[1]
    parts = PROJ_SUBTILES if tm % (PROJ_SUBTILES * 2 * LANES) == 0 else 1
    sub = tm // parts
    for h in range(parts):
        _proj_subtile(slice(h * sub, (h + 1) * sub), x_ref, sh_ref, sc_ref, gmix_ref, w_ref, wqu_ref,
                      wkk_ref, wkv_ref, gq_ref, gk_ref, gcq_ref, gckv_ref, tabs, out, rope)


def _proj_subtile(rows, x_ref, sh_ref, sc_ref, gmix_ref, w_ref, wqu_ref, wkk_ref, wkv_ref,
                  gq_ref, gk_ref, gcq_ref, gckv_ref, tabs, out, rope):
    r0, sub = rows.start, rows.stop - rows.start
    if rope:
        cg, sg, cd, sd, cm, sm = (t[rows, :] for t in tabs)

    def want(g, t):
        return (g + "." + t) in out or (g + "." + t + "T") in out

    def put(g, t, val):
        if g + "." + t in out:
            out[g + "." + t][0, rows, :] = val.astype(BF16)
        if g + "." + t + "T" in out:
            ref = out[g + "." + t + "T"]
            vt = val.T.astype(BF16)
            cw = ref.shape[3]
            if cw >= sub:
                ref[0, r0 // cw, :, r0 % cw:r0 % cw + sub] = vt
            else:
                for c in range(sub // cw):
                    ref[0, r0 // cw + c] = vt[:, c * cw:(c + 1) * cw]

    x = x_ref[0, rows, :]
    ms = jnp.mean(x * x, axis=-1, keepdims=True)
    h = x * lax.rsqrt(ms + EPS) * gmix_ref[...]
    h = h * (1.0 + sc_ref[0]) + sh_ref[0]
    hb = h.astype(BF16)

    def seg(off, width=SEG_W):
        return jnp.dot(hb, w_ref[:, off:off + width], preferred_element_type=F32)

    if want("A", "q"):
        q = _group_rms(seg(OFF_AQ), GQA_DIM, GQA_DIM) * gq_ref[...]
        if rope:
            q = _rope(q, cg, sg, GQA_DIM // 4)
        put("A", "q", q)
    k = _group_rms(seg(OFF_AK), GQA_DIM, GQA_DIM) * gk_ref[...]
    if rope:
        k = _rope(k, cg, sg, GQA_DIM // 4)
    put("A", "k", k)
    put("A", "v", seg(OFF_AV))

    if want("B", "q"):
        put("B", "q", seg(OFF_BQ) * (NA_DIM ** -0.5 * LOG2E))
    put("B", "k", seg(OFF_BK))
    put("B", "v", seg(OFF_BV))

    if want("C", "q"):
        q = seg(OFF_CQ) * (DIFF_QK_DIM ** -0.5 * LOG2E)
        if rope:
            q = _rope(q, cd, sd, DIFF_QK_DIM // 4)
        put("C", "q", q)
    k = seg(OFF_CK)
    if rope:
        k = _rope(k, cd, sd, DIFF_QK_DIM // 4)
    put("C", "k", k)
    put("C", "v", seg(OFF_CV))

    if want("D", "q"):
        cq = _group_rms(seg(OFF_DQ), SEG_W, MLA_Q_LORA) * gcq_ref[...]
        q = jnp.dot(cq.astype(BF16), wqu_ref[...], preferred_element_type=F32)
        q = q * ((MLA_NOPE + MLA_ROPE) ** -0.5 * LOG2E)
        if rope:
            q = _rope(q, cm, sm, MLA_ROPE // 4)
        put("D", "q", q)
    ckv = _group_rms(seg(OFF_DKV, LANES), LANES, MLA_KV_LORA) * gckv_ref[...]
    ckvb = ckv.astype(BF16)
    k = jnp.dot(ckvb, wkk_ref[...], preferred_element_type=F32)
    kpe = seg(OFF_DPE, LANES)
    if rope:
        kpe = _rope(kpe, cm, sm, MLA_ROPE // 4)
    put("D", "k", k + jnp.concatenate([kpe] * MLA_HEADS, axis=1))
    put("D", "v", jnp.dot(ckvb, wkv_ref[...], preferred_element_type=F32))


def _project(x, sh, sc, row_of_batch, pw, tables, tm, outs):
    b, n, d = x.shape
    nt = n // tm
    rope = tables is not None
    full = lambda a: pl.BlockSpec(a.shape, lambda bi, i: (0,) * a.ndim)
    mod_spec = pl.BlockSpec((1, 1, d), lambda bi, i: (row_of_batch(bi), 0, 0))
    ins = [x, sh, sc, pw["gmix"], pw["w_in"], pw["wq_up"], pw["wkv_k"], pw["wkv_v"],
           pw["gq"], pw["gk"], pw["gcq"], pw["gckv"]]
    in_specs = [pl.BlockSpec((1, tm, d), lambda bi, i: (bi, i, 0)), mod_spec, mod_spec]
    in_specs += [full(a) for a in ins[3:]]
    if rope:
        ins += list(tables)
        in_specs += [pl.BlockSpec((tm, LANES), lambda bi, i: (i, 0)) for _ in tables]
    shapes, specs = [], []
    for name in outs:
        g, t = name.split(".")
        w = SEG_W if t[0] == "v" else GROUP_W[g]
        if t.endswith("T"):
            cw = min(tm, NA_TILE) if g == "B" else tm
            shapes.append(jax.ShapeDtypeStruct((b, n // cw, w, cw), BF16))
            specs.append(pl.BlockSpec((1, tm // cw, w, cw), lambda bi, i: (bi, i, 0, 0)))
        else:
            shapes.append(jax.ShapeDtypeStruct((b, n, w), BF16))
            specs.append(pl.BlockSpec((1, tm, w), lambda bi, i: (bi, i, 0)))
    res = pl.pallas_call(
        functools.partial(_proj_kernel, rope=rope, outs=tuple(outs)),
        grid=(b, nt),
        in_specs=in_specs,
        out_specs=specs,
        out_shape=shapes,
        compiler_params=_cparams(("parallel", "parallel")),
        name="proj_rope" if rope else "proj_ctx",
    )(*ins)
    return dict(zip(outs, res))


HEAD_V = 64


def _tile_lanes(x, width):
    reps = width // x.shape[1]
    return jnp.tile(x, (1, reps)) if reps > 1 else x


def with_ones(vt):
    rows = lax.broadcasted_iota(jnp.int32, (LANES - HEAD_V, vt.shape[1]), 0)
    ones = jnp.where(rows == 0, 1.0, 0.0).astype(BF16)
    return jnp.concatenate([vt, ones], axis=0)


def _map_slices(kind):
    def k_lanes(m):
        return slice(m * LANES, (m + 1) * LANES) if kind == "mla" else slice(None)

    def v_rows(m):
        hv = m // 2 if kind == "diff" else m
        return slice(hv * HEAD_V, (hv + 1) * HEAD_V)

    return k_lanes, v_rows


def _query_slabs(qt_ref, kind, nm):
    qv = jnp.concatenate([qt_ref[0, c] for c in range(qt_ref.shape[1])], axis=1)
    if kind == "mla":
        return [qv[m * LANES:(m + 1) * LANES, :] for m in range(nm)]
    rows = lax.broadcasted_iota(jnp.int32, qv.shape, 0)
    grp = lax.shift_right_logical(rows, int(math.log2(LANES // nm)))
    return [jnp.where(grp == m, qv, jnp.zeros_like(qv)) for m in range(nm)]


def _finish_heads(accs, kind, lam_init, diff_refs):
    outs = [a[0:HEAD_V] / a[HEAD_V:HEAD_V + 1] for a in accs]
    if kind == "diff":
        lq1_ref, lk1_ref, lq2_ref, lk2_ref, sub_ref = diff_refs
        lam = (jnp.exp(jnp.sum(lq1_ref[...] * lk1_ref[...], axis=1, keepdims=True))
               - jnp.exp(jnp.sum(lq2_ref[...] * lk2_ref[...], axis=1, keepdims=True))
               + lam_init)
        heads = []
        for hd in range(2):
            dlt = outs[2 * hd] - lam * outs[2 * hd + 1]
            ms2 = jnp.mean(dlt * dlt, axis=0, keepdims=True)
            heads.append(dlt * lax.rsqrt(ms2 + EPS) * sub_ref[...] * (1.0 - lam_init))
        outs = heads
    return jnp.concatenate(outs, axis=0).T


def _flash_ctx_kernel(*refs, kind, lam_init):
    qt_ref, kc_ref, vct_ref = refs[:3]
    nd = 5 if kind == "diff" else 0
    diff_refs, o_ref = refs[3:3 + nd], refs[3 + nd]
    nm = 4 if kind == "diff" else 2
    k_lanes, v_rows = _map_slices(kind)
    accs = []
    for m, qm in enumerate(_query_slabs(qt_ref, kind, nm)):
        st = jnp.dot(kc_ref[0, :, k_lanes(m)], qm, preferred_element_type=F32)
        p = jnp.exp2(st - jnp.max(st, axis=0, keepdims=True)).astype(BF16)
        accs.append(jnp.dot(with_ones(vct_ref[0, 0, v_rows(m), :]), p, preferred_element_type=F32))
    o_ref[0] = _finish_heads(accs, kind, lam_init, diff_refs).astype(o_ref.dtype)


def _flash_lat_kernel(*refs, kind, lam_init):
    qt_ref, kc_ref, vct_ref, k_e, vt_e, k_l, vt_l = refs[:7]
    nd = 5 if kind == "diff" else 0
    diff_refs, o_ref = refs[7:7 + nd], refs[7 + nd]
    qs_e, acc_e, s_e, qs_l, acc_l, s_l, m_ref, mc_ref = refs[8 + nd:16 + nd]
    nk, tk = vt_e.shape[1], vt_e.shape[3]
    half = nk // 2
    iters = half // 2
    early = (qs_e, acc_e, s_e, 0, False, k_e, vt_e)
    late = (qs_l, acc_l, s_l, half, True, k_l, vt_l)

    nm = qs_e.shape[0]
    g = pl.program_id(0)
    last = pl.num_programs(0) - 1
    k_lanes, v_rows = _map_slices(kind)

    def produce(tile, slot, j, m):
        qs_ref, _, s_ref = tile[:3]
        rows = pl.ds(pl.multiple_of(j * tk, tk), tk)
        st = jnp.dot(tile[5][0, rows, k_lanes(m)], qs_ref[m], preferred_element_type=F32)
        s_ref[slot, m] = st
        return jnp.max(st, axis=0, keepdims=True)

    def consume(tile, slot, j, m, m_prev, m_chunk):
        _, acc_ref, s_ref = tile[:3]
        m_new = jnp.maximum(m_prev, m_chunk)
        alpha = jnp.exp2(m_prev - m_new)
        p = jnp.exp2(s_ref[slot, m] - m_new).astype(BF16)
        acc_ref[m] = alpha * acc_ref[m] + jnp.dot(
            with_ones(tile[6][0, j, v_rows(m), :]), p, preferred_element_type=F32)
        return m_new

    def iteration(i, carries, tiles, last):
        maps = range(nm)
        j0s = [tile[3] + 2 * i for tile in tiles]
        c1 = [[produce(tile, 1, j0 + 1, m) for m in maps] for tile, j0 in zip(tiles, j0s)]
        m1 = [[consume(tile, 0, j0, m, c[0][m], c[1][m]) for m in maps]
              for tile, j0, c in zip(tiles, j0s, carries)]
        c2 = [c1[t] if (last and tile[4]) else [produce(tile, 0, j0 + 2, m) for m in maps]
              for t, (tile, j0) in enumerate(zip(tiles, j0s))]
        m2 = [[consume(tile, 1, j0 + 1, m, m1[t][m], c1[t][m]) for m in maps]
              for t, (tile, j0) in enumerate(zip(tiles, j0s))]
        return tuple((tuple(a), tuple(b)) for a, b in zip(m2, c2))

    def run(carries, tiles):
        carries = iteration(0, carries, tiles, iters == 1)
        if iters > 2:
            carries = lax.fori_loop(1, iters - 1, lambda i, c: iteration(i, c, tiles, False), carries)
        if iters > 1:
            carries = iteration(iters - 1, carries, tiles, True)
        return carries

    def start_early():
        for m, qm in enumerate(_query_slabs(qt_ref, kind, nm)):
            qs_e[m] = qm
        sts = [jnp.dot(kc_ref[0, :, k_lanes(m)], qs_e[m], preferred_element_type=F32)
               for m in range(nm)]
        ms = tuple(jnp.max(st, axis=0, keepdims=True) for st in sts)
        first = tuple(produce(early, 0, 0, m) for m in range(nm))

        def init_from_context():
            for m in range(nm):
                p = jnp.exp2(sts[m] - ms[m]).astype(BF16)
                acc_e[m] = jnp.dot(with_ones(vct_ref[0, 0, v_rows(m), :]), p,
                                   preferred_element_type=F32)

        return (ms, first), init_from_context

    def load_late():
        return (tuple(m_ref[m] for m in range(nm)), tuple(mc_ref[m] for m in range(nm)))

    def hand_over(carry):
        qs_l[...] = qs_e[...]
        acc_l[...] = acc_e[...]
        s_l[0] = s_e[0]
        for m in range(nm):
            m_ref[m] = carry[0][m]
            mc_ref[m] = carry[1][m]

    def finish_late():
        accs = [acc_l[m] for m in range(nm)]
        o_ref[0] = _finish_heads(accs, kind, lam_init, diff_refs).astype(o_ref.dtype)

    @pl.when(g == 0)
    def _():
        carry, init_from_context = start_early()
        init_from_context()
        hand_over(run((carry,), (early,))[0])

    @pl.when((g > 0) & (g < last))
    def _():
        carry_l = load_late()
        carry_e, init_from_context = start_early()
        init_from_context()
        carry_l, carry_e = run((carry_l, carry_e), (late, early))
        finish_late()
        hand_over(carry_e)

    @pl.when(g == last)
    def _():
        run((load_late(),), (late,))
        finish_late()


def _flash_specs(qt, kc, vct, kind, extras):
    b, nqc, wtot, cw = qt.shape
    wq = 2 * LANES if kind == "mla" else LANES
    c = kc.shape[1]
    extra_specs = [pl.BlockSpec(e.shape, lambda bi, p, i: (0, 0)) for e in extras]
    kv_specs = [pl.BlockSpec((1, c, wq), lambda bi, p, i: (bi, 0, p)),
                pl.BlockSpec((1, 1, LANES, c), lambda bi, p, i: (bi, 0, p, 0))]
    return b, nqc, wtot // wq, wq, cw, kv_specs, extra_specs


def _flash_ctx(qt, kc, vct, kind, extras=(), lam_init=0.0):
    b, nqc, npair, wq, cw, kv_specs, extra_specs = _flash_specs(qt, kc, vct, kind, extras)
    return pl.pallas_call(
        functools.partial(_flash_ctx_kernel, kind=kind, lam_init=lam_init),
        grid=(b, npair, nqc),
        in_specs=[pl.BlockSpec((1, 1, wq, cw), lambda bi, p, i: (bi, i, p, 0))] + kv_specs + extra_specs,
        out_specs=pl.BlockSpec((1, cw, LANES), lambda bi, p, i: (bi, i, p)),
        out_shape=jax.ShapeDtypeStruct((b, nqc * cw, npair * LANES), BF16),
        compiler_params=_cparams(("parallel", "parallel", "parallel")),
        name="flash_" + kind + "_ctx",
    )(qt, kc, vct, *extras)


def _flash_lat(qt, kc, vct, k, vt, kind, extras=(), lam_init=0.0):
    b, nq, wtot, tq = qt.shape
    wq = 2 * LANES if kind == "mla" else LANES
    npair = wtot // wq
    c = kc.shape[1]
    nm = 4 if kind == "diff" else 2
    nk, tk = vt.shape[1], vt.shape[3]
    assert nk % 4 == 0, "two phases of key chunks, processed in pairs"
    ntiles = b * npair * nq

    def tile_of(t):
        return t // (npair * nq), (t // nq) % npair, t % nq

    def early(fn):
        return lambda g: fn(*tile_of(jnp.minimum(g, ntiles - 1)))

    def late(fn):
        return lambda g: fn(*tile_of(jnp.maximum(g - 1, 0)))

    k_block, vt_block = (1, k.shape[1], wq), (1, nk, LANES, tk)
    k_map, vt_map = (lambda bi, p, i: (bi, 0, p)), (lambda bi, p, i: (bi, 0, p, 0))
    return pl.pallas_call(
        functools.partial(_flash_lat_kernel, kind=kind, lam_init=lam_init),
        grid=(ntiles + 1,),
        in_specs=[
            pl.BlockSpec((1, 1, wq, tq), early(lambda bi, p, i: (bi, i, p, 0))),
            pl.BlockSpec((1, c, wq), early(lambda bi, p, i: (bi, 0, p))),
            pl.BlockSpec((1, 1, LANES, c), early(lambda bi, p, i: (bi, 0, p, 0))),
            pl.BlockSpec(k_block, early(k_map)), pl.BlockSpec(vt_block, early(vt_map)),
            pl.BlockSpec(k_block, late(k_map)), pl.BlockSpec(vt_block, late(vt_map)),
        ] + [pl.BlockSpec(e.shape, lambda g: (0, 0)) for e in extras],
        out_specs=pl.BlockSpec((1, tq, LANES), late(lambda bi, p, i: (bi, i, p))),
        out_shape=jax.ShapeDtypeStruct((b, nq * tq, npair * LANES), BF16),
        scratch_shapes=2 * [
            pltpu.VMEM((nm, LANES, tq), BF16),
            pltpu.VMEM((nm, LANES, tq), F32),
            pltpu.VMEM((2, nm, tk, tq), F32),
        ] + 2 * [pltpu.VMEM((nm, 1, tq), F32)],
        compiler_params=_cparams(("arbitrary",)),
        name="flash_" + kind,
    )(qt, kc, vct, k, vt, k, vt, *extras)


NA_QCHUNKS = 2
NA_CHUNKS = NA_QCHUNKS + 2


def _na_kernel(qt_ref, k_ref, vt_ref, kc_ref, vct_ref, bias_ref, o_ref):
    nkc, tk = vt_ref.shape[1], vt_ref.shape[3]
    i = pl.program_id(2)
    c0 = jnp.clip(i * NA_QCHUNKS - 1, 0, nkc - NA_CHUNKS)
    qv = jnp.concatenate([qt_ref[0, c] for c in range(NA_QCHUNKS)], axis=1)
    rows = lax.broadcasted_iota(jnp.int32, qv.shape, 0)
    scores = []
    for m in range(2):
        own = lax.shift_right_logical(rows, int(math.log2(LANES // 2))) == m
        qm = jnp.where(own, qv, jnp.zeros_like(qv))
        ss = []
        for j in range(NA_CHUNKS):
            kj = k_ref[0, pl.ds(pl.multiple_of((c0 + j) * tk, tk), tk), :]
            ss.append(jnp.dot(kj, qm, preferred_element_type=F32) + bias_ref[0, m, j])
        scores.append((ss, jnp.dot(kc_ref[0], qm, preferred_element_type=F32)))
    outs = []
    for m, (ss, sc) in enumerate(scores):
        v_rows = slice(m * HEAD_V, (m + 1) * HEAD_V)
        mx = jnp.max(sc, axis=0, keepdims=True)
        for st in ss:
            mx = jnp.maximum(mx, jnp.max(st, axis=0, keepdims=True))
        acc = jnp.dot(with_ones(vct_ref[0, 0, v_rows, :]), jnp.exp2(sc - mx).astype(BF16),
                      preferred_element_type=F32)
        for j, st in enumerate(ss):
            acc = acc + jnp.dot(with_ones(vt_ref[0, c0 + j, v_rows, :]),
                                jnp.exp2(st - mx).astype(BF16), preferred_element_type=F32)
        outs.append(acc[0:HEAD_V] / acc[HEAD_V:HEAD_V + 1])
    o_ref[0] = jnp.concatenate(outs, axis=0).T.astype(o_ref.dtype)


def _na_bias_tiles(rel_bias, rows, tk):
    qrows = NA_QCHUNKS * tk // GRID_W
    nblk = rows // qrows
    nk = rows * GRID_W // tk
    krows = NA_CHUNKS * tk // GRID_W
    kh = min(NA_WIN_ROWS, rows)
    kw = NA_WIN_COLS
    blocks = np.array([0, 1, nblk - 1])
    c0 = np.clip(blocks * NA_QCHUNKS - 1, 0, nk - NA_CHUNKS)
    r = blocks[:, None] * qrows + np.arange(qrows)[None, :]
    kr = c0[:, None] * (tk // GRID_W) + np.arange(krows)[None, :]
    r0 = np.clip(r - kh // 2, 0, rows - kh)
    dr = kr[:, None, :] - r[:, :, None]
    valid_r = (kr[:, None, :] >= r0[:, :, None]) & (kr[:, None, :] < r0[:, :, None] + kh)
    cols = np.arange(GRID_W)
    cstart = np.clip(cols - kw // 2, 0, GRID_W - kw)
    dc = cols[None, :] - cols[:, None]
    valid_c = (cols[None, :] >= cstart[:, None]) & (cols[None, :] < cstart[:, None] + kw)
    ndc = 2 * NA_WIN_COLS - 1
    onehot = ((dc[None] + NA_WIN_COLS - 1) == np.arange(ndc)[:, None, None]) & valid_c[None]
    dr_idx = np.clip(dr + NA_WIN_ROWS - 1, 0, 2 * NA_WIN_ROWS - 2)
    g1 = rel_bias[:, dr_idx, :]
    tiles = jnp.einsum("htabd,dcx->thbxac", g1, jnp.asarray(onehot, F32),
                       precision=lax.Precision.HIGHEST)
    valid = valid_r.transpose(0, 2, 1)[:, None, :, None, :, None] & valid_c.T[None, None, None, :, None, :]
    tiles = jnp.where(jnp.asarray(valid), tiles * LOG2E, NEG)
    return tiles.reshape(3, rel_bias.shape[0], NA_CHUNKS, tk, qrows * GRID_W)


def _na_attention(qt, kc, vct, k, vt, bias):
    b, nqc, wtot, cw = qt.shape
    nblk, tq = nqc // NA_QCHUNKS, NA_QCHUNKS * cw
    npair = wtot // LANES
    nkc, tk = vt.shape[1], vt.shape[3]
    c = kc.shape[1]
    n = k.shape[1]

    def bias_map(bi, p, i):
        return (jnp.where(i == 0, 0, jnp.where(i == nblk - 1, 2, 1)), p, 0, 0, 0)

    return pl.pallas_call(
        _na_kernel,
        grid=(b, npair, nblk),
        in_specs=[
            pl.BlockSpec((1, NA_QCHUNKS, LANES, cw), lambda bi, p, i: (bi, i, p, 0)),
            pl.BlockSpec((1, n, LANES), lambda bi, p, i: (bi, 0, p)),
            pl.BlockSpec((1, nkc, LANES, tk), lambda bi, p, i: (bi, 0, p, 0)),
            pl.BlockSpec((1, c, LANES), lambda bi, p, i: (bi, 0, p)),
            pl.BlockSpec((1, 1, LANES, c), lambda bi, p, i: (bi, 0, p, 0)),
            pl.BlockSpec((1, 2, NA_CHUNKS, tk, tq), bias_map),
        ],
        out_specs=pl.BlockSpec((1, tq, LANES), lambda bi, p, i: (bi, i, p)),
        out_shape=jax.ShapeDtypeStruct((b, nblk * tq, wtot), BF16),
        compiler_params=_cparams(("parallel", "parallel", "parallel")),
        name="na_attention",
    )(qt, k, vt, kc, vct, bias)


def _first_index_of_max(vals, row, big):
    mx = jnp.max(vals, axis=0, keepdims=True)
    idx = jnp.min(jnp.where(vals == mx, row, big), axis=0, keepdims=True)
    return mx, idx


def _router_gates(scores_t, sel_t):
    irow = lax.broadcasted_iota(jnp.int32, sel_t.shape, 0)
    row = irow.astype(F32)
    grp = lax.shift_right_logical(irow, int(math.log2(EXPERTS_PER_GROUP))).astype(F32)
    big = float(N_EXPERTS)
    best = None
    for g in range(MOE_GROUPS):
        vals = jnp.where(grp == float(g), sel_t, -jnp.inf)
        m1, i1 = _first_index_of_max(vals, row, big)
        m2 = jnp.max(jnp.where(row == i1, -jnp.inf, vals), axis=0, keepdims=True)
        gs = m1 + m2
        if best is None:
            best, bi = gs, jnp.zeros(gs.shape, F32)
        else:
            better = gs > best
            bi = jnp.where(better, float(g), bi)
            best = jnp.where(better, gs, best)
    msel = jnp.where(grp == bi, sel_t, -jnp.inf)
    _, i1 = _first_index_of_max(msel, row, big)
    msel2 = jnp.where(row == i1, -jnp.inf, msel)
    _, i2 = _first_index_of_max(msel2, row, big)
    w = jnp.where((row == i1) | (row == i2), scores_t, 0.0)
    return w / jnp.sum(w, axis=0, keepdims=True)


def _out_kernel(a_ref, b_ref, c_ref, d_ref, x_ref, g1_ref, sh_ref, sc_ref, gffn_ref,
                w_ref, rwh_ref, rwl_ref, rb_ref, xo_ref, h2_ref, gate_ref):
    tm = x_ref.shape[1]
    parts = OUT_SUBTILES if tm % (OUT_SUBTILES * LANES) == 0 else 1
    sub = tm // parts
    ys = []
    for h in range(parts):
        rows = slice(h * sub, (h + 1) * sub)
        y = None
        for gi, r in enumerate((a_ref, b_ref, c_ref, d_ref)):
            part = jnp.dot(r[0, rows, :], w_ref[gi * SEG_W:(gi + 1) * SEG_W, :],
                           preferred_element_type=F32)
            y = part if y is None else y + part
        ys.append(y)
    for h in range(parts):
        rows = slice(h * sub, (h + 1) * sub)
        xn = x_ref[0, rows, :] + g1_ref[0] * ys[h]
        xo_ref[0, rows, :] = xn
        ms = jnp.mean(xn * xn, axis=-1, keepdims=True)
        h2 = xn * lax.rsqrt(ms + EPS) * gffn_ref[...]
        h2 = h2 * (1.0 + sc_ref[0]) + sh_ref[0]
        hi = h2.astype(BF16)
        h2_ref[0, rows, :] = hi
        lo = (h2 - hi.astype(F32)).astype(BF16)
        logits = (jnp.dot(hi, rwh_ref[...], preferred_element_type=F32)
                  + jnp.dot(lo, rwh_ref[...], preferred_element_type=F32)
                  + jnp.dot(hi, rwl_ref[...], preferred_element_type=F32))
        scores_t = _sigmoid(logits).T[:N_EXPERTS]
        sel_t = scores_t + _tile_lanes(rb_ref[...], sub)
        gates_t = _router_gates(scores_t, sel_t)
        pad = jnp.zeros((LANES - N_EXPERTS, sub), F32)
        gate_ref[0, rows, :] = jnp.concatenate([gates_t, pad], axis=0).T


def _out_router(parts, x, g1, sh2, sc2, row_of_batch, pw, tm):
    b, n, d = x.shape
    full = lambda a: pl.BlockSpec(a.shape, lambda bi, i: (0,) * a.ndim)
    tok = lambda w: pl.BlockSpec((1, tm, w), lambda bi, i: (bi, i, 0))
    mod_spec = pl.BlockSpec((1, 1, d), lambda bi, i: (row_of_batch(bi), 0, 0))
    consts = [pw["gffn"], pw["w_out"], pw["rw_hi"], pw["rw_lo"], pw["rb"]]
    return pl.pallas_call(
        _out_kernel,
        grid=(b, n // tm),
        in_specs=[tok(SEG_W)] * 4 + [tok(d), mod_spec, mod_spec, mod_spec] + [full(a) for a in consts],
        out_specs=[tok(d), tok(d), tok(LANES)],
        out_shape=[jax.ShapeDtypeStruct((b, n, d), F32), jax.ShapeDtypeStruct((b, n, d), BF16),
                   jax.ShapeDtypeStruct((b, n, LANES), F32)],
        compiler_params=_cparams(("parallel", "parallel")),
        name="out_router",
    )(*parts, x, g1, sh2, sc2, *consts)


MOE_EXPERTS_PER_STEP = 4


def _swiglu_act(gu, scale=None):
    de = gu.shape[1] // 2
    g, u = gu[:, :de], gu[:, de:]
    act = (g * _sigmoid(g)) * u
    return (act if scale is None else act * scale).astype(BF16)


def _moe_kernel(h_ref, gate_ref, x_ref, g2_ref, wgu_ref, wd_ref, sgu_ref, sd_ref, fin_ref,
                o_ref, acc_ref, *, final_norm):
    step = pl.program_id(1)
    k = wgu_ref.shape[0]
    hb = h_ref[...]
    gates = gate_ref[...]
    lane = _lane_iota(gates.shape)
    acts = []
    for i in range(k):
        gu = jnp.dot(hb, wgu_ref[i], preferred_element_type=F32)
        col = jnp.sum(jnp.where(lane == step * k + i, gates, 0.0), axis=1, keepdims=True)
        acts.append(_swiglu_act(gu, col))
    wd = wd_ref[...]
    contrib = jnp.dot(jnp.concatenate(acts, axis=1), wd.reshape(k * wd.shape[1], wd.shape[2]),
                      preferred_element_type=F32)

    @pl.when(step == 0)
    def _():
        shared = _swiglu_act(jnp.dot(hb, sgu_ref[...], preferred_element_type=F32))
        acc_ref[...] = contrib + jnp.dot(shared, sd_ref[...], preferred_element_type=F32)

    @pl.when(step > 0)
    def _():
        acc_ref[...] += contrib

    @pl.when(step == pl.num_programs(1) - 1)
    def _():
        y = x_ref[...] + g2_ref[0] * acc_ref[...]
        if final_norm:
            ms = jnp.mean(y * y, axis=-1, keepdims=True)
            y = y * lax.rsqrt(ms + EPS) * fin_ref[...]
        o_ref[...] = y


def _moe(h2, gates, x, g2, row_of_tile, pw, tm, final_norm):
    t, d = x.shape
    ne, _, de2 = pw["wgu"].shape
    k = MOE_EXPERTS_PER_STEP
    full = lambda a: pl.BlockSpec(a.shape, lambda i, e: (0,) * a.ndim)
    return pl.pallas_call(
        functools.partial(_moe_kernel, final_norm=final_norm),
        grid=(t // tm, ne // k),
        in_specs=[
            pl.BlockSpec((tm, d), lambda i, e: (i, 0)),
            pl.BlockSpec((tm, LANES), lambda i, e: (i, 0)),
            pl.BlockSpec((tm, d), lambda i, e: (i, 0)),
            pl.BlockSpec((1, 1, d), lambda i, e: (row_of_tile(i), 0, 0)),
            pl.BlockSpec((k, d, de2), lambda i, e: (e, 0, 0)),
            pl.BlockSpec((k, de2 // 2, d), lambda i, e: (e, 0, 0)),
            full(pw["sgu"]), full(pw["sd"]), full(pw["fin"]),
        ],
        out_specs=pl.BlockSpec((tm, d), lambda i, e: (i, 0)),
        out_shape=jax.ShapeDtypeStruct((t, d), F32),
        scratch_shapes=[pltpu.VMEM((tm, d), F32)],
        compiler_params=_cparams(("parallel", "arbitrary")),
        name="moe_final" if final_norm else "moe",
    )(h2, gates, x, g2, pw["wgu"], pw["wd"], pw["sgu"], pw["sd"], pw["fin"])


def _axial_angles(n_tok, rot_dim):
    half = rot_dim // 2
    freqs = ROPE_BASE ** (-jnp.arange(0, half, 2, dtype=F32) / half)
    t = jnp.arange(n_tok, dtype=jnp.int32)
    row = (t // GRID_W).astype(F32)
    col = (t % GRID_W).astype(F32)
    return row[:, None] * freqs, col[:, None] * freqs


def _rope_table(n_tok, rot_dim):
    ar, ac = _axial_angles(n_tok, rot_dim)
    cos = jnp.concatenate([jnp.cos(ar)] * 2 + [jnp.cos(ac)] * 2, axis=1)
    sin = jnp.concatenate([-jnp.sin(ar), jnp.sin(ar), -jnp.sin(ac), jnp.sin(ac)], axis=1)
    return cos, sin


def _rope_tables(n_tok):
    cg, sg = _rope_table(n_tok, GQA_DIM)
    cd, sd = _rope_table(n_tok, DIFF_QK_DIM)
    cm32, sm32 = _rope_table(n_tok, MLA_ROPE)
    ones = jnp.ones((n_tok, MLA_NOPE), F32)
    zeros = jnp.zeros((n_tok, MLA_NOPE), F32)
    pad = LANES - MLA_NOPE - MLA_ROPE
    cm = jnp.concatenate([ones, cm32, ones[:, :pad]], axis=1)
    sm = jnp.concatenate([zeros, sm32, zeros[:, :pad]], axis=1)
    return (jnp.tile(cg, (1, LANES // GQA_DIM)), jnp.tile(sg, (1, LANES // GQA_DIM)),
            jnp.tile(cd, (1, LANES // DIFF_QK_DIM)), jnp.tile(sd, (1, LANES // DIFF_QK_DIM)),
            cm, sm)


def _pack_layer(l, p):
    d = p["w_in"].shape[1]
    w = p["w_in"][l]
    o = np.cumsum([0, 256, 128, 128, 256, 256, 256, 256, 256, 256, MLA_Q_LORA, MLA_KV_LORA, MLA_ROPE])
    col = lambda i: w[:, int(o[i]):int(o[i + 1])]
    dup = lambda a: jnp.concatenate([a[:, :GQA_DIM], a[:, :GQA_DIM], a[:, GQA_DIM:], a[:, GQA_DIM:]], axis=1)
    z = lambda n: jnp.zeros((d, n), F32)
    w_in = jnp.concatenate(
        [col(0), dup(col(1)), dup(col(2)), col(3), col(4), col(5), col(6), col(7), col(8),
         col(9), z(SEG_W - MLA_Q_LORA), col(10),
         z(MLA_NOPE), col(11), z(LANES - MLA_NOPE - MLA_ROPE)], axis=1).astype(BF16)
    dq = MLA_NOPE + MLA_ROPE
    wqu = p["mla_w_q_up"][l].reshape(MLA_Q_LORA, MLA_HEADS, dq)
    wqu = jnp.pad(wqu, ((0, SEG_W - MLA_Q_LORA), (0, 0), (0, LANES - dq)))
    wkv = p["mla_w_kv_up"][l].reshape(MLA_KV_LORA, MLA_HEADS, MLA_NOPE + MLA_V)
    wkk = jnp.pad(wkv[:, :, :MLA_NOPE], ((0, 0), (0, 0), (0, LANES - MLA_NOPE)))
    e = p["moe_w_gate"].shape[1]
    rw = jnp.pad(p["router_w"], ((0, 0), (0, LANES - e)))
    rw_hi = rw.astype(BF16)
    rw_lo = (rw - rw_hi.astype(F32)).astype(BF16)
    return {
        "gmix": p["norm_mix"][l][None], "gffn": p["norm_ffn"][l][None],
        "w_in": w_in,
        "wq_up": wqu.reshape(SEG_W, MLA_HEADS * LANES).astype(BF16),
        "wkv_k": wkk.reshape(MLA_KV_LORA, MLA_HEADS * LANES).astype(BF16),
        "wkv_v": wkv[:, :, MLA_NOPE:].reshape(MLA_KV_LORA, MLA_HEADS * MLA_V).astype(BF16),
        "gq": (jnp.tile(p["gqa_q_norm"][l], GQA_HEADS) * (GQA_DIM ** -0.5 * LOG2E))[None],
        "gk": jnp.tile(p["gqa_k_norm"][l], GQA_HEADS)[None],
        "gcq": jnp.pad(p["mla_q_norm"][l], (0, SEG_W - MLA_Q_LORA))[None],
        "gckv": p["mla_kv_norm"][l][None],
        "w_out": p["w_out"][l].astype(BF16),
        "rw_hi": rw_hi, "rw_lo": rw_lo,
        "rb": jnp.broadcast_to(p["router_b"][:, None], (e, LANES)),
        "wgu": jnp.concatenate([p["moe_w_gate"][l], p["moe_w_up"][l]], axis=2).astype(BF16),
        "wd": p["moe_w_down"][l].astype(BF16),
        "sgu": jnp.concatenate([p["shared_w_gate"][l], p["shared_w_up"][l]], axis=1).astype(BF16),
        "sd": p["shared_w_down"][l].astype(BF16),
        "fin": p["final_norm"][None],
        "diff": (p["diff_lq1"][l][None], p["diff_lk1"][l][None], p["diff_lq2"][l][None],
                 p["diff_lk2"][l][None], p["diff_subln"][l][:, None]),
    }


def kernel(x, c, ctx, c_ctx, w_mod, b_mod, norm_mix, norm_ffn, w_in, w_out, gqa_q_norm, gqa_k_norm,
           na_rel_bias, diff_lq1, diff_lk1, diff_lq2, diff_lk2, diff_subln, mla_q_norm, mla_w_q_up,
           mla_kv_norm, mla_w_kv_up, router_w, router_b, moe_w_gate, moe_w_up, moe_w_down,
           shared_w_gate, shared_w_up, shared_w_down, final_norm):
    p = dict(w_in=w_in, norm_mix=norm_mix, norm_ffn=norm_ffn, w_out=w_out, gqa_q_norm=gqa_q_norm,
             gqa_k_norm=gqa_k_norm, diff_lq1=diff_lq1, diff_lk1=diff_lk1, diff_lq2=diff_lq2,
             diff_lk2=diff_lk2, diff_subln=diff_subln, mla_q_norm=mla_q_norm, mla_w_q_up=mla_w_q_up,
             mla_kv_norm=mla_kv_norm, mla_w_kv_up=mla_w_kv_up, router_w=router_w, router_b=router_b,
             moe_w_gate=moe_w_gate, moe_w_up=moe_w_up, moe_w_down=moe_w_down,
             shared_w_gate=shared_w_gate, shared_w_up=shared_w_up, shared_w_down=shared_w_down,
             final_norm=final_norm)
    b, s, d = x.shape
    nctx = ctx.shape[1]
    depth = w_mod.shape[0]
    rows = s // GRID_W
    assert s % TOK_TILE == 0 and s // NA_TILE >= NA_CHUNKS + NA_QCHUNKS
    assert nctx % LANES == 0 and (b * s) % MOE_TILE == 0 and s % MOE_TILE == 0

    mrows = -(-(b + 1) // SUBLANES) * SUBLANES
    cc = jnp.zeros((mrows, d), F32).at[:b].set(c).at[b].set(c_ctx)
    mod = _modulation(cc, w_mod, b_mod)
    tables = _rope_tables(s)
    lat_row = lambda bi: bi
    ctx_row = lambda bi: b
    tm_ctx = b * nctx if b * nctx <= MOE_TILE else nctx

    xc = ctx
    for l in range(depth):
        want_ctx = l < depth - 1
        lam_init = 0.8 - 0.6 * math.exp(-0.3 * l)
        pw = _pack_layer(l, p)
        m6 = mod[l].reshape(mrows, 6, 1, d)
        sh1, sc1, g1, sh2, sc2, g2 = (m6[:, k] for k in range(6))

        lat = _project(x, sh1, sc1, lat_row, pw, tables, TOK_TILE, LAT_OUTS)
        cx = _project(xc, sh1, sc1, ctx_row, pw, None, nctx, CTX_OUTS if want_ctx else CTX_KV_OUTS)

        def dense(g, kind, extras=()):
            return _flash_lat(lat[g + ".qT"], cx[g + ".k"], cx[g + ".vT"], lat[g + ".k"], lat[g + ".vT"],
                              kind, extras, lam_init)

        def ctx_only(g, kind, extras=()):
            return _flash_ctx(cx[g + ".qT"], cx[g + ".k"], cx[g + ".vT"], kind, extras, lam_init)

        a_lat = dense("A", "pair")
        bias = _na_bias_tiles(na_rel_bias[l], rows, NA_TILE)
        b_lat = _na_attention(lat["B.qT"], cx["B.k"], cx["B.vT"], lat["B.k"], lat["B.vT"], bias)
        c_lat = dense("C", "diff", pw["diff"])
        d_lat = dense("D", "mla")
        x1, h2, gates = _out_router((a_lat, b_lat, c_lat, d_lat), x, g1, sh2, sc2, lat_row, pw, TOK_TILE)
        tiles_per_batch = s // MOE_TILE
        x = _moe(h2.reshape(b * s, d), gates.reshape(b * s, LANES), x1.reshape(b * s, d), g2,
                 lambda i: i // tiles_per_batch, pw, MOE_TILE, not want_ctx).reshape(b, s, d)

        if want_ctx:
            a_c = ctx_only("A", "pair")
            b_c = ctx_only("B", "pair")
            c_c = ctx_only("C", "diff", pw["diff"])
            d_c = ctx_only("D", "mla")
            xc1, h2c, gates_c = _out_router((a_c, b_c, c_c, d_c), xc, g1, sh2, sc2, ctx_row, pw, nctx)
            xc = _moe(h2c.reshape(b * nctx, d), gates_c.reshape(b * nctx, LANES),
                      xc1.reshape(b * nctx, d), g2, lambda i: b, pw, tm_ctx, False).reshape(b, nctx, d)
    return x
```

```python
import functools
import math

import numpy as np
import jax
import jax.numpy as jnp
from jax import lax
from jax.experimental import pallas as pl
from jax.experimental.pallas import tpu as pltpu

F32 = jnp.float32
BF16 = jnp.bfloat16

GRID_W = 64
EPS = 1e-6
ROPE_BASE = 10000.0
GQA_HEADS, GQA_KV_HEADS, GQA_DIM = 4, 2, 64
NA_HEADS, NA_DIM, NA_WIN_ROWS, NA_WIN_COLS = 4, 64, 8, 16
DIFF_HEADS, DIFF_QK_DIM, DIFF_V_DIM = 4, 32, 64
MLA_HEADS, MLA_NOPE, MLA_ROPE, MLA_V, MLA_Q_LORA, MLA_KV_LORA = 4, 64, 32, 64, 192, 128
N_EXPERTS, MOE_GROUPS, D_EXPERT = 16, 4, 256
EXPERTS_PER_GROUP = N_EXPERTS // MOE_GROUPS

LANES = 128
SUBLANES = 8
VMEM_LIMIT = 56 * 1024 * 1024

TOK_TILE = 512
NA_TILE = 256
MOE_TILE = 1024
MOD_TILE = 1024
PROJ_SUBTILES = 2
OUT_SUBTILES = 2
NEG = -1e30
LOG2E = math.log2(math.e)

SEG_W = 256
OFF_AQ, OFF_AK, OFF_AV = 0, 256, 512
OFF_BQ, OFF_BK, OFF_BV = 768, 1024, 1280
OFF_CQ, OFF_CK, OFF_CV = 1536, 1792, 2048
OFF_DQ, OFF_DKV, OFF_DPE = 2304, 2560, 2688
IN_PACKED = 2816


def _cparams(sem):
    return pltpu.CompilerParams(dimension_semantics=sem, vmem_limit_bytes=VMEM_LIMIT)


def _lane_iota(shape):
    return lax.broadcasted_iota(jnp.int32, shape, 1)


def _lane_group(shape, width):
    return lax.shift_right_logical(_lane_iota(shape), int(math.log2(width)))


def _sigmoid(x):
    return 1.0 / (1.0 + jnp.exp(-x))


def _mod_kernel(c_ref, w_ref, b_ref, o_ref):
    c = c_ref[...]
    s = c * _sigmoid(c)
    o_ref[0] = jnp.dot(s, w_ref[0], preferred_element_type=F32,
                       precision=lax.Precision.HIGHEST) + b_ref[0]


def _modulation(cc, w_mod, b_mod):
    depth, d, n = w_mod.shape
    rows = cc.shape[0]
    return pl.pallas_call(
        _mod_kernel,
        grid=(depth, n // MOD_TILE),
        in_specs=[
            pl.BlockSpec((rows, d), lambda l, j: (0, 0)),
            pl.BlockSpec((1, d, MOD_TILE), lambda l, j: (l, 0, j)),
            pl.BlockSpec((1, 1, MOD_TILE), lambda l, j: (l, 0, j)),
        ],
        out_specs=pl.BlockSpec((1, rows, MOD_TILE), lambda l, j: (l, 0, j)),
        out_shape=jax.ShapeDtypeStruct((depth, rows, n), F32),
        compiler_params=_cparams(("parallel", "parallel")),
        name="modulation",
    )(cc, w_mod, b_mod.reshape(depth, 1, n))


def _swap_halves(x, half):
    w = x.shape[1]
    lane = _lane_iota(x.shape)
    fwd = pltpu.roll(x, half, 1)
    bwd = pltpu.roll(x, w - half, 1)
    return jnp.where((lane & (2 * half - 1)) < half, bwd, fwd)


def _rope(x, cos, sin, half):
    reps = x.shape[1] // cos.shape[1]
    c = jnp.concatenate([cos] * reps, axis=1) if reps > 1 else cos
    s = jnp.concatenate([sin] * reps, axis=1) if reps > 1 else sin
    return x * c + _swap_halves(x, half) * s


def _group_rms(x, group, count):
    gid = _lane_group(x.shape, group)
    x2 = x * x
    inv = jnp.zeros_like(x)
    for g in range(x.shape[1] // group):
        msk = gid == g
        ms = jnp.sum(jnp.where(msk, x2, 0.0), axis=1, keepdims=True) * (1.0 / count)
        inv = jnp.where(msk, lax.rsqrt(ms + EPS), inv)
    return x * inv


LAT_OUTS = ("A.qT", "A.k", "A.vT", "B.qT", "B.k", "B.vT", "C.qT", "C.k", "C.vT", "D.qT", "D.k", "D.vT")
CTX_KV_OUTS = ("A.k", "A.vT", "B.k", "B.vT", "C.k", "C.vT", "D.k", "D.vT")
CTX_OUTS = CTX_KV_OUTS + ("A.qT", "B.qT", "C.qT", "D.qT")
GROUP_W = {"A": SEG_W, "B": SEG_W, "C": SEG_W, "D": 2 * SEG_W}


def _proj_kernel(*refs, rope, outs):
    (x_ref, sh_ref, sc_ref, gmix_ref, w_ref, wqu_ref, wkk_ref, wkv_ref,
     gq_ref, gk_ref, gcq_ref, gckv_ref) = refs[:12]
    pos = 12
    tabs = refs[pos:pos + 6] if rope else ()
    pos += len(tabs)
    out = dict(zip(outs, refs[pos:pos + len(outs)]))
    tm = x_ref.shape[1]
    parts = PROJ_SUBTILES if tm % (PROJ_SUBTILES * 2 * LANES) == 0 else 1
    sub = tm // parts
    for h in range(parts):
        _proj_subtile(slice(h * sub, (h + 1) * sub), x_ref, sh_ref, sc_ref, gmix_ref, w_ref, wqu_ref,
                      wkk_ref, wkv_ref, gq_ref, gk_ref, gcq_ref, gckv_ref, tabs, out, rope)


def _proj_subtile(rows, x_ref, sh_ref, sc_ref, gmix_ref, w_ref, wqu_ref, wkk_ref, wkv_ref,
                  gq_ref, gk_ref, gcq_ref, gckv_ref, tabs, out, rope):
    r0, sub = rows.start, rows.stop - rows.start
    if rope:
        cg, sg, cd, sd, cm, sm = (t[rows, :] for t in tabs)

    def want(g, t):
        return (g + "." + t) in out or (g + "." + t + "T") in out

    def put(g, t, val):
        if g + "." + t in out:
            out[g + "." + t][0, rows, :] = val.astype(BF16)
        if g + "." + t + "T" in out:
            ref = out[g + "." + t + "T"]
            vt = val.T.astype(BF16)
            cw = ref.shape[3]
            if cw >= sub:
                ref[0, r0 // cw, :, r0 % cw:r0 % cw + sub] = vt
            else:
                for c in range(sub // cw):
                    ref[0, r0 // cw + c] = vt[:, c * cw:(c + 1) * cw]

    x = x_ref[0, rows, :]
    ms = jnp.mean(x * x, axis=-1, keepdims=True)
    h = x * lax.rsqrt(ms + EPS) * gmix_ref[...]
    h = h * (1.0 + sc_ref[0]) + sh_ref[0]
    hb = h.astype(BF16)

    def seg(off, width=SEG_W):
        return jnp.dot(hb, w_ref[:, off:off + width], preferred_element_type=F32)

    if want("A", "q"):
        q = _group_rms(seg(OFF_AQ), GQA_DIM, GQA_DIM) * gq_ref[...]
        if rope:
            q = _rope(q, cg, sg, GQA_DIM // 4)
        put("A", "q", q)
    k = _group_rms(seg(OFF_AK), GQA_DIM, GQA_DIM) * gk_ref[...]
    if rope:
        k = _rope(k, cg, sg, GQA_DIM // 4)
    put("A", "k", k)
    put("A", "v", seg(OFF_AV))

    if want("B", "q"):
        put("B", "q", seg(OFF_BQ) * (NA_DIM ** -0.5 * LOG2E))
    put("B", "k", seg(OFF_BK))
    put("B", "v", seg(OFF_BV))

    if want("C", "q"):
        q = seg(OFF_CQ) * (DIFF_QK_DIM ** -0.5 * LOG2E)
        if rope:
            q = _rope(q, cd, sd, DIFF_QK_DIM // 4)
        put("C", "q", q)
    k = seg(OFF_CK)
    if rope:
        k = _rope(k, cd, sd, DIFF_QK_DIM // 4)
    put("C", "k", k)
    put("C", "v", seg(OFF_CV))

    if want("D", "q"):
        cq = _group_rms(seg(OFF_DQ), SEG_W, MLA_Q_LORA) * gcq_ref[...]
        q = jnp.dot(cq.astype(BF16), wqu_ref[...], preferred_element_type=F32)
        q = q * ((MLA_NOPE + MLA_ROPE) ** -0.5 * LOG2E)
        if rope:
            q = _rope(q, cm, sm, MLA_ROPE // 4)
        put("D", "q", q)
    ckv = _group_rms(seg(OFF_DKV, LANES), LANES, MLA_KV_LORA) * gckv_ref[...]
    ckvb = ckv.astype(BF16)
    k = jnp.dot(ckvb, wkk_ref[...], preferred_element_type=F32)
    kpe = seg(OFF_DPE, LANES)
    if rope:
        kpe = _rope(kpe, cm, sm, MLA_ROPE // 4)
    put("D", "k", k + jnp.concatenate([kpe] * MLA_HEADS, axis=1))
    put("D", "v", jnp.dot(ckvb, wkv_ref[...], preferred_element_type=F32))


def _project(x, sh, sc, row_of_batch, pw, tables, tm, outs):
    b, n, d = x.shape
    nt = n // tm
    rope = tables is not None
    full = lambda a: pl.BlockSpec(a.shape, lambda bi, i: (0,) * a.ndim)
    mod_spec = pl.BlockSpec((1, 1, d), lambda bi, i: (row_of_batch(bi), 0, 0))
    ins = [x, sh, sc, pw["gmix"], pw["w_in"], pw["wq_up"], pw["wkv_k"], pw["wkv_v"],
           pw["gq"], pw["gk"], pw["gcq"], pw["gckv"]]
    in_specs = [pl.BlockSpec((1, tm, d), lambda bi, i: (bi, i, 0)), mod_spec, mod_spec]
    in_specs += [full(a) for a in ins[3:]]
    if rope:
        ins += list(tables)
        in_specs += [pl.BlockSpec((tm, LANES), lambda bi, i: (i, 0)) for _ in tables]
    shapes, specs = [], []
    for name in outs:
        g, t = name.split(".")
        w = SEG_W if t[0] == "v" else GROUP_W[g]
        if t.endswith("T"):
            cw = min(tm, NA_TILE) if g == "B" else tm
            shapes.append(jax.ShapeDtypeStruct((b, n // cw, w, cw), BF16))
            specs.append(pl.BlockSpec((1, tm // cw, w, cw), lambda bi, i: (bi, i, 0, 0)))
        else:
            shapes.append(jax.ShapeDtypeStruct((b, n, w), BF16))
            specs.append(pl.BlockSpec((1, tm, w), lambda bi, i: (bi, i, 0)))
    res = pl.pallas_call(
        functools.partial(_proj_kernel, rope=rope, outs=tuple(outs)),
        grid=(b, nt),
        in_specs=in_specs,
        out_specs=specs,
        out_shape=shapes,
        compiler_params=_cparams(("parallel", "parallel")),
        name="proj_rope" if rope else "proj_ctx",
    )(*ins)
    return dict(zip(outs, res))


HEAD_V = 64


def _tile_lanes(x, width):
    reps = width // x.shape[1]
    return jnp.tile(x, (1, reps)) if reps > 1 else x


def with_ones(vt):
    rows = lax.broadcasted_iota(jnp.int32, (LANES - HEAD_V, vt.shape[1]), 0)
    ones = jnp.where(rows == 0, 1.0, 0.0).astype(BF16)
    return jnp.concatenate([vt, ones], axis=0)


def _map_slices(kind):
    def k_lanes(m):
        return slice(m * LANES, (m + 1) * LANES) if kind == "mla" else slice(None)

    def v_rows(m):
        hv = m // 2 if kind == "diff" else m
        return slice(hv * HEAD_V, (hv + 1) * HEAD_V)

    return k_lanes, v_rows


def _query_slabs(qt_ref, kind, nm):
    qv = jnp.concatenate([qt_ref[0, c] for c in range(qt_ref.shape[1])], axis=1)
    if kind == "mla":
        return [qv[m * LANES:(m + 1) * LANES, :] for m in range(nm)]
    rows = lax.broadcasted_iota(jnp.int32, qv.shape, 0)
    grp = lax.shift_right_logical(rows, int(math.log2(LANES // nm)))
    return [jnp.where(grp == m, qv, jnp.zeros_like(qv)) for m in range(nm)]


def _finish_heads(accs, kind, lam_init, diff_refs):
    outs = [a[0:HEAD_V] / a[HEAD_V:HEAD_V + 1] for a in accs]
    if kind == "diff":
        lq1_ref, lk1_ref, lq2_ref, lk2_ref, sub_ref = diff_refs
        lam = (jnp.exp(jnp.sum(lq1_ref[...] * lk1_ref[...], axis=1, keepdims=True))
               - jnp.exp(jnp.sum(lq2_ref[...] * lk2_ref[...], axis=1, keepdims=True))
               + lam_init)
        heads = []
        for hd in range(2):
            dlt = outs[2 * hd] - lam * outs[2 * hd + 1]
            ms2 = jnp.mean(dlt * dlt, axis=0, keepdims=True)
            heads.append(dlt * lax.rsqrt(ms2 + EPS) * sub_ref[...] * (1.0 - lam_init))
        outs = heads
    return jnp.concatenate(outs, axis=0).T


def _flash_ctx_kernel(*refs, kind, lam_init):
    qt_ref, kc_ref, vct_ref = refs[:3]
    nd = 5 if kind == "diff" else 0
    diff_refs, o_ref = refs[3:3 + nd], refs[3 + nd]
    nm = 4 if kind == "diff" else 2
    k_lanes, v_rows = _map_slices(kind)
    accs = []
    for m, qm in enumerate(_query_slabs(qt_ref, kind, nm)):
        st = jnp.dot(kc_ref[0, :, k_lanes(m)], qm, preferred_element_type=F32)
        p = jnp.exp2(st - jnp.max(st, axis=0, keepdims=True)).astype(BF16)
        accs.append(jnp.dot(with_ones(vct_ref[0, 0, v_rows(m), :]), p, preferred_element_type=F32))
    o_ref[0] = _finish_heads(accs, kind, lam_init, diff_refs).astype(o_ref.dtype)


def _flash_lat_kernel(*refs, kind, lam_init):
    qt_ref, kc_ref, vct_ref, k_ref, vt_ref = refs[:5]
    nd = 5 if kind == "diff" else 0
    diff_refs, o_ref = refs[5:5 + nd], refs[5 + nd]
    qs_e, acc_e, s_e, qs_l, acc_l, s_l, m_ref, mc_ref = refs[6 + nd:14 + nd]
    early = (qs_e, acc_e, s_e, 0, False)
    late = (qs_l, acc_l, s_l, vt_ref.shape[1] // 2, True)

    nm = qs_e.shape[0]
    nk, tk = vt_ref.shape[1], vt_ref.shape[3]
    half = nk // 2
    iters = half // 2
    g = pl.program_id(2)
    nq = pl.num_programs(2) - 1
    k_lanes, v_rows = _map_slices(kind)

    def produce(tile, slot, j, m):
        qs_ref, _, s_ref = tile[:3]
        rows = pl.ds(pl.multiple_of(j * tk, tk), tk)
        st = jnp.dot(k_ref[0, rows, k_lanes(m)], qs_ref[m], preferred_element_type=F32)
        s_ref[slot, m] = st
        return jnp.max(st, axis=0, keepdims=True)

    def consume(tile, slot, j, m, m_prev, m_chunk):
        _, acc_ref, s_ref = tile[:3]
        m_new = jnp.maximum(m_prev, m_chunk)
        alpha = jnp.exp2(m_prev - m_new)
        p = jnp.exp2(s_ref[slot, m] - m_new).astype(BF16)
        acc_ref[m] = alpha * acc_ref[m] + jnp.dot(
            with_ones(vt_ref[0, j, v_rows(m), :]), p, preferred_element_type=F32)
        return m_new

    def iteration(i, carries, tiles, last):
        maps = range(nm)
        j0s = [tile[3] + 2 * i for tile in tiles]
        c1 = [[None] * nm for _ in tiles]
        m1 = [[None] * nm for _ in tiles]
        c2 = [[None] * nm for _ in tiles]
        m2 = [[None] * nm for _ in tiles]
        for t, (tile, j0, c) in enumerate(zip(tiles, j0s, carries)):
            for m in maps:
                c1[t][m] = produce(tile, 1, j0 + 1, m)
                m1[t][m] = consume(tile, 0, j0, m, c[0][m], c[1][m])
        for t, (tile, j0) in enumerate(zip(tiles, j0s)):
            for m in maps:
                c2[t][m] = c1[t][m] if (last and tile[4]) else produce(tile, 0, j0 + 2, m)
                m2[t][m] = consume(tile, 1, j0 + 1, m, m1[t][m], c1[t][m])
        return tuple((tuple(a), tuple(b)) for a, b in zip(m2, c2))

    def run(carries, tiles):
        carries = iteration(0, carries, tiles, iters == 1)
        if iters > 2:
            carries = lax.fori_loop(1, iters - 1, lambda i, c: iteration(i, c, tiles, False), carries)
        if iters > 1:
            carries = iteration(iters - 1, carries, tiles, True)
        return carries

    def start_early():
        for m, qm in enumerate(_query_slabs(qt_ref, kind, nm)):
            qs_e[m] = qm
        sts = [jnp.dot(kc_ref[0, :, k_lanes(m)], qs_e[m], preferred_element_type=F32)
               for m in range(nm)]
        ms = tuple(jnp.max(st, axis=0, keepdims=True) for st in sts)
        first = tuple(produce(early, 0, 0, m) for m in range(nm))

        def init_from_context():
            for m in range(nm):
                p = jnp.exp2(sts[m] - ms[m]).astype(BF16)
                acc_e[m] = jnp.dot(with_ones(vct_ref[0, 0, v_rows(m), :]), p,
                                   preferred_element_type=F32)

        return (ms, first), init_from_context

    def load_late():
        return (tuple(m_ref[m] for m in range(nm)), tuple(mc_ref[m] for m in range(nm)))

    def hand_over(carry):
        qs_l[...] = qs_e[...]
        acc_l[...] = acc_e[...]
        s_l[0] = s_e[0]
        for m in range(nm):
            m_ref[m] = carry[0][m]
            mc_ref[m] = carry[1][m]

    def finish_late():
        accs = [acc_l[m] for m in range(nm)]
        o_ref[0] = _finish_heads(accs, kind, lam_init, diff_refs).astype(o_ref.dtype)

    @pl.when(g == 0)
    def _():
        carry, init_from_context = start_early()
        init_from_context()
        hand_over(run((carry,), (early,))[0])

    @pl.when((g > 0) & (g < nq))
    def _():
        carry_l = load_late()
        carry_e, init_from_context = start_early()
        init_from_context()
        carry_l, carry_e = run((carry_l, carry_e), (late, early))
        finish_late()
        hand_over(carry_e)

    @pl.when(g == nq)
    def _():
        run((load_late(),), (late,))
        finish_late()


def _flash_specs(qt, kc, vct, kind, extras):
    b, nqc, wtot, cw = qt.shape
    wq = 2 * LANES if kind == "mla" else LANES
    c = kc.shape[1]
    extra_specs = [pl.BlockSpec(e.shape, lambda bi, p, i: (0, 0)) for e in extras]
    kv_specs = [pl.BlockSpec((1, c, wq), lambda bi, p, i: (bi, 0, p)),
                pl.BlockSpec((1, 1, LANES, c), lambda bi, p, i: (bi, 0, p, 0))]
    return b, nqc, wtot // wq, wq, cw, kv_specs, extra_specs


def _flash_ctx(qt, kc, vct, kind, extras=(), lam_init=0.0):
    b, nqc, npair, wq, cw, kv_specs, extra_specs = _flash_specs(qt, kc, vct, kind, extras)
    return pl.pallas_call(
        functools.partial(_flash_ctx_kernel, kind=kind, lam_init=lam_init),
        grid=(b, npair, nqc),
        in_specs=[pl.BlockSpec((1, 1, wq, cw), lambda bi, p, i: (bi, i, p, 0))] + kv_specs + extra_specs,
        out_specs=pl.BlockSpec((1, cw, LANES), lambda bi, p, i: (bi, i, p)),
        out_shape=jax.ShapeDtypeStruct((b, nqc * cw, npair * LANES), BF16),
        compiler_params=_cparams(("parallel", "parallel", "parallel")),
        name="flash_" + kind + "_ctx",
    )(qt, kc, vct, *extras)


def _flash_lat(qt, kc, vct, k, vt, kind, extras=(), lam_init=0.0):
    b, nq, npair, wq, tq, kv_specs, extra_specs = _flash_specs(qt, kc, vct, kind, extras)
    nm = 4 if kind == "diff" else 2
    nk, tk = vt.shape[1], vt.shape[3]
    assert nk % 4 == 0, "two phases of key chunks, processed in pairs"
    return pl.pallas_call(
        functools.partial(_flash_lat_kernel, kind=kind, lam_init=lam_init),
        grid=(b, npair, nq + 1),
        in_specs=[pl.BlockSpec((1, 1, wq, tq), lambda bi, p, g: (bi, jnp.minimum(g, nq - 1), p, 0))]
        + kv_specs
        + [pl.BlockSpec((1, k.shape[1], wq), lambda bi, p, g: (bi, 0, p)),
           pl.BlockSpec((1, nk, LANES, tk), lambda bi, p, g: (bi, 0, p, 0))]
        + extra_specs,
        out_specs=pl.BlockSpec((1, tq, LANES), lambda bi, p, g: (bi, jnp.maximum(g - 1, 0), p)),
        out_shape=jax.ShapeDtypeStruct((b, nq * tq, npair * LANES), BF16),
        scratch_shapes=2 * [
            pltpu.VMEM((nm, LANES, tq), BF16),
            pltpu.VMEM((nm, LANES, tq), F32),
            pltpu.VMEM((2, nm, tk, tq), F32),
        ] + 2 * [pltpu.VMEM((nm, 1, tq), F32)],
        compiler_params=_cparams(("parallel", "parallel", "arbitrary")),
        name="flash_" + kind,
    )(qt, kc, vct, k, vt, *extras)


NA_QCHUNKS = 2
NA_CHUNKS = NA_QCHUNKS + 2


def _na_kernel(qt_ref, k_ref, vt_ref, kc_ref, vct_ref, bias_ref, o_ref):
    nkc, tk = vt_ref.shape[1], vt_ref.shape[3]
    i = pl.program_id(2)
    c0 = jnp.clip(i * NA_QCHUNKS - 1, 0, nkc - NA_CHUNKS)
    qv = jnp.concatenate([qt_ref[0, c] for c in range(NA_QCHUNKS)], axis=1)
    rows = lax.broadcasted_iota(jnp.int32, qv.shape, 0)
    scores = []
    for m in range(2):
        own = lax.shift_right_logical(rows, int(math.log2(LANES // 2))) == m
        qm = jnp.where(own, qv, jnp.zeros_like(qv))
        ss = []
        for j in range(NA_CHUNKS):
            kj = k_ref[0, pl.ds(pl.multiple_of((c0 + j) * tk, tk), tk), :]
            ss.append(jnp.dot(kj, qm, preferred_element_type=F32) + bias_ref[0, m, j])
        scores.append((ss, jnp.dot(kc_ref[0], qm, preferred_element_type=F32)))
    outs = []
    for m, (ss, sc) in enumerate(scores):
        v_rows = slice(m * HEAD_V, (m + 1) * HEAD_V)
        mx = jnp.max(sc, axis=0, keepdims=True)
        for st in ss:
            mx = jnp.maximum(mx, jnp.max(st, axis=0, keepdims=True))
        acc = jnp.dot(with_ones(vct_ref[0, 0, v_rows, :]), jnp.exp2(sc - mx).astype(BF16),
                      preferred_element_type=F32)
        for j, st in enumerate(ss):
            acc = acc + jnp.dot(with_ones(vt_ref[0, c0 + j, v_rows, :]),
                                jnp.exp2(st - mx).astype(BF16), preferred_element_type=F32)
        outs.append(acc[0:HEAD_V] / acc[HEAD_V:HEAD_V + 1])
    o_ref[0] = jnp.concatenate(outs, axis=0).T.astype(o_ref.dtype)


def _na_bias_tiles(rel_bias, rows, tk):
    qrows = NA_QCHUNKS * tk // GRID_W
    nblk = rows // qrows
    nk = rows * GRID_W // tk
    krows = NA_CHUNKS * tk // GRID_W
    kh = min(NA_WIN_ROWS, rows)
    kw = NA_WIN_COLS
    blocks = np.array([0, 1, nblk - 1])
    c0 = np.clip(blocks * NA_QCHUNKS - 1, 0, nk - NA_CHUNKS)
    r = blocks[:, None] * qrows + np.arange(qrows)[None, :]
    kr = c0[:, None] * (tk // GRID_W) + np.arange(krows)[None, :]
    r0 = np.clip(r - kh // 2, 0, rows - kh)
    dr = kr[:, None, :] - r[:, :, None]
    valid_r = (kr[:, None, :] >= r0[:, :, None]) & (kr[:, None, :] < r0[:, :, None] + kh)
    cols = np.arange(GRID_W)
    cstart = np.clip(cols - kw // 2, 0, GRID_W - kw)
    dc = cols[None, :] - cols[:, None]
    valid_c = (cols[None, :] >= cstart[:, None]) & (cols[None, :] < cstart[:, None] + kw)
    ndc = 2 * NA_WIN_COLS - 1
    onehot = ((dc[None] + NA_WIN_COLS - 1) == np.arange(ndc)[:, None, None]) & valid_c[None]
    dr_idx = np.clip(dr + NA_WIN_ROWS - 1, 0, 2 * NA_WIN_ROWS - 2)
    g1 = rel_bias[:, dr_idx, :]
    tiles = jnp.einsum("htabd,dcx->thbxac", g1, jnp.asarray(onehot, F32),
                       precision=lax.Precision.HIGHEST)
    valid = valid_r.transpose(0, 2, 1)[:, None, :, None, :, None] & valid_c.T[None, None, None, :, None, :]
    tiles = jnp.where(jnp.asarray(valid), tiles * LOG2E, NEG)
    return tiles.reshape(3, rel_bias.shape[0], NA_CHUNKS, tk, qrows * GRID_W)


def _na_attention(qt, kc, vct, k, vt, bias):
    b, nqc, wtot, cw = qt.shape
    nblk, tq = nqc // NA_QCHUNKS, NA_QCHUNKS * cw
    npair = wtot // LANES
    nkc, tk = vt.shape[1], vt.shape[3]
    c = kc.shape[1]
    n = k.shape[1]

    def bias_map(bi, p, i):
        return (jnp.where(i == 0, 0, jnp.where(i == nblk - 1, 2, 1)), p, 0, 0, 0)

    return pl.pallas_call(
        _na_kernel,
        grid=(b, npair, nblk),
        in_specs=[
            pl.BlockSpec((1, NA_QCHUNKS, LANES, cw), lambda bi, p, i: (bi, i, p, 0)),
            pl.BlockSpec((1, n, LANES), lambda bi, p, i: (bi, 0, p)),
            pl.BlockSpec((1, nkc, LANES, tk), lambda bi, p, i: (bi, 0, p, 0)),
            pl.BlockSpec((1, c, LANES), lambda bi, p, i: (bi, 0, p)),
            pl.BlockSpec((1, 1, LANES, c), lambda bi, p, i: (bi, 0, p, 0)),
            pl.BlockSpec((1, 2, NA_CHUNKS, tk, tq), bias_map),
        ],
        out_specs=pl.BlockSpec((1, tq, LANES), lambda bi, p, i: (bi, i, p)),
        out_shape=jax.ShapeDtypeStruct((b, nblk * tq, wtot), BF16),
        compiler_params=_cparams(("parallel", "parallel", "parallel")),
        name="na_attention",
    )(qt, k, vt, kc, vct, bias)


def _first_index_of_max(vals, row, big):
    mx = jnp.max(vals, axis=0, keepdims=True)
    idx = jnp.min(jnp.where(vals == mx, row, big), axis=0, keepdims=True)
    return mx, idx


def _router_gates(scores_t, sel_t):
    irow = lax.broadcasted_iota(jnp.int32, sel_t.shape, 0)
    row = irow.astype(F32)
    grp = lax.shift_right_logical(irow, int(math.log2(EXPERTS_PER_GROUP))).astype(F32)
    big = float(N_EXPERTS)
    best = None
    for g in range(MOE_GROUPS):
        vals = jnp.where(grp == float(g), sel_t, -jnp.inf)
        m1, i1 = _first_index_of_max(vals, row, big)
        m2 = jnp.max(jnp.where(row == i1, -jnp.inf, vals), axis=0, keepdims=True)
        gs = m1 + m2
        if best is None:
            best, bi = gs, jnp.zeros(gs.shape, F32)
        else:
            better = gs > best
            bi = jnp.where(better, float(g), bi)
            best = jnp.where(better, gs, best)
    msel = jnp.where(grp == bi, sel_t, -jnp.inf)
    _, i1 = _first_index_of_max(msel, row, big)
    msel2 = jnp.where(row == i1, -jnp.inf, msel)
    _, i2 = _first_index_of_max(msel2, row, big)
    w = jnp.where((row == i1) | (row == i2), scores_t, 0.0)
    return w / jnp.sum(w, axis=0, keepdims=True)


def _out_kernel(a_ref, b_ref, c_ref, d_ref, x_ref, g1_ref, sh_ref, sc_ref, gffn_ref,
                w_ref, rwh_ref, rwl_ref, rb_ref, xo_ref, h2_ref, gate_ref):
    tm = x_ref.shape[1]
    parts = OUT_SUBTILES if tm % (OUT_SUBTILES * LANES) == 0 else 1
    sub = tm // parts
    ys = []
    for h in range(parts):
        rows = slice(h * sub, (h + 1) * sub)
        y = None
        for gi, r in enumerate((a_ref, b_ref, c_ref, d_ref)):
            part = jnp.dot(r[0, rows, :], w_ref[gi * SEG_W:(gi + 1) * SEG_W, :],
                           preferred_element_type=F32)
            y = part if y is None else y + part
        ys.append(y)
    for h in range(parts):
        rows = slice(h * sub, (h + 1) * sub)
        xn = x_ref[0, rows, :] + g1_ref[0] * ys[h]
        xo_ref[0, rows, :] = xn
        ms = jnp.mean(xn * xn, axis=-1, keepdims=True)
        h2 = xn * lax.rsqrt(ms + EPS) * gffn_ref[...]
        h2 = h2 * (1.0 + sc_ref[0]) + sh_ref[0]
        hi = h2.astype(BF16)
        h2_ref[0, rows, :] = hi
        lo = (h2 - hi.astype(F32)).astype(BF16)
        logits = (jnp.dot(hi, rwh_ref[...], preferred_element_type=F32)
                  + jnp.dot(lo, rwh_ref[...], preferred_element_type=F32)
                  + jnp.dot(hi, rwl_ref[...], preferred_element_type=F32))
        scores_t = _sigmoid(logits).T[:N_EXPERTS]
        sel_t = scores_t + _tile_lanes(rb_ref[...], sub)
        gates_t = _router_gates(scores_t, sel_t)
        pad = jnp.zeros((LANES - N_EXPERTS, sub), F32)
        gate_ref[0, rows, :] = jnp.concatenate([gates_t, pad], axis=0).T


def _out_router(parts, x, g1, sh2, sc2, row_of_batch, pw, tm):
    b, n, d = x.shape
    full = lambda a: pl.BlockSpec(a.shape, lambda bi, i: (0,) * a.ndim)
    tok = lambda w: pl.BlockSpec((1, tm, w), lambda bi, i: (bi, i, 0))
    mod_spec = pl.BlockSpec((1, 1, d), lambda bi, i: (row_of_batch(bi), 0, 0))
    consts = [pw["gffn"], pw["w_out"], pw["rw_hi"], pw["rw_lo"], pw["rb"]]
    return pl.pallas_call(
        _out_kernel,
        grid=(b, n // tm),
        in_specs=[tok(SEG_W)] * 4 + [tok(d), mod_spec, mod_spec, mod_spec] + [full(a) for a in consts],
        out_specs=[tok(d), tok(d), tok(LANES)],
        out_shape=[jax.ShapeDtypeStruct((b, n, d), F32), jax.ShapeDtypeStruct((b, n, d), BF16),
                   jax.ShapeDtypeStruct((b, n, LANES), F32)],
        compiler_params=_cparams(("parallel", "parallel")),
        name="out_router",
    )(*parts, x, g1, sh2, sc2, *consts)


MOE_EXPERTS_PER_STEP = 4


def _swiglu_act(gu, scale=None):
    de = gu.shape[1] // 2
    g, u = gu[:, :de], gu[:, de:]
    act = (g * _sigmoid(g)) * u
    return (act if scale is None else act * scale).astype(BF16)


def _moe_kernel(h_ref, gate_ref, x_ref, g2_ref, wgu_ref, wd_ref, sgu_ref, sd_ref, fin_ref,
                o_ref, acc_ref, *, final_norm):
    step = pl.program_id(1)
    k = wgu_ref.shape[0]
    hb = h_ref[...]
    gates = gate_ref[...]
    lane = _lane_iota(gates.shape)
    acts = []
    for i in range(k):
        gu = jnp.dot(hb, wgu_ref[i], preferred_element_type=F32)
        col = jnp.sum(jnp.where(lane == step * k + i, gates, 0.0), axis=1, keepdims=True)
        acts.append(_swiglu_act(gu, col))
    wd = wd_ref[...]
    contrib = jnp.dot(jnp.concatenate(acts, axis=1), wd.reshape(k * wd.shape[1], wd.shape[2]),
                      preferred_element_type=F32)

    @pl.when(step == 0)
    def _():
        shared = _swiglu_act(jnp.dot(hb, sgu_ref[...], preferred_element_type=F32))
        acc_ref[...] = contrib + jnp.dot(shared, sd_ref[...], preferred_element_type=F32)

    @pl.when(step > 0)
    def _():
        acc_ref[...] += contrib

    @pl.when(step == pl.num_programs(1) - 1)
    def _():
        y = x_ref[...] + g2_ref[0] * acc_ref[...]
        if final_norm:
            ms = jnp.mean(y * y, axis=-1, keepdims=True)
            y = y * lax.rsqrt(ms + EPS) * fin_ref[...]
        o_ref[...] = y


def _moe(h2, gates, x, g2, row_of_tile, pw, tm, final_norm):
    t, d = x.shape
    ne, _, de2 = pw["wgu"].shape
    k = MOE_EXPERTS_PER_STEP
    full = lambda a: pl.BlockSpec(a.shape, lambda i, e: (0,) * a.ndim)
    return pl.pallas_call(
        functools.partial(_moe_kernel, final_norm=final_norm),
        grid=(t // tm, ne // k),
        in_specs=[
            pl.BlockSpec((tm, d), lambda i, e: (i, 0)),
            pl.BlockSpec((tm, LANES), lambda i, e: (i, 0)),
            pl.BlockSpec((tm, d), lambda i, e: (i, 0)),
            pl.BlockSpec((1, 1, d), lambda i, e: (row_of_tile(i), 0, 0)),
            pl.BlockSpec((k, d, de2), lambda i, e: (e, 0, 0)),
            pl.BlockSpec((k, de2 // 2, d), lambda i, e: (e, 0, 0)),
            full(pw["sgu"]), full(pw["sd"]), full(pw["fin"]),
        ],
        out_specs=pl.BlockSpec((tm, d), lambda i, e: (i, 0)),
        out_shape=jax.ShapeDtypeStruct((t, d), F32),
        scratch_shapes=[pltpu.VMEM((tm, d), F32)],
        compiler_params=_cparams(("parallel", "arbitrary")),
        name="moe_final" if final_norm else "moe",
    )(h2, gates, x, g2, pw["wgu"], pw["wd"], pw["sgu"], pw["sd"], pw["fin"])


def _axial_angles(n_tok, rot_dim):
    half = rot_dim // 2
    freqs = ROPE_BASE ** (-jnp.arange(0, half, 2, dtype=F32) / half)
    t = jnp.arange(n_tok, dtype=jnp.int32)
    row = (t // GRID_W).astype(F32)
    col = (t % GRID_W).astype(F32)
    return row[:, None] * freqs, col[:, None] * freqs


def _rope_table(n_tok, rot_dim):
    ar, ac = _axial_angles(n_tok, rot_dim)
    cos = jnp.concatenate([jnp.cos(ar)] * 2 + [jnp.cos(ac)] * 2, axis=1)
    sin = jnp.concatenate([-jnp.sin(ar), jnp.sin(ar), -jnp.sin(ac), jnp.sin(ac)], axis=1)
    return cos, sin


def _rope_tables(n_tok):
    cg, sg = _rope_table(n_tok, GQA_DIM)
    cd, sd = _rope_table(n_tok, DIFF_QK_DIM)
    cm32, sm32 = _rope_table(n_tok, MLA_ROPE)
    ones = jnp.ones((n_tok, MLA_NOPE), F32)
    zeros = jnp.zeros((n_tok, MLA_NOPE), F32)
    pad = LANES - MLA_NOPE - MLA_ROPE
    cm = jnp.concatenate([ones, cm32, ones[:, :pad]], axis=1)
    sm = jnp.concatenate([zeros, sm32, zeros[:, :pad]], axis=1)
    return (jnp.tile(cg, (1, LANES // GQA_DIM)), jnp.tile(sg, (1, LANES // GQA_DIM)),
            jnp.tile(cd, (1, LANES // DIFF_QK_DIM)), jnp.tile(sd, (1, LANES // DIFF_QK_DIM)),
            cm, sm)


def _pack_layer(l, p):
    d = p["w_in"].shape[1]
    w = p["w_in"][l]
    o = np.cumsum([0, 256, 128, 128, 256, 256, 256, 256, 256, 256, MLA_Q_LORA, MLA_KV_LORA, MLA_ROPE])
    col = lambda i: w[:, int(o[i]):int(o[i + 1])]
    dup = lambda a: jnp.concatenate([a[:, :GQA_DIM], a[:, :GQA_DIM], a[:, GQA_DIM:], a[:, GQA_DIM:]], axis=1)
    z = lambda n: jnp.zeros((d, n), F32)
    w_in = jnp.concatenate(
        [col(0), dup(col(1)), dup(col(2)), col(3), col(4), col(5), col(6), col(7), col(8),
         col(9), z(SEG_W - MLA_Q_LORA), col(10),
         z(MLA_NOPE), col(11), z(LANES - MLA_NOPE - MLA_ROPE)], axis=1).astype(BF16)
    dq = MLA_NOPE + MLA_ROPE
    wqu = p["mla_w_q_up"][l].reshape(MLA_Q_LORA, MLA_HEADS, dq)
    wqu = jnp.pad(wqu, ((0, SEG_W - MLA_Q_LORA), (0, 0), (0, LANES - dq)))
    wkv = p["mla_w_kv_up"][l].reshape(MLA_KV_LORA, MLA_HEADS, MLA_NOPE + MLA_V)
    wkk = jnp.pad(wkv[:, :, :MLA_NOPE], ((0, 0), (0, 0), (0, LANES - MLA_NOPE)))
    e = p["moe_w_gate"].shape[1]
    rw = jnp.pad(p["router_w"], ((0, 0), (0, LANES - e)))
    rw_hi = rw.astype(BF16)
    rw_lo = (rw - rw_hi.astype(F32)).astype(BF16)
    return {
        "gmix": p["norm_mix"][l][None], "gffn": p["norm_ffn"][l][None],
        "w_in": w_in,
        "wq_up": wqu.reshape(SEG_W, MLA_HEADS * LANES).astype(BF16),
        "wkv_k": wkk.reshape(MLA_KV_LORA, MLA_HEADS * LANES).astype(BF16),
        "wkv_v": wkv[:, :, MLA_NOPE:].reshape(MLA_KV_LORA, MLA_HEADS * MLA_V).astype(BF16),
        "gq": (jnp.tile(p["gqa_q_norm"][l], GQA_HEADS) * (GQA_DIM ** -0.5 * LOG2E))[None],
        "gk": jnp.tile(p["gqa_k_norm"][l], GQA_HEADS)[None],
        "gcq": jnp.pad(p["mla_q_norm"][l], (0, SEG_W - MLA_Q_LORA))[None],
        "gckv": p["mla_kv_norm"][l][None],
        "w_out": p["w_out"][l].astype(BF16),
        "rw_hi": rw_hi, "rw_lo": rw_lo,
        "rb": jnp.broadcast_to(p["router_b"][:, None], (e, LANES)),
        "wgu": jnp.concatenate([p["moe_w_gate"][l], p["moe_w_up"][l]], axis=2).astype(BF16),
        "wd": p["moe_w_down"][l].astype(BF16),
        "sgu": jnp.concatenate([p["shared_w_gate"][l], p["shared_w_up"][l]], axis=1).astype(BF16),
        "sd": p["shared_w_down"][l].astype(BF16),
        "fin": p["final_norm"][None],
        "diff": (p["diff_lq1"][l][None], p["diff_lk1"][l][None], p["diff_lq2"][l][None],
                 p["diff_lk2"][l][None], p["diff_subln"][l][:, None]),
    }


def kernel(x, c, ctx, c_ctx, w_mod, b_mod, norm_mix, norm_ffn, w_in, w_out, gqa_q_norm, gqa_k_norm,
           na_rel_bias, diff_lq1, diff_lk1, diff_lq2, diff_lk2, diff_subln, mla_q_norm, mla_w_q_up,
           mla_kv_norm, mla_w_kv_up, router_w, router_b, moe_w_gate, moe_w_up, moe_w_down,
           shared_w_gate, shared_w_up, shared_w_down, final_norm):
    p = dict(w_in=w_in, norm_mix=norm_mix, norm_ffn=norm_ffn, w_out=w_out, gqa_q_norm=gqa_q_norm,
             gqa_k_norm=gqa_k_norm, diff_lq1=diff_lq1, diff_lk1=diff_lk1, diff_lq2=diff_lq2,
             diff_lk2=diff_lk2, diff_subln=diff_subln, mla_q_norm=mla_q_norm, mla_w_q_up=mla_w_q_up,
             mla_kv_norm=mla_kv_norm, mla_w_kv_up=mla_w_kv_up, router_w=router_w, router_b=router_b,
             moe_w_gate=moe_w_gate, moe_w_up=moe_w_up, moe_w_down=moe_w_down,
             shared_w_gate=shared_w_gate, shared_w_up=shared_w_up, shared_w_down=shared_w_down,
             final_norm=final_norm)
    b, s, d = x.shape
    nctx = ctx.shape[1]
    depth = w_mod.shape[0]
    rows = s // GRID_W
    assert s % TOK_TILE == 0 and s // NA_TILE >= NA_CHUNKS + NA_QCHUNKS
    assert nctx % LANES == 0 and (b * s) % MOE_TILE == 0 and s % MOE_TILE == 0

    mrows = -(-(b + 1) // SUBLANES) * SUBLANES
    cc = jnp.zeros((mrows, d), F32).at[:b].set(c).at[b].set(c_ctx)
    mod = _modulation(cc, w_mod, b_mod)
    tables = _rope_tables(s)
    lat_row = lambda bi: bi
    ctx_row = lambda bi: b
    tm_ctx = b * nctx if b * nctx <= MOE_TILE else nctx

    xc = ctx
    for l in range(depth):
        want_ctx = l < depth - 1
        lam_init = 0.8 - 0.6 * math.exp(-0.3 * l)
        pw = _pack_layer(l, p)
        m6 = mod[l].reshape(mrows, 6, 1, d)
        sh1, sc1, g1, sh2, sc2, g2 = (m6[:, k] for k in range(6))

        lat = _project(x, sh1, sc1, lat_row, pw, tables, TOK_TILE, LAT_OUTS)
        cx = _project(xc, sh1, sc1, ctx_row, pw, None, nctx, CTX_OUTS if want_ctx else CTX_KV_OUTS)

        def dense(g, kind, extras=()):
            return _flash_lat(lat[g + ".qT"], cx[g + ".k"], cx[g + ".vT"], lat[g + ".k"], lat[g + ".vT"],
                              kind, extras, lam_init)

        def ctx_only(g, kind, extras=()):
            return _flash_ctx(cx[g + ".qT"], cx[g + ".k"], cx[g + ".vT"], kind, extras, lam_init)

        a_lat = dense("A", "pair")
        bias = _na_bias_tiles(na_rel_bias[l], rows, NA_TILE)
        b_lat = _na_attention(lat["B.qT"], cx["B.k"], cx["B.vT"], lat["B.k"], lat["B.vT"], bias)
        c_lat = dense("C", "diff", pw["diff"])
        d_lat = dense("D", "mla")
        x1, h2, gates = _out_router((a_lat, b_lat, c_lat, d_lat), x, g1, sh2, sc2, lat_row, pw, TOK_TILE)
        tiles_per_batch = s // MOE_TILE
        x = _moe(h2.reshape(b * s, d), gates.reshape(b * s, LANES), x1.reshape(b * s, d), g2,
                 lambda i: i // tiles_per_batch, pw, MOE_TILE, not want_ctx).reshape(b, s, d)

        if want_ctx:
            a_c = ctx_only("A", "pair")
            b_c = ctx_only("B", "pair")
            c_c = ctx_only("C", "diff", pw["diff"])
            d_c = ctx_only("D", "mla")
            xc1, h2c, gates_c = _out_router((a_c, b_c, c_c, d_c), xc, g1, sh2, sc2, ctx_row, pw, nctx)
            xc = _moe(h2c.reshape(b * nctx, d), gates_c.reshape(b * nctx, LANES),
                      xc1.reshape(b * nctx, d), g2, lambda i: b, pw, tm_ctx, False).reshape(b, nctx, d)
    return x
```

```python
import functools
import math

import numpy as np
import jax
import jax.numpy as jnp
from jax import lax
from jax.experimental import pallas as pl
from jax.experimental.pallas import tpu as pltpu

F32 = jnp.float32
BF16 = jnp.bfloat16

GRID_W = 64
EPS = 1e-6
ROPE_BASE = 10000.0
GQA_HEADS, GQA_KV_HEADS, GQA_DIM = 4, 2, 64
NA_HEADS, NA_DIM, NA_WIN_ROWS, NA_WIN_COLS = 4, 64, 8, 16
DIFF_HEADS, DIFF_QK_DIM, DIFF_V_DIM = 4, 32, 64
MLA_HEADS, MLA_NOPE, MLA_ROPE, MLA_V, MLA_Q_LORA, MLA_KV_LORA = 4, 64, 32, 64, 192, 128
N_EXPERTS, MOE_GROUPS, D_EXPERT = 16, 4, 256
EXPERTS_PER_GROUP = N_EXPERTS // MOE_GROUPS

LANES = 128
SUBLANES = 8
VMEM_LIMIT = 56 * 1024 * 1024

TOK_TILE = 512
NA_TILE = 256
MOE_TILE = 1024
MOD_TILE = 1024
PROJ_SUBTILES = 2
OUT_SUBTILES = 2
NEG = -1e30
LOG2E = math.log2(math.e)

SEG_W = 256
OFF_AQ, OFF_AK, OFF_AV = 0, 256, 512
OFF_BQ, OFF_BK, OFF_BV = 768, 1024, 1280
OFF_CQ, OFF_CK, OFF_CV = 1536, 1792, 2048
OFF_DQ, OFF_DKV, OFF_DPE = 2304, 2560, 2688
IN_PACKED = 2816


def _cparams(sem):
    return pltpu.CompilerParams(dimension_semantics=sem, vmem_limit_bytes=VMEM_LIMIT)


def _lane_iota(shape):
    return lax.broadcasted_iota(jnp.int32, shape, 1)


def _lane_group(shape, width):
    return lax.shift_right_logical(_lane_iota(shape), int(math.log2(width)))


def _sigmoid(x):
    return 1.0 / (1.0 + jnp.exp(-x))


def _mod_kernel(c_ref, w_ref, b_ref, o_ref):
    c = c_ref[...]
    s = c * _sigmoid(c)
    o_ref[0] = jnp.dot(s, w_ref[0], preferred_element_type=F32,
                       precision=lax.Precision.HIGHEST) + b_ref[0]


def _modulation(cc, w_mod, b_mod):
    depth, d, n = w_mod.shape
    rows = cc.shape[0]
    return pl.pallas_call(
        _mod_kernel,
        grid=(depth, n // MOD_TILE),
        in_specs=[
            pl.BlockSpec((rows, d), lambda l, j: (0, 0)),
            pl.BlockSpec((1, d, MOD_TILE), lambda l, j: (l, 0, j)),
            pl.BlockSpec((1, 1, MOD_TILE), lambda l, j: (l, 0, j)),
        ],
        out_specs=pl.BlockSpec((1, rows, MOD_TILE), lambda l, j: (l, 0, j)),
        out_shape=jax.ShapeDtypeStruct((depth, rows, n), F32),
        compiler_params=_cparams(("parallel", "parallel")),
        name="modulation",
    )(cc, w_mod, b_mod.reshape(depth, 1, n))


def _swap_halves(x, half):
    w = x.shape[1]
    lane = _lane_iota(x.shape)
    fwd = pltpu.roll(x, half, 1)
    bwd = pltpu.roll(x, w - half, 1)
    return jnp.where((lane & (2 * half - 1)) < half, bwd, fwd)


def _rope(x, cos, sin, half):
    reps = x.shape[1] // cos.shape[1]
    c = jnp.concatenate([cos] * reps, axis=1) if reps > 1 else cos
    s = jnp.concatenate([sin] * reps, axis=1) if reps > 1 else sin
    return x * c + _swap_halves(x, half) * s


def _group_rms(x, group, count):
    gid = _lane_group(x.shape, group)
    x2 = x * x
    inv = jnp.zeros_like(x)
    for g in range(x.shape[1] // group):
        msk = gid == g
        ms = jnp.sum(jnp.where(msk, x2, 0.0), axis=1, keepdims=True) * (1.0 / count)
        inv = jnp.where(msk, lax.rsqrt(ms + EPS), inv)
    return x * inv


LAT_OUTS = ("A.qT", "A.k", "A.vT", "B.qT", "B.k", "B.vT", "C.qT", "C.k", "C.vT", "D.qT", "D.k", "D.vT")
CTX_KV_OUTS = ("A.k", "A.vT", "B.k", "B.vT", "C.k", "C.vT", "D.k", "D.vT")
CTX_OUTS = CTX_KV_OUTS + ("A.qT", "B.qT", "C.qT", "D.qT")
GROUP_W = {"A": SEG_W, "B": SEG_W, "C": SEG_W, "D": 2 * SEG_W}


def _proj_kernel(*refs, rope, outs):
    (x_ref, sh_ref, sc_ref, gmix_ref, w_ref, wqu_ref, wkk_ref, wkv_ref,
     gq_ref, gk_ref, gcq_ref, gckv_ref) = refs[:12]
    pos = 12
    tabs = refs[pos:pos + 6] if rope else ()
    pos += len(tabs)
    out = dict(zip(outs, refs[pos:pos + len(outs)]))
    tm = x_ref.shape[1]
    parts = PROJ_SUBTILES if tm % (PROJ_SUBTILES * 2 * LANES) == 0 else 1
    sub = tm // parts
    for h in range(parts):
        _proj_subtile(slice(h * sub, (h + 1) * sub), x_ref, sh_ref, sc_ref, gmix_ref, w_ref, wqu_ref,
                      wkk_ref, wkv_ref, gq_ref, gk_ref, gcq_ref, gckv_ref, tabs, out, rope)


def _proj_subtile(rows, x_ref, sh_ref, sc_ref, gmix_ref, w_ref, wqu_ref, wkk_ref, wkv_ref,
                  gq_ref, gk_ref, gcq_ref, gckv_ref, tabs, out, rope):
    r0, sub = rows.start, rows.stop - rows.start
    if rope:
        cg, sg, cd, sd, cm, sm = (t[rows, :] for t in tabs)

    def want(g, t):
        return (g + "." + t) in out or (g + "." + t + "T") in out

    def put(g, t, val):
        if g + "." + t in out:
            out[g + "." + t][0, rows, :] = val.astype(BF16)
        if g + "." + t + "T" in out:
            ref = out[g + "." + t + "T"]
            vt = val.T.astype(BF16)
            cw = ref.shape[3]
            if cw >= sub:
                ref[0, r0 // cw, :, r0 % cw:r0 % cw + sub] = vt
            else:
                for c in range(sub // cw):
                    ref[0, r0 // cw + c] = vt[:, c * cw:(c + 1) * cw]

    x = x_ref[0, rows, :]
    ms = jnp.mean(x * x, axis=-1, keepdims=True)
    h = x * lax.rsqrt(ms + EPS) * gmix_ref[...]
    h = h * (1.0 + sc_ref[0]) + sh_ref[0]
    hb = h.astype(BF16)

    def seg(off, width=SEG_W):
        return jnp.dot(hb, w_ref[:, off:off + width], preferred_element_type=F32)

    cq_raw = seg(OFF_DQ) if want("D", "q") else None
    ckv_raw = seg(OFF_DKV, LANES)
    kpe = seg(OFF_DPE, LANES)

    if want("A", "q"):
        q = _group_rms(seg(OFF_AQ), GQA_DIM, GQA_DIM) * gq_ref[...]
        if rope:
            q = _rope(q, cg, sg, GQA_DIM // 4)
        put("A", "q", q)
    k = _group_rms(seg(OFF_AK), GQA_DIM, GQA_DIM) * gk_ref[...]
    if rope:
        k = _rope(k, cg, sg, GQA_DIM // 4)
    put("A", "k", k)
    put("A", "v", seg(OFF_AV))

    if want("B", "q"):
        put("B", "q", seg(OFF_BQ) * (NA_DIM ** -0.5 * LOG2E))
    put("B", "k", seg(OFF_BK))
    put("B", "v", seg(OFF_BV))

    if want("C", "q"):
        q = seg(OFF_CQ) * (DIFF_QK_DIM ** -0.5 * LOG2E)
        if rope:
            q = _rope(q, cd, sd, DIFF_QK_DIM // 4)
        put("C", "q", q)
    k = seg(OFF_CK)
    if rope:
        k = _rope(k, cd, sd, DIFF_QK_DIM // 4)
    put("C", "k", k)
    put("C", "v", seg(OFF_CV))

    if want("D", "q"):
        cq = _group_rms(cq_raw, SEG_W, MLA_Q_LORA) * gcq_ref[...]
        q = jnp.dot(cq.astype(BF16), wqu_ref[...], preferred_element_type=F32)
        q = q * ((MLA_NOPE + MLA_ROPE) ** -0.5 * LOG2E)
        if rope:
            q = _rope(q, cm, sm, MLA_ROPE // 4)
        put("D", "q", q)
    ckv = _group_rms(ckv_raw, LANES, MLA_KV_LORA) * gckv_ref[...]
    ckvb = ckv.astype(BF16)
    k = jnp.dot(ckvb, wkk_ref[...], preferred_element_type=F32)
    if rope:
        kpe = _rope(kpe, cm, sm, MLA_ROPE // 4)
    put("D", "k", k + jnp.concatenate([kpe] * MLA_HEADS, axis=1))
    put("D", "v", jnp.dot(ckvb, wkv_ref[...], preferred_element_type=F32))


def _project(x, sh, sc, row_of_batch, pw, tables, tm, outs):
    b, n, d = x.shape
    nt = n // tm
    rope = tables is not None
    full = lambda a: pl.BlockSpec(a.shape, lambda bi, i: (0,) * a.ndim)
    mod_spec = pl.BlockSpec((1, 1, d), lambda bi, i: (row_of_batch(bi), 0, 0))
    ins = [x, sh, sc, pw["gmix"], pw["w_in"], pw["wq_up"], pw["wkv_k"], pw["wkv_v"],
           pw["gq"], pw["gk"], pw["gcq"], pw["gckv"]]
    in_specs = [pl.BlockSpec((1, tm, d), lambda bi, i: (bi, i, 0)), mod_spec, mod_spec]
    in_specs += [full(a) for a in ins[3:]]
    if rope:
        ins += list(tables)
        in_specs += [pl.BlockSpec((tm, LANES), lambda bi, i: (i, 0)) for _ in tables]
    shapes, specs = [], []
    for name in outs:
        g, t = name.split(".")
        w = SEG_W if t[0] == "v" else GROUP_W[g]
        if t.endswith("T"):
            cw = min(tm, NA_TILE) if g == "B" else tm
            shapes.append(jax.ShapeDtypeStruct((b, n // cw, w, cw), BF16))
            specs.append(pl.BlockSpec((1, tm // cw, w, cw), lambda bi, i: (bi, i, 0, 0)))
        else:
            shapes.append(jax.ShapeDtypeStruct((b, n, w), BF16))
            specs.append(pl.BlockSpec((1, tm, w), lambda bi, i: (bi, i, 0)))
    res = pl.pallas_call(
        functools.partial(_proj_kernel, rope=rope, outs=tuple(outs)),
        grid=(b, nt),
        in_specs=in_specs,
        out_specs=specs,
        out_shape=shapes,
        compiler_params=_cparams(("parallel", "parallel")),
        name="proj_rope" if rope else "proj_ctx",
    )(*ins)
    return dict(zip(outs, res))


HEAD_V = 64


def _tile_lanes(x, width):
    reps = width // x.shape[1]
    return jnp.tile(x, (1, reps)) if reps > 1 else x


def with_ones(vt):
    rows = lax.broadcasted_iota(jnp.int32, (LANES - HEAD_V, vt.shape[1]), 0)
    ones = jnp.where(rows == 0, 1.0, 0.0).astype(BF16)
    return jnp.concatenate([vt, ones], axis=0)


def _map_slices(kind):
    def k_lanes(m):
        return slice(m * LANES, (m + 1) * LANES) if kind == "mla" else slice(None)

    def v_rows(m):
        hv = m // 2 if kind == "diff" else m
        return slice(hv * HEAD_V, (hv + 1) * HEAD_V)

    return k_lanes, v_rows


def _query_slabs(qt_ref, kind, nm):
    qv = jnp.concatenate([qt_ref[0, c] for c in range(qt_ref.shape[1])], axis=1)
    if kind == "mla":
        return [qv[m * LANES:(m + 1) * LANES, :] for m in range(nm)]
    rows = lax.broadcasted_iota(jnp.int32, qv.shape, 0)
    grp = lax.shift_right_logical(rows, int(math.log2(LANES // nm)))
    return [jnp.where(grp == m, qv, jnp.zeros_like(qv)) for m in range(nm)]


def _finish_heads(accs, kind, lam_init, diff_refs):
    outs = [a[0:HEAD_V] / a[HEAD_V:HEAD_V + 1] for a in accs]
    if kind == "diff":
        lq1_ref, lk1_ref, lq2_ref, lk2_ref, sub_ref = diff_refs
        lam = (jnp.exp(jnp.sum(lq1_ref[...] * lk1_ref[...], axis=1, keepdims=True))
               - jnp.exp(jnp.sum(lq2_ref[...] * lk2_ref[...], axis=1, keepdims=True))
               + lam_init)
        heads = []
        for hd in range(2):
            dlt = outs[2 * hd] - lam * outs[2 * hd + 1]
            ms2 = jnp.mean(dlt * dlt, axis=0, keepdims=True)
            heads.append(dlt * lax.rsqrt(ms2 + EPS) * sub_ref[...] * (1.0 - lam_init))
        outs = heads
    return jnp.concatenate(outs, axis=0).T


def _flash_ctx_kernel(*refs, kind, lam_init):
    qt_ref, kc_ref, vct_ref = refs[:3]
    nd = 5 if kind == "diff" else 0
    diff_refs, o_ref = refs[3:3 + nd], refs[3 + nd]
    nm = 4 if kind == "diff" else 2
    k_lanes, v_rows = _map_slices(kind)
    accs = []
    for m, qm in enumerate(_query_slabs(qt_ref, kind, nm)):
        st = jnp.dot(kc_ref[0, :, k_lanes(m)], qm, preferred_element_type=F32)
        p = jnp.exp2(st - jnp.max(st, axis=0, keepdims=True)).astype(BF16)
        accs.append(jnp.dot(with_ones(vct_ref[0, 0, v_rows(m), :]), p, preferred_element_type=F32))
    o_ref[0] = _finish_heads(accs, kind, lam_init, diff_refs).astype(o_ref.dtype)


def _flash_lat_kernel(*refs, kind, lam_init):
    qt_ref, kc_ref, vct_ref, k_ref, vt_ref = refs[:5]
    nd = 5 if kind == "diff" else 0
    diff_refs, o_ref = refs[5:5 + nd], refs[5 + nd]
    qs_e, acc_e, s_e, qs_l, acc_l, s_l, m_ref, mc_ref = refs[6 + nd:14 + nd]
    early = (qs_e, acc_e, s_e, 0, False)
    late = (qs_l, acc_l, s_l, vt_ref.shape[1] // 2, True)

    nm = qs_e.shape[0]
    nk, tk = vt_ref.shape[1], vt_ref.shape[3]
    half = nk // 2
    iters = half // 2
    g = pl.program_id(2)
    nq = pl.num_programs(2) - 1
    k_lanes, v_rows = _map_slices(kind)

    def produce(tile, slot, j, m):
        qs_ref, _, s_ref = tile[:3]
        rows = pl.ds(pl.multiple_of(j * tk, tk), tk)
        st = jnp.dot(k_ref[0, rows, k_lanes(m)], qs_ref[m], preferred_element_type=F32)
        s_ref[slot, m] = st
        return jnp.max(st, axis=0, keepdims=True)

    def consume(tile, slot, j, m, m_prev, m_chunk):
        _, acc_ref, s_ref = tile[:3]
        m_new = jnp.maximum(m_prev, m_chunk)
        alpha = jnp.exp2(m_prev - m_new)
        p = jnp.exp2(s_ref[slot, m] - m_new).astype(BF16)
        acc_ref[m] = alpha * acc_ref[m] + jnp.dot(
            with_ones(vt_ref[0, j, v_rows(m), :]), p, preferred_element_type=F32)
        return m_new

    def iteration(i, carries, tiles, last):
        maps = range(nm)
        j0s = [tile[3] + 2 * i for tile in tiles]
        c1 = [[None] * nm for _ in tiles]
        m1 = [[None] * nm for _ in tiles]
        c2 = [[None] * nm for _ in tiles]
        m2 = [[None] * nm for _ in tiles]
        for t, (tile, j0, c) in enumerate(zip(tiles, j0s, carries)):
            for m in maps:
                c1[t][m] = produce(tile, 1, j0 + 1, m)
                m1[t][m] = consume(tile, 0, j0, m, c[0][m], c[1][m])
        for t, (tile, j0) in enumerate(zip(tiles, j0s)):
            for m in maps:
                c2[t][m] = c1[t][m] if (last and tile[4]) else produce(tile, 0, j0 + 2, m)
                m2[t][m] = consume(tile, 1, j0 + 1, m, m1[t][m], c1[t][m])
        return tuple((tuple(a), tuple(b)) for a, b in zip(m2, c2))

    def run(carries, tiles):
        carries = iteration(0, carries, tiles, iters == 1)
        if iters > 2:
            carries = lax.fori_loop(1, iters - 1, lambda i, c: iteration(i, c, tiles, False), carries)
        if iters > 1:
            carries = iteration(iters - 1, carries, tiles, True)
        return carries

    def start_early():
        for m, qm in enumerate(_query_slabs(qt_ref, kind, nm)):
            qs_e[m] = qm
        sts = [jnp.dot(kc_ref[0, :, k_lanes(m)], qs_e[m], preferred_element_type=F32)
               for m in range(nm)]
        ms = tuple(jnp.max(st, axis=0, keepdims=True) for st in sts)
        first = tuple(produce(early, 0, 0, m) for m in range(nm))

        def init_from_context():
            for m in range(nm):
                p = jnp.exp2(sts[m] - ms[m]).astype(BF16)
                acc_e[m] = jnp.dot(with_ones(vct_ref[0, 0, v_rows(m), :]), p,
                                   preferred_element_type=F32)

        return (ms, first), init_from_context

    def load_late():
        return (tuple(m_ref[m] for m in range(nm)), tuple(mc_ref[m] for m in range(nm)))

    def hand_over(carry):
        qs_l[...] = qs_e[...]
        acc_l[...] = acc_e[...]
        s_l[0] = s_e[0]
        for m in range(nm):
            m_ref[m] = carry[0][m]
            mc_ref[m] = carry[1][m]

    def finish_late():
        accs = [acc_l[m] for m in range(nm)]
        o_ref[0] = _finish_heads(accs, kind, lam_init, diff_refs).astype(o_ref.dtype)

    @pl.when(g == 0)
    def _():
        carry, init_from_context = start_early()
        init_from_context()
        hand_over(run((carry,), (early,))[0])

    @pl.when((g > 0) & (g < nq))
    def _():
        carry_l = load_late()
        carry_e, init_from_context = start_early()
        init_from_context()
        carry_l, carry_e = run((carry_l, carry_e), (late, early))
        finish_late()
        hand_over(carry_e)

    @pl.when(g == nq)
    def _():
        run((load_late(),), (late,))
        finish_late()


def _flash_specs(qt, kc, vct, kind, extras):
    b, nqc, wtot, cw = qt.shape
    wq = 2 * LANES if kind == "mla" else LANES
    c = kc.shape[1]
    extra_specs = [pl.BlockSpec(e.shape, lambda bi, p, i: (0, 0)) for e in extras]
    kv_specs = [pl.BlockSpec((1, c, wq), lambda bi, p, i: (bi, 0, p)),
                pl.BlockSpec((1, 1, LANES, c), lambda bi, p, i: (bi, 0, p, 0))]
    return b, nqc, wtot // wq, wq, cw, kv_specs, extra_specs


def _flash_ctx(qt, kc, vct, kind, extras=(), lam_init=0.0):
    b, nqc, npair, wq, cw, kv_specs, extra_specs = _flash_specs(qt, kc, vct, kind, extras)
    return pl.pallas_call(
        functools.partial(_flash_ctx_kernel, kind=kind, lam_init=lam_init),
        grid=(b, npair, nqc),
        in_specs=[pl.BlockSpec((1, 1, wq, cw), lambda bi, p, i: (bi, i, p, 0))] + kv_specs + extra_specs,
        out_specs=pl.BlockSpec((1, cw, LANES), lambda bi, p, i: (bi, i, p)),
        out_shape=jax.ShapeDtypeStruct((b, nqc * cw, npair * LANES), BF16),
        compiler_params=_cparams(("parallel", "parallel", "parallel")),
        name="flash_" + kind + "_ctx",
    )(qt, kc, vct, *extras)


def _flash_lat(qt, kc, vct, k, vt, kind, extras=(), lam_init=0.0):
    b, nq, npair, wq, tq, kv_specs, extra_specs = _flash_specs(qt, kc, vct, kind, extras)
    nm = 4 if kind == "diff" else 2
    nk, tk = vt.shape[1], vt.shape[3]
    assert nk % 4 == 0, "two phases of key chunks, processed in pairs"
    return pl.pallas_call(
        functools.partial(_flash_lat_kernel, kind=kind, lam_init=lam_init),
        grid=(b, npair, nq + 1),
        in_specs=[pl.BlockSpec((1, 1, wq, tq), lambda bi, p, g: (bi, jnp.minimum(g, nq - 1), p, 0))]
        + kv_specs
        + [pl.BlockSpec((1, k.shape[1], wq), lambda bi, p, g: (bi, 0, p)),
           pl.BlockSpec((1, nk, LANES, tk), lambda bi, p, g: (bi, 0, p, 0))]
        + extra_specs,
        out_specs=pl.BlockSpec((1, tq, LANES), lambda bi, p, g: (bi, jnp.maximum(g - 1, 0), p)),
        out_shape=jax.ShapeDtypeStruct((b, nq * tq, npair * LANES), BF16),
        scratch_shapes=2 * [
            pltpu.VMEM((nm, LANES, tq), BF16),
            pltpu.VMEM((nm, LANES, tq), F32),
            pltpu.VMEM((2, nm, tk, tq), F32),
        ] + 2 * [pltpu.VMEM((nm, 1, tq), F32)],
        compiler_params=_cparams(("parallel", "parallel", "arbitrary")),
        name="flash_" + kind,
    )(qt, kc, vct, k, vt, *extras)


NA_QCHUNKS = 2
NA_CHUNKS = NA_QCHUNKS + 2


def _na_kernel(qt_ref, k_ref, vt_ref, kc_ref, vct_ref, bias_ref, o_ref):
    nkc, tk = vt_ref.shape[1], vt_ref.shape[3]
    i = pl.program_id(2)
    c0 = jnp.clip(i * NA_QCHUNKS - 1, 0, nkc - NA_CHUNKS)
    qv = jnp.concatenate([qt_ref[0, c] for c in range(NA_QCHUNKS)], axis=1)
    rows = lax.broadcasted_iota(jnp.int32, qv.shape, 0)
    scores = []
    for m in range(2):
        own = lax.shift_right_logical(rows, int(math.log2(LANES // 2))) == m
        qm = jnp.where(own, qv, jnp.zeros_like(qv))
        ss = []
        for j in range(NA_CHUNKS):
            kj = k_ref[0, pl.ds(pl.multiple_of((c0 + j) * tk, tk), tk), :]
            ss.append(jnp.dot(kj, qm, preferred_element_type=F32) + bias_ref[0, m, j])
        scores.append((ss, jnp.dot(kc_ref[0], qm, preferred_element_type=F32)))
    outs = []
    for m, (ss, sc) in enumerate(scores):
        v_rows = slice(m * HEAD_V, (m + 1) * HEAD_V)
        mx = jnp.max(sc, axis=0, keepdims=True)
        for st in ss:
            mx = jnp.maximum(mx, jnp.max(st, axis=0, keepdims=True))
        acc = jnp.dot(with_ones(vct_ref[0, 0, v_rows, :]), jnp.exp2(sc - mx).astype(BF16),
                      preferred_element_type=F32)
        for j, st in enumerate(ss):
            acc = acc + jnp.dot(with_ones(vt_ref[0, c0 + j, v_rows, :]),
                                jnp.exp2(st - mx).astype(BF16), preferred_element_type=F32)
        outs.append(acc[0:HEAD_V] / acc[HEAD_V:HEAD_V + 1])
    o_ref[0] = jnp.concatenate(outs, axis=0).T.astype(o_ref.dtype)


def _na_bias_tiles(rel_bias, rows, tk):
    qrows = NA_QCHUNKS * tk // GRID_W
    nblk = rows // qrows
    nk = rows * GRID_W // tk
    krows = NA_CHUNKS * tk // GRID_W
    kh = min(NA_WIN_ROWS, rows)
    kw = NA_WIN_COLS
    blocks = np.array([0, 1, nblk - 1])
    c0 = np.clip(blocks * NA_QCHUNKS - 1, 0, nk - NA_CHUNKS)
    r = blocks[:, None] * qrows + np.arange(qrows)[None, :]
    kr = c0[:, None] * (tk // GRID_W) + np.arange(krows)[None, :]
    r0 = np.clip(r - kh // 2, 0, rows - kh)
    dr = kr[:, None, :] - r[:, :, None]
    valid_r = (kr[:, None, :] >= r0[:, :, None]) & (kr[:, None, :] < r0[:, :, None] + kh)
    cols = np.arange(GRID_W)
    cstart = np.clip(cols - kw // 2, 0, GRID_W - kw)
    dc = cols[None, :] - cols[:, None]
    valid_c = (cols[None, :] >= cstart[:, None]) & (cols[None, :] < cstart[:, None] + kw)
    ndc = 2 * NA_WIN_COLS - 1
    onehot = ((dc[None] + NA_WIN_COLS - 1) == np.arange(ndc)[:, None, None]) & valid_c[None]
    onehot = np.concatenate([onehot, ~valid_c[None]], axis=0)
    dr_idx = np.clip(dr + NA_WIN_ROWS - 1, 0, 2 * NA_WIN_ROWS - 2)
    g1 = rel_bias[:, dr_idx, :] * LOG2E
    g1 = jnp.concatenate([g1, jnp.full(g1.shape[:-1] + (1,), NEG, F32)], axis=-1)
    g1 = jnp.where(jnp.asarray(valid_r)[None, :, :, :, None], g1, NEG)
    tiles = jnp.einsum("htabd,dcx->thbxac", g1, jnp.asarray(onehot, F32),
                       precision=lax.Precision.HIGHEST)
    return tiles.reshape(3, rel_bias.shape[0], NA_CHUNKS, tk, qrows * GRID_W)


def _na_attention(qt, kc, vct, k, vt, bias):
    b, nqc, wtot, cw = qt.shape
    nblk, tq = nqc // NA_QCHUNKS, NA_QCHUNKS * cw
    npair = wtot // LANES
    nkc, tk = vt.shape[1], vt.shape[3]
    c = kc.shape[1]
    n = k.shape[1]

    def bias_map(bi, p, i):
        return (jnp.where(i == 0, 0, jnp.where(i == nblk - 1, 2, 1)), p, 0, 0, 0)

    return pl.pallas_call(
        _na_kernel,
        grid=(b, npair, nblk),
        in_specs=[
            pl.BlockSpec((1, NA_QCHUNKS, LANES, cw), lambda bi, p, i: (bi, i, p, 0)),
            pl.BlockSpec((1, n, LANES), lambda bi, p, i: (bi, 0, p)),
            pl.BlockSpec((1, nkc, LANES, tk), lambda bi, p, i: (bi, 0, p, 0)),
            pl.BlockSpec((1, c, LANES), lambda bi, p, i: (bi, 0, p)),
            pl.BlockSpec((1, 1, LANES, c), lambda bi, p, i: (bi, 0, p, 0)),
            pl.BlockSpec((1, 2, NA_CHUNKS, tk, tq), bias_map),
        ],
        out_specs=pl.BlockSpec((1, tq, LANES), lambda bi, p, i: (bi, i, p)),
        out_shape=jax.ShapeDtypeStruct((b, nblk * tq, wtot), BF16),
        compiler_params=_cparams(("parallel", "parallel", "parallel")),
        name="na_attention",
    )(qt, k, vt, kc, vct, bias)


def _first_index_of_max(vals, row, big):
    mx = jnp.max(vals, axis=0, keepdims=True)
    idx = jnp.min(jnp.where(vals == mx, row, big), axis=0, keepdims=True)
    return mx, idx


def _router_gates(scores_t, sel_t):
    irow = lax.broadcasted_iota(jnp.int32, sel_t.shape, 0)
    row = irow.astype(F32)
    grp = lax.shift_right_logical(irow, int(math.log2(EXPERTS_PER_GROUP))).astype(F32)
    big = float(N_EXPERTS)
    best = None
    for g in range(MOE_GROUPS):
        vals = jnp.where(grp == float(g), sel_t, -jnp.inf)
        m1, i1 = _first_index_of_max(vals, row, big)
        m2 = jnp.max(jnp.where(row == i1, -jnp.inf, vals), axis=0, keepdims=True)
        gs = m1 + m2
        if best is None:
            best, bi = gs, jnp.zeros(gs.shape, F32)
        else:
            better = gs > best
            bi = jnp.where(better, float(g), bi)
            best = jnp.where(better, gs, best)
    msel = jnp.where(grp == bi, sel_t, -jnp.inf)
    _, i1 = _first_index_of_max(msel, row, big)
    msel2 = jnp.where(row == i1, -jnp.inf, msel)
    _, i2 = _first_index_of_max(msel2, row, big)
    w = jnp.where((row == i1) | (row == i2), scores_t, 0.0)
    return w / jnp.sum(w, axis=0, keepdims=True)


def _out_kernel(a_ref, b_ref, c_ref, d_ref, x_ref, g1_ref, sh_ref, sc_ref, gffn_ref,
                w_ref, rwh_ref, rwl_ref, rb_ref, xo_ref, h2_ref, gate_ref):
    tm = x_ref.shape[1]
    parts = OUT_SUBTILES if tm % (OUT_SUBTILES * LANES) == 0 else 1
    sub = tm // parts
    ys = []
    for h in range(parts):
        rows = slice(h * sub, (h + 1) * sub)
        y = None
        for gi, r in enumerate((a_ref, b_ref, c_ref, d_ref)):
            part = jnp.dot(r[0, rows, :], w_ref[gi * SEG_W:(gi + 1) * SEG_W, :],
                           preferred_element_type=F32)
            y = part if y is None else y + part
        ys.append(y)
    for h in range(parts):
        rows = slice(h * sub, (h + 1) * sub)
        xn = x_ref[0, rows, :] + g1_ref[0] * ys[h]
        xo_ref[0, rows, :] = xn
        ms = jnp.mean(xn * xn, axis=-1, keepdims=True)
        h2 = xn * lax.rsqrt(ms + EPS) * gffn_ref[...]
        h2 = h2 * (1.0 + sc_ref[0]) + sh_ref[0]
        hi = h2.astype(BF16)
        h2_ref[0, rows, :] = hi
        lo = (h2 - hi.astype(F32)).astype(BF16)
        logits = (jnp.dot(hi, rwh_ref[...], preferred_element_type=F32)
                  + jnp.dot(lo, rwh_ref[...], preferred_element_type=F32)
                  + jnp.dot(hi, rwl_ref[...], preferred_element_type=F32))
        scores_t = _sigmoid(logits).T[:N_EXPERTS]
        sel_t = scores_t + _tile_lanes(rb_ref[...], sub)
        gates_t = _router_gates(scores_t, sel_t)
        pad = jnp.zeros((LANES - N_EXPERTS, sub), F32)
        gate_ref[0, rows, :] = jnp.concatenate([gates_t, pad], axis=0).T


def _out_router(parts, x, g1, sh2, sc2, row_of_batch, pw, tm):
    b, n, d = x.shape
    full = lambda a: pl.BlockSpec(a.shape, lambda bi, i: (0,) * a.ndim)
    tok = lambda w: pl.BlockSpec((1, tm, w), lambda bi, i: (bi, i, 0))
    mod_spec = pl.BlockSpec((1, 1, d), lambda bi, i: (row_of_batch(bi), 0, 0))
    consts = [pw["gffn"], pw["w_out"], pw["rw_hi"], pw["rw_lo"], pw["rb"]]
    return pl.pallas_call(
        _out_kernel,
        grid=(b, n // tm),
        in_specs=[tok(SEG_W)] * 4 + [tok(d), mod_spec, mod_spec, mod_spec] + [full(a) for a in consts],
        out_specs=[tok(d), tok(d), tok(LANES)],
        out_shape=[jax.ShapeDtypeStruct((b, n, d), F32), jax.ShapeDtypeStruct((b, n, d), BF16),
                   jax.ShapeDtypeStruct((b, n, LANES), F32)],
        compiler_params=_cparams(("parallel", "parallel")),
        name="out_router",
    )(*parts, x, g1, sh2, sc2, *consts)


MOE_EXPERTS_PER_STEP = 4


def _swiglu_act(gu, scale=None):
    de = gu.shape[1] // 2
    g, u = gu[:, :de], gu[:, de:]
    act = (g * _sigmoid(g)) * u
    return (act if scale is None else act * scale).astype(BF16)


def _moe_kernel(h_ref, gate_ref, x_ref, g2_ref, wgu_ref, wd_ref, sgu_ref, sd_ref, fin_ref,
                o_ref, acc_ref, *, final_norm):
    step = pl.program_id(1)
    k = wgu_ref.shape[0]
    hb = h_ref[...]
    gates = gate_ref[...]
    lane = _lane_iota(gates.shape)
    acts = []
    for i in range(k):
        gu = jnp.dot(hb, wgu_ref[i], preferred_element_type=F32)
        col = jnp.sum(jnp.where(lane == step * k + i, gates, 0.0), axis=1, keepdims=True)
        acts.append(_swiglu_act(gu, col))
    wd = wd_ref[...]
    contrib = jnp.dot(jnp.concatenate(acts, axis=1), wd.reshape(k * wd.shape[1], wd.shape[2]),
                      preferred_element_type=F32)

    @pl.when(step == 0)
    def _():
        shared = _swiglu_act(jnp.dot(hb, sgu_ref[...], preferred_element_type=F32))
        acc_ref[...] = contrib + jnp.dot(shared, sd_ref[...], preferred_element_type=F32)

    @pl.when(step > 0)
    def _():
        acc_ref[...] += contrib

    @pl.when(step == pl.num_programs(1) - 1)
    def _():
        y = x_ref[...] + g2_ref[0] * acc_ref[...]
        if final_norm:
            ms = jnp.mean(y * y, axis=-1, keepdims=True)
            y = y * lax.rsqrt(ms + EPS) * fin_ref[...]
        o_ref[...] = y


def _moe(h2, gates, x, g2, row_of_tile, pw, tm, final_norm):
    t, d = x.shape
    ne, _, de2 = pw["wgu"].shape
    k = MOE_EXPERTS_PER_STEP
    full = lambda a: pl.BlockSpec(a.shape, lambda i, e: (0,) * a.ndim)
    return pl.pallas_call(
        functools.partial(_moe_kernel, final_norm=final_norm),
        grid=(t // tm, ne // k),
        in_specs=[
            pl.BlockSpec((tm, d), lambda i, e: (i, 0)),
            pl.BlockSpec((tm, LANES), lambda i, e: (i, 0)),
            pl.BlockSpec((tm, d), lambda i, e: (i, 0)),
            pl.BlockSpec((1, 1, d), lambda i, e: (row_of_tile(i), 0, 0)),
            pl.BlockSpec((k, d, de2), lambda i, e: (e, 0, 0)),
            pl.BlockSpec((k, de2 // 2, d), lambda i, e: (e, 0, 0)),
            full(pw["sgu"]), full(pw["sd"]), full(pw["fin"]),
        ],
        out_specs=pl.BlockSpec((tm, d), lambda i, e: (i, 0)),
        out_shape=jax.ShapeDtypeStruct((t, d), F32),
        scratch_shapes=[pltpu.VMEM((tm, d), F32)],
        compiler_params=_cparams(("parallel", "arbitrary")),
        name="moe_final" if final_norm else "moe",
    )(h2, gates, x, g2, pw["wgu"], pw["wd"], pw["sgu"], pw["sd"], pw["fin"])


def _axial_angles(n_tok, rot_dim):
    half = rot_dim // 2
    freqs = ROPE_BASE ** (-jnp.arange(0, half, 2, dtype=F32) / half)
    t = jnp.arange(n_tok, dtype=jnp.int32)
    row = (t // GRID_W).astype(F32)
    col = (t % GRID_W).astype(F32)
    return row[:, None] * freqs, col[:, None] * freqs


def _rope_table(n_tok, rot_dim):
    ar, ac = _axial_angles(n_tok, rot_dim)
    cos = jnp.concatenate([jnp.cos(ar)] * 2 + [jnp.cos(ac)] * 2, axis=1)
    sin = jnp.concatenate([-jnp.sin(ar), jnp.sin(ar), -jnp.sin(ac), jnp.sin(ac)], axis=1)
    return cos, sin


def _rope_tables(n_tok):
    cg, sg = _rope_table(n_tok, GQA_DIM)
    cd, sd = _rope_table(n_tok, DIFF_QK_DIM)
    cm32, sm32 = _rope_table(n_tok, MLA_ROPE)
    ones = jnp.ones((n_tok, MLA_NOPE), F32)
    zeros = jnp.zeros((n_tok, MLA_NOPE), F32)
    pad = LANES - MLA_NOPE - MLA_ROPE
    cm = jnp.concatenate([ones, cm32, ones[:, :pad]], axis=1)
    sm = jnp.concatenate([zeros, sm32, zeros[:, :pad]], axis=1)
    return (jnp.tile(cg, (1, LANES // GQA_DIM)), jnp.tile(sg, (1, LANES // GQA_DIM)),
            jnp.tile(cd, (1, LANES // DIFF_QK_DIM)), jnp.tile(sd, (1, LANES // DIFF_QK_DIM)),
            cm, sm)


def _pack_layer(l, p):
    d = p["w_in"].shape[1]
    w = p["w_in"][l]
    o = np.cumsum([0, 256, 128, 128, 256, 256, 256, 256, 256, 256, MLA_Q_LORA, MLA_KV_LORA, MLA_ROPE])
    col = lambda i: w[:, int(o[i]):int(o[i + 1])]
    dup = lambda a: jnp.concatenate([a[:, :GQA_DIM], a[:, :GQA_DIM], a[:, GQA_DIM:], a[:, GQA_DIM:]], axis=1)
    z = lambda n: jnp.zeros((d, n), F32)
    w_in = jnp.concatenate(
        [col(0), dup(col(1)), dup(col(2)), col(3), col(4), col(5), col(6), col(7), col(8),
         col(9), z(SEG_W - MLA_Q_LORA), col(10),
         z(MLA_NOPE), col(11), z(LANES - MLA_NOPE - MLA_ROPE)], axis=1).astype(BF16)
    dq = MLA_NOPE + MLA_ROPE
    wqu = p["mla_w_q_up"][l].reshape(MLA_Q_LORA, MLA_HEADS, dq)
    wqu = jnp.pad(wqu, ((0, SEG_W - MLA_Q_LORA), (0, 0), (0, LANES - dq)))
    wkv = p["mla_w_kv_up"][l].reshape(MLA_KV_LORA, MLA_HEADS, MLA_NOPE + MLA_V)
    wkk = jnp.pad(wkv[:, :, :MLA_NOPE], ((0, 0), (0, 0), (0, LANES - MLA_NOPE)))
    e = p["moe_w_gate"].shape[1]
    rw = jnp.pad(p["router_w"], ((0, 0), (0, LANES - e)))
    rw_hi = rw.astype(BF16)
    rw_lo = (rw - rw_hi.astype(F32)).astype(BF16)
    return {
        "gmix": p["norm_mix"][l][None], "gffn": p["norm_ffn"][l][None],
        "w_in": w_in,
        "wq_up": wqu.reshape(SEG_W, MLA_HEADS * LANES).astype(BF16),
        "wkv_k": wkk.reshape(MLA_KV_LORA, MLA_HEADS * LANES).astype(BF16),
        "wkv_v": wkv[:, :, MLA_NOPE:].reshape(MLA_KV_LORA, MLA_HEADS * MLA_V).astype(BF16),
        "gq": (jnp.tile(p["gqa_q_norm"][l], GQA_HEADS) * (GQA_DIM ** -0.5 * LOG2E))[None],
        "gk": jnp.tile(p["gqa_k_norm"][l], GQA_HEADS)[None],
        "gcq": jnp.pad(p["mla_q_norm"][l], (0, SEG_W - MLA_Q_LORA))[None],
        "gckv": p["mla_kv_norm"][l][None],
        "w_out": p["w_out"][l].astype(BF16),
        "rw_hi": rw_hi, "rw_lo": rw_lo,
        "rb": jnp.broadcast_to(p["router_b"][:, None], (e, LANES)),
        "wgu": jnp.concatenate([p["moe_w_gate"][l], p["moe_w_up"][l]], axis=2).astype(BF16),
        "wd": p["moe_w_down"][l].astype(BF16),
        "sgu": jnp.concatenate([p["shared_w_gate"][l], p["shared_w_up"][l]], axis=1).astype(BF16),
        "sd": p["shared_w_down"][l].astype(BF16),
        "fin": p["final_norm"][None],
        "diff": (p["diff_lq1"][l][None], p["diff_lk1"][l][None], p["diff_lq2"][l][None],
                 p["diff_lk2"][l][None], p["diff_subln"][l][:, None]),
    }


def kernel(x, c, ctx, c_ctx, w_mod, b_mod, norm_mix, norm_ffn, w_in, w_out, gqa_q_norm, gqa_k_norm,
           na_rel_bias, diff_lq1, diff_lk1, diff_lq2, diff_lk2, diff_subln, mla_q_norm, mla_w_q_up,
           mla_kv_norm, mla_w_kv_up, router_w, router_b, moe_w_gate, moe_w_up, moe_w_down,
           shared_w_gate, shared_w_up, shared_w_down, final_norm):
    p = dict(w_in=w_in, norm_mix=norm_mix, norm_ffn=norm_ffn, w_out=w_out, gqa_q_norm=gqa_q_norm,
             gqa_k_norm=gqa_k_norm, diff_lq1=diff_lq1, diff_lk1=diff_lk1, diff_lq2=diff_lq2,
             diff_lk2=diff_lk2, diff_subln=diff_subln, mla_q_norm=mla_q_norm, mla_w_q_up=mla_w_q_up,
             mla_kv_norm=mla_kv_norm, mla_w_kv_up=mla_w_kv_up, router_w=router_w, router_b=router_b,
             moe_w_gate=moe_w_gate, moe_w_up=moe_w_up, moe_w_down=moe_w_down,
             shared_w_gate=shared_w_gate, shared_w_up=shared_w_up, shared_w_down=shared_w_down,
             final_norm=final_norm)
    b, s, d = x.shape
    nctx = ctx.shape[1]
    depth = w_mod.shape[0]
    rows = s // GRID_W
    assert s % TOK_TILE == 0 and s // NA_TILE >= NA_CHUNKS + NA_QCHUNKS
    assert nctx % LANES == 0 and (b * s) % MOE_TILE == 0 and s % MOE_TILE == 0

    mrows = -(-(b + 1) // SUBLANES) * SUBLANES
    cc = jnp.zeros((mrows, d), F32).at[:b].set(c).at[b].set(c_ctx)
    mod = _modulation(cc, w_mod, b_mod)
    tables = _rope_tables(s)
    lat_row = lambda bi: bi
    ctx_row = lambda bi: b
    tm_ctx = b * nctx if b * nctx <= MOE_TILE else nctx

    xc = ctx
    for l in range(depth):
        want_ctx = l < depth - 1
        lam_init = 0.8 - 0.6 * math.exp(-0.3 * l)
        pw = _pack_layer(l, p)
        m6 = mod[l].reshape(mrows, 6, 1, d)
        sh1, sc1, g1, sh2, sc2, g2 = (m6[:, k] for k in range(6))

        lat = _project(x, sh1, sc1, lat_row, pw, tables, TOK_TILE, LAT_OUTS)
        cx = _project(xc, sh1, sc1, ctx_row, pw, None, nctx, CTX_OUTS if want_ctx else CTX_KV_OUTS)

        def dense(g, kind, extras=()):
            return _flash_lat(lat[g + ".qT"], cx[g + ".k"], cx[g + ".vT"], lat[g + ".k"], lat[g + ".vT"],
                              kind, extras, lam_init)

        def ctx_only(g, kind, extras=()):
            return _flash_ctx(cx[g + ".qT"], cx[g + ".k"], cx[g + ".vT"], kind, extras, lam_init)

        a_lat = dense("A", "pair")
        bias = _na_bias_tiles(na_rel_bias[l], rows, NA_TILE)
        b_lat = _na_attention(lat["B.qT"], cx["B.k"], cx["B.vT"], lat["B.k"], lat["B.vT"], bias)
        c_lat = dense("C", "diff", pw["diff"])
        d_lat = dense("D", "mla")
        x1, h2, gates = _out_router((a_lat, b_lat, c_lat, d_lat), x, g1, sh2, sc2, lat_row, pw, TOK_TILE)
        tiles_per_batch = s // MOE_TILE
        x = _moe(h2.reshape(b * s, d), gates.reshape(b * s, LANES), x1.reshape(b * s, d), g2,
                 lambda i: i // tiles_per_batch, pw, MOE_TILE, not want_ctx).reshape(b, s, d)

        if want_ctx:
            a_c = ctx_only("A", "pair")
            b_c = ctx_only("B", "pair")
            c_c = ctx_only("C", "diff", pw["diff"])
            d_c = ctx_only("D", "mla")
            xc1, h2c, gates_c = _out_router((a_c, b_c, c_c, d_c), xc, g1, sh2, sc2, ctx_row, pw, nctx)
            xc = _moe(h2c.reshape(b * nctx, d), gates_c.reshape(b * nctx, LANES),
                      xc1.reshape(b * nctx, d), g2, lambda i: b, pw, tm_ctx, False).reshape(b, nctx, d)
    return x
```

```python
import functools
import math

import numpy as np
import jax
import jax.numpy as jnp
from jax import lax
from jax.experimental import pallas as pl
from jax.experimental.pallas import tpu as pltpu

F32 = jnp.float32
BF16 = jnp.bfloat16

GRID_W = 64
EPS = 1e-6
ROPE_BASE = 10000.0
GQA_HEADS, GQA_KV_HEADS, GQA_DIM = 4, 2, 64
NA_HEADS, NA_DIM, NA_WIN_ROWS, NA_WIN_COLS = 4, 64, 8, 16
DIFF_HEADS, DIFF_QK_DIM, DIFF_V_DIM = 4, 32, 64
MLA_HEADS, MLA_NOPE, MLA_ROPE, MLA_V, MLA_Q_LORA, MLA_KV_LORA = 4, 64, 32, 64, 192, 128
N_EXPERTS, MOE_GROUPS, D_EXPERT = 16, 4, 256
EXPERTS_PER_GROUP = N_EXPERTS // MOE_GROUPS

LANES = 128
SUBLANES = 8
VMEM_LIMIT = 56 * 1024 * 1024

TOK_TILE = 512
NA_TILE = 256
MOE_TILE = 1024
MOD_TILE = 1024
PROJ_SUBTILES = 2
OUT_SUBTILES = 2
NEG = -1e30
LOG2E = math.log2(math.e)

SEG_W = 256
OFF_AQ, OFF_AK, OFF_AV = 0, 256, 512
OFF_BQ, OFF_BK, OFF_BV = 768, 1024, 1280
OFF_CQ, OFF_CK, OFF_CV = 1536, 1792, 2048
OFF_DQ, OFF_DKV, OFF_DPE = 2304, 2560, 2688


def _cparams(sem):
    return pltpu.CompilerParams(dimension_semantics=sem, vmem_limit_bytes=VMEM_LIMIT)


def _lane_iota(shape):
    return lax.broadcasted_iota(jnp.int32, shape, 1)


def _lane_group(shape, width):
    return lax.shift_right_logical(_lane_iota(shape), int(math.log2(width)))


def _sigmoid(x):
    return 1.0 / (1.0 + jnp.exp(-x))


def _mod_kernel(c_ref, w_ref, b_ref, o_ref):
    c = c_ref[...]
    s = c * _sigmoid(c)
    o_ref[0] = jnp.dot(s, w_ref[0], preferred_element_type=F32,
                       precision=lax.Precision.HIGHEST) + b_ref[0]


def _modulation(cc, w_mod, b_mod):
    depth, d, n = w_mod.shape
    rows = cc.shape[0]
    return pl.pallas_call(
        _mod_kernel,
        grid=(depth, n // MOD_TILE),
        in_specs=[
            pl.BlockSpec((rows, d), lambda l, j: (0, 0)),
            pl.BlockSpec((1, d, MOD_TILE), lambda l, j: (l, 0, j)),
            pl.BlockSpec((1, 1, MOD_TILE), lambda l, j: (l, 0, j)),
        ],
        out_specs=pl.BlockSpec((1, rows, MOD_TILE), lambda l, j: (l, 0, j)),
        out_shape=jax.ShapeDtypeStruct((depth, rows, n), F32),
        compiler_params=_cparams(("parallel", "parallel")),
        name="modulation",
    )(cc, w_mod, b_mod.reshape(depth, 1, n))


def _swap_halves(x, half):
    w = x.shape[1]
    lane = _lane_iota(x.shape)
    fwd = pltpu.roll(x, half, 1)
    bwd = pltpu.roll(x, w - half, 1)
    return jnp.where((lane & (2 * half - 1)) < half, bwd, fwd)


def _rope(x, cos, sin, half):
    reps = x.shape[1] // cos.shape[1]
    c = jnp.concatenate([cos] * reps, axis=1) if reps > 1 else cos
    s = jnp.concatenate([sin] * reps, axis=1) if reps > 1 else sin
    return x * c + _swap_halves(x, half) * s


def _group_rms(x, group, count):
    gid = _lane_group(x.shape, group)
    x2 = x * x
    inv = jnp.zeros_like(x)
    for g in range(x.shape[1] // group):
        msk = gid == g
        ms = jnp.sum(jnp.where(msk, x2, 0.0), axis=1, keepdims=True) * (1.0 / count)
        inv = jnp.where(msk, lax.rsqrt(ms + EPS), inv)
    return x * inv


LAT_OUTS = ("A.qT", "A.k", "A.vT", "B.qT", "B.k", "B.vT", "C.qT", "C.k", "C.vT", "D.qT", "D.k", "D.vT")
CTX_KV_OUTS = ("A.k", "A.vT", "B.k", "B.vT", "C.k", "C.vT", "D.k", "D.vT")
CTX_OUTS = CTX_KV_OUTS + ("A.qT", "B.qT", "C.qT", "D.qT")
GROUP_W = {"A": SEG_W, "B": SEG_W, "C": SEG_W, "D": 2 * SEG_W}


def _proj_kernel(*refs, rope, outs):
    (x_ref, sh_ref, sc_ref, gmix_ref, w_ref, wqu_ref, wkk_ref, wkv_ref,
     gq_ref, gk_ref, gcq_ref, gckv_ref) = refs[:12]
    pos = 12
    tabs = refs[pos:pos + 6] if rope else ()
    pos += len(tabs)
    out = dict(zip(outs, refs[pos:pos + len(outs)]))
    tm = x_ref.shape[1]
    parts = PROJ_SUBTILES if tm % (PROJ_SUBTILES * 2 * LANES) == 0 else 1
    sub = tm // parts
    for h in range(parts):
        _proj_subtile(slice(h * sub, (h + 1) * sub), x_ref, sh_ref, sc_ref, gmix_ref, w_ref, wqu_ref,
                      wkk_ref, wkv_ref, gq_ref, gk_ref, gcq_ref, gckv_ref, tabs, out, rope)


def _proj_subtile(rows, x_ref, sh_ref, sc_ref, gmix_ref, w_ref, wqu_ref, wkk_ref, wkv_ref,
                  gq_ref, gk_ref, gcq_ref, gckv_ref, tabs, out, rope):
    r0, sub = rows.start, rows.stop - rows.start
    if rope:
        cg, sg, cd, sd, cm, sm = (t[rows, :] for t in tabs)

    def want(g, t):
        return (g + "." + t) in out or (g + "." + t + "T") in out

    def put(g, t, val):
        if g + "." + t in out:
            out[g + "." + t][0, rows, :] = val.astype(BF16)
        if g + "." + t + "T" in out:
            ref = out[g + "." + t + "T"]
            vt = val.T.astype(BF16)
            cw = ref.shape[3]
            if cw >= sub:
                ref[0, r0 // cw, :, r0 % cw:r0 % cw + sub] = vt
            else:
                for c in range(sub // cw):
                    ref[0, r0 // cw + c] = vt[:, c * cw:(c + 1) * cw]

    x = x_ref[0, rows, :]
    ms = jnp.mean(x * x, axis=-1, keepdims=True)
    h = x * lax.rsqrt(ms + EPS) * gmix_ref[...]
    h = h * (1.0 + sc_ref[0]) + sh_ref[0]
    hb = h.astype(BF16)

    def seg(off, width=SEG_W):
        return jnp.dot(hb, w_ref[:, off:off + width], preferred_element_type=F32)

    cq_raw = seg(OFF_DQ) if want("D", "q") else None
    ckv_raw = seg(OFF_DKV, LANES)
    kpe = seg(OFF_DPE, LANES)

    if want("A", "q"):
        q = _group_rms(seg(OFF_AQ), GQA_DIM, GQA_DIM) * gq_ref[...]
        if rope:
            q = _rope(q, cg, sg, GQA_DIM // 4)
        put("A", "q", q)
    k = _group_rms(seg(OFF_AK), GQA_DIM, GQA_DIM) * gk_ref[...]
    if rope:
        k = _rope(k, cg, sg, GQA_DIM // 4)
    put("A", "k", k)
    put("A", "v", seg(OFF_AV))

    if want("B", "q"):
        put("B", "q", seg(OFF_BQ) * (NA_DIM ** -0.5 * LOG2E))
    put("B", "k", seg(OFF_BK))
    put("B", "v", seg(OFF_BV))

    if want("C", "q"):
        q = seg(OFF_CQ) * (DIFF_QK_DIM ** -0.5 * LOG2E)
        if rope:
            q = _rope(q, cd, sd, DIFF_QK_DIM // 4)
        put("C", "q", q)
    k = seg(OFF_CK)
    if rope:
        k = _rope(k, cd, sd, DIFF_QK_DIM // 4)
    put("C", "k", k)
    put("C", "v", seg(OFF_CV))

    if want("D", "q"):
        cq = _group_rms(cq_raw, SEG_W, MLA_Q_LORA) * gcq_ref[...]
        q = jnp.dot(cq.astype(BF16), wqu_ref[...], preferred_element_type=F32)
        q = q * ((MLA_NOPE + MLA_ROPE) ** -0.5 * LOG2E)
        if rope:
            q = _rope(q, cm, sm, MLA_ROPE // 4)
        put("D", "q", q)
    ckv = _group_rms(ckv_raw, LANES, MLA_KV_LORA) * gckv_ref[...]
    ckvb = ckv.astype(BF16)
    k = jnp.dot(ckvb, wkk_ref[...], preferred_element_type=F32)
    if rope:
        kpe = _rope(kpe, cm, sm, MLA_ROPE // 4)
    put("D", "k", k + jnp.concatenate([kpe] * MLA_HEADS, axis=1))
    put("D", "v", jnp.dot(ckvb, wkv_ref[...], preferred_element_type=F32))


def _project(x, sh, sc, row_of_batch, pw, tables, tm, outs):
    b, n, d = x.shape
    nt = n // tm
    rope = tables is not None
    full = lambda a: pl.BlockSpec(a.shape, lambda bi, i: (0,) * a.ndim)
    mod_spec = pl.BlockSpec((1, 1, d), lambda bi, i: (row_of_batch(bi), 0, 0))
    ins = [x, sh, sc, pw["gmix"], pw["w_in"], pw["wq_up"], pw["wkv_k"], pw["wkv_v"],
           pw["gq"], pw["gk"], pw["gcq"], pw["gckv"]]
    in_specs = [pl.BlockSpec((1, tm, d), lambda bi, i: (bi, i, 0)), mod_spec, mod_spec]
    in_specs += [full(a) for a in ins[3:]]
    if rope:
        ins += list(tables)
        in_specs += [pl.BlockSpec((tm, LANES), lambda bi, i: (i, 0)) for _ in tables]
    shapes, specs = [], []
    for name in outs:
        g, t = name.split(".")
        w = SEG_W if t[0] == "v" else GROUP_W[g]
        if t.endswith("T"):
            cw = min(tm, NA_TILE) if g == "B" else tm
            shapes.append(jax.ShapeDtypeStruct((b, n // cw, w, cw), BF16))
            specs.append(pl.BlockSpec((1, tm // cw, w, cw), lambda bi, i: (bi, i, 0, 0)))
        else:
            shapes.append(jax.ShapeDtypeStruct((b, n, w), BF16))
            specs.append(pl.BlockSpec((1, tm, w), lambda bi, i: (bi, i, 0)))
    res = pl.pallas_call(
        functools.partial(_proj_kernel, rope=rope, outs=tuple(outs)),
        grid=(b, nt),
        in_specs=in_specs,
        out_specs=specs,
        out_shape=shapes,
        compiler_params=_cparams(("parallel", "parallel")),
        name="proj_rope" if rope else "proj_ctx",
    )(*ins)
    return dict(zip(outs, res))


HEAD_V = 64


def _tile_lanes(x, width):
    reps = width // x.shape[1]
    return jnp.tile(x, (1, reps)) if reps > 1 else x


def with_ones(vt):
    rows = lax.broadcasted_iota(jnp.int32, (LANES - HEAD_V, vt.shape[1]), 0)
    ones = jnp.where(rows == 0, 1.0, 0.0).astype(BF16)
    return jnp.concatenate([vt, ones], axis=0)


def _map_slices(kind):
    def k_lanes(m):
        return slice(m * LANES, (m + 1) * LANES) if kind == "mla" else slice(None)

    def v_rows(m):
        hv = m // 2 if kind == "diff" else m
        return slice(hv * HEAD_V, (hv + 1) * HEAD_V)

    return k_lanes, v_rows


def _query_slabs(qt_ref, kind, nm):
    qv = jnp.concatenate([qt_ref[0, c] for c in range(qt_ref.shape[1])], axis=1)
    if kind == "mla":
        return [qv[m * LANES:(m + 1) * LANES, :] for m in range(nm)]
    rows = lax.broadcasted_iota(jnp.int32, qv.shape, 0)
    grp = lax.shift_right_logical(rows, int(math.log2(LANES // nm)))
    return [jnp.where(grp == m, qv, jnp.zeros_like(qv)) for m in range(nm)]


def _finish_heads(accs, kind, lam_init, diff_refs):
    outs = [a[0:HEAD_V] / a[HEAD_V:HEAD_V + 1] for a in accs]
    if kind == "diff":
        lq1_ref, lk1_ref, lq2_ref, lk2_ref, sub_ref = diff_refs
        lam = (jnp.exp(jnp.sum(lq1_ref[...] * lk1_ref[...], axis=1, keepdims=True))
               - jnp.exp(jnp.sum(lq2_ref[...] * lk2_ref[...], axis=1, keepdims=True))
               + lam_init)
        heads = []
        for hd in range(2):
            dlt = outs[2 * hd] - lam * outs[2 * hd + 1]
            ms2 = jnp.mean(dlt * dlt, axis=0, keepdims=True)
            heads.append(dlt * lax.rsqrt(ms2 + EPS) * sub_ref[...] * (1.0 - lam_init))
        outs = heads
    return jnp.concatenate(outs, axis=0).T


def _flash_ctx_kernel(*refs, kind, lam_init):
    qt_ref, kc_ref, vct_ref = refs[:3]
    nd = 5 if kind == "diff" else 0
    diff_refs, o_ref = refs[3:3 + nd], refs[3 + nd]
    nm = 4 if kind == "diff" else 2
    k_lanes, v_rows = _map_slices(kind)
    accs = []
    for m, qm in enumerate(_query_slabs(qt_ref, kind, nm)):
        st = jnp.dot(kc_ref[0, :, k_lanes(m)], qm, preferred_element_type=F32)
        p = jnp.exp2(st - jnp.max(st, axis=0, keepdims=True)).astype(BF16)
        accs.append(jnp.dot(with_ones(vct_ref[0, 0, v_rows(m), :]), p, preferred_element_type=F32))
    o_ref[0] = _finish_heads(accs, kind, lam_init, diff_refs).astype(o_ref.dtype)


def _flash_lat_kernel(*refs, kind, lam_init):
    qt_ref, kc_ref, vct_ref, k_ref, vt_ref = refs[:5]
    nd = 5 if kind == "diff" else 0
    diff_refs, o_ref = refs[5:5 + nd], refs[5 + nd]
    qs_e, acc_e, s_e, qs_l, acc_l, s_l, m_ref, mc_ref = refs[6 + nd:14 + nd]
    early = (qs_e, acc_e, s_e, 0, False)
    late = (qs_l, acc_l, s_l, vt_ref.shape[1] // 2, True)

    nm = qs_e.shape[0]
    nk, tk = vt_ref.shape[1], vt_ref.shape[3]
    half = nk // 2
    iters = half // 2
    g = pl.program_id(2)
    nq = pl.num_programs(2) - 1
    k_lanes, v_rows = _map_slices(kind)

    def produce(tile, slot, j, m):
        qs_ref, _, s_ref = tile[:3]
        rows = pl.ds(pl.multiple_of(j * tk, tk), tk)
        st = jnp.dot(k_ref[0, rows, k_lanes(m)], qs_ref[m], preferred_element_type=F32)
        s_ref[slot, m] = st
        return jnp.max(st, axis=0, keepdims=True)

    def consume(tile, slot, j, m, m_prev, m_chunk):
        _, acc_ref, s_ref = tile[:3]
        m_new = jnp.maximum(m_prev, m_chunk)
        alpha = jnp.exp2(m_prev - m_new)
        p = jnp.exp2(s_ref[slot, m] - m_new).astype(BF16)
        acc_ref[m] = alpha * acc_ref[m] + jnp.dot(
            with_ones(vt_ref[0, j, v_rows(m), :]), p, preferred_element_type=F32)
        return m_new

    def iteration(i, carries, tiles, last):
        maps = range(nm)
        j0s = [tile[3] + 2 * i for tile in tiles]
        c1 = [[None] * nm for _ in tiles]
        m1 = [[None] * nm for _ in tiles]
        c2 = [[None] * nm for _ in tiles]
        m2 = [[None] * nm for _ in tiles]
        for t, (tile, j0, c) in enumerate(zip(tiles, j0s, carries)):
            for m in maps:
                c1[t][m] = produce(tile, 1, j0 + 1, m)
                m1[t][m] = consume(tile, 0, j0, m, c[0][m], c[1][m])
        for t, (tile, j0) in enumerate(zip(tiles, j0s)):
            for m in maps:
                c2[t][m] = c1[t][m] if (last and tile[4]) else produce(tile, 0, j0 + 2, m)
                m2[t][m] = consume(tile, 1, j0 + 1, m, m1[t][m], c1[t][m])
        return tuple((tuple(a), tuple(b)) for a, b in zip(m2, c2))

    def run(carries, tiles):
        carries = iteration(0, carries, tiles, iters == 1)
        if iters > 2:
            carries = lax.fori_loop(1, iters - 1, lambda i, c: iteration(i, c, tiles, False), carries)
        if iters > 1:
            carries = iteration(iters - 1, carries, tiles, True)
        return carries

    def start_early():
        for m, qm in enumerate(_query_slabs(qt_ref, kind, nm)):
            qs_e[m] = qm
        sts = [jnp.dot(kc_ref[0, :, k_lanes(m)], qs_e[m], preferred_element_type=F32)
               for m in range(nm)]
        ms = tuple(jnp.max(st, axis=0, keepdims=True) for st in sts)
        first = tuple(produce(early, 0, 0, m) for m in range(nm))

        def init_from_context():
            for m in range(nm):
                p = jnp.exp2(sts[m] - ms[m]).astype(BF16)
                acc_e[m] = jnp.dot(with_ones(vct_ref[0, 0, v_rows(m), :]), p,
                                   preferred_element_type=F32)

        return (ms, first), init_from_context

    def load_late():
        return (tuple(m_ref[m] for m in range(nm)), tuple(mc_ref[m] for m in range(nm)))

    def hand_over(carry):
        qs_l[...] = qs_e[...]
        acc_l[...] = acc_e[...]
        s_l[0] = s_e[0]
        for m in range(nm):
            m_ref[m] = carry[0][m]
            mc_ref[m] = carry[1][m]

    def finish_late():
        accs = [acc_l[m] for m in range(nm)]
        o_ref[0] = _finish_heads(accs, kind, lam_init, diff_refs).astype(o_ref.dtype)

    @pl.when(g == 0)
    def _():
        carry, init_from_context = start_early()
        init_from_context()
        hand_over(run((carry,), (early,))[0])

    @pl.when((g > 0) & (g < nq))
    def _():
        carry_l = load_late()
        carry_e, init_from_context = start_early()
        init_from_context()
        carry_l, carry_e = run((carry_l, carry_e), (late, early))
        finish_late()
        hand_over(carry_e)

    @pl.when(g == nq)
    def _():
        run((load_late(),), (late,))
        finish_late()


def _flash_specs(qt, kc, vct, kind, extras):
    b, nqc, wtot, cw = qt.shape
    wq = 2 * LANES if kind == "mla" else LANES
    c = kc.shape[1]
    extra_specs = [pl.BlockSpec(e.shape, lambda bi, p, i: (0, 0)) for e in extras]
    kv_specs = [pl.BlockSpec((1, c, wq), lambda bi, p, i: (bi, 0, p)),
                pl.BlockSpec((1, 1, LANES, c), lambda bi, p, i: (bi, 0, p, 0))]
    return b, nqc, wtot // wq, wq, cw, kv_specs, extra_specs


def _flash_ctx(qt, kc, vct, kind, extras=(), lam_init=0.0):
    b, nqc, npair, wq, cw, kv_specs, extra_specs = _flash_specs(qt, kc, vct, kind, extras)
    return pl.pallas_call(
        functools.partial(_flash_ctx_kernel, kind=kind, lam_init=lam_init),
        grid=(b, npair, nqc),
        in_specs=[pl.BlockSpec((1, 1, wq, cw), lambda bi, p, i: (bi, i, p, 0))] + kv_specs + extra_specs,
        out_specs=pl.BlockSpec((1, cw, LANES), lambda bi, p, i: (bi, i, p)),
        out_shape=jax.ShapeDtypeStruct((b, nqc * cw, npair * LANES), BF16),
        compiler_params=_cparams(("parallel", "parallel", "parallel")),
        name="flash_" + kind + "_ctx",
    )(qt, kc, vct, *extras)


def _flash_lat(qt, kc, vct, k, vt, kind, extras=(), lam_init=0.0):
    b, nq, npair, wq, tq, kv_specs, extra_specs = _flash_specs(qt, kc, vct, kind, extras)
    nm = 4 if kind == "diff" else 2
    nk, tk = vt.shape[1], vt.shape[3]
    assert nk % 4 == 0, "two phases of key chunks, processed in pairs"
    return pl.pallas_call(
        functools.partial(_flash_lat_kernel, kind=kind, lam_init=lam_init),
        grid=(b, npair, nq + 1),
        in_specs=[pl.BlockSpec((1, 1, wq, tq), lambda bi, p, g: (bi, jnp.minimum(g, nq - 1), p, 0))]
        + kv_specs
        + [pl.BlockSpec((1, k.shape[1], wq), lambda bi, p, g: (bi, 0, p)),
           pl.BlockSpec((1, nk, LANES, tk), lambda bi, p, g: (bi, 0, p, 0))]
        + extra_specs,
        out_specs=pl.BlockSpec((1, tq, LANES), lambda bi, p, g: (bi, jnp.maximum(g - 1, 0), p)),
        out_shape=jax.ShapeDtypeStruct((b, nq * tq, npair * LANES), BF16),
        scratch_shapes=2 * [
            pltpu.VMEM((nm, LANES, tq), BF16),
            pltpu.VMEM((nm, LANES, tq), F32),
            pltpu.VMEM((2, nm, tk, tq), F32),
        ] + 2 * [pltpu.VMEM((nm, 1, tq), F32)],
        compiler_params=_cparams(("parallel", "parallel", "arbitrary")),
        name="flash_" + kind,
    )(qt, kc, vct, k, vt, *extras)


NA_QCHUNKS = 2
NA_CHUNKS = NA_QCHUNKS + 2


def _na_kernel(qt_ref, k_ref, vt_ref, kc_ref, vct_ref, bias_ref, o_ref):
    nkc, tk = vt_ref.shape[1], vt_ref.shape[3]
    i = pl.program_id(2)
    c0 = jnp.clip(i * NA_QCHUNKS - 1, 0, nkc - NA_CHUNKS)
    qv = jnp.concatenate([qt_ref[0, c] for c in range(NA_QCHUNKS)], axis=1)
    rows = lax.broadcasted_iota(jnp.int32, qv.shape, 0)
    scores = []
    for m in range(2):
        own = lax.shift_right_logical(rows, int(math.log2(LANES // 2))) == m
        qm = jnp.where(own, qv, jnp.zeros_like(qv))
        ss = []
        for j in range(NA_CHUNKS):
            kj = k_ref[0, pl.ds(pl.multiple_of((c0 + j) * tk, tk), tk), :]
            ss.append(jnp.dot(kj, qm, preferred_element_type=F32) + bias_ref[0, m, j])
        scores.append((ss, jnp.dot(kc_ref[0], qm, preferred_element_type=F32)))
    outs = []
    for m, (ss, sc) in enumerate(scores):
        v_rows = slice(m * HEAD_V, (m + 1) * HEAD_V)
        mx = jnp.max(sc, axis=0, keepdims=True)
        for st in ss:
            mx = jnp.maximum(mx, jnp.max(st, axis=0, keepdims=True))
        acc = jnp.dot(with_ones(vct_ref[0, 0, v_rows, :]), jnp.exp2(sc - mx).astype(BF16),
                      preferred_element_type=F32)
        for j, st in enumerate(ss):
            acc = acc + jnp.dot(with_ones(vt_ref[0, c0 + j, v_rows, :]),
                                jnp.exp2(st - mx).astype(BF16), preferred_element_type=F32)
        outs.append(acc[0:HEAD_V] / acc[HEAD_V:HEAD_V + 1])
    o_ref[0] = jnp.concatenate(outs, axis=0).T.astype(o_ref.dtype)


def _na_bias_tiles(rel_bias, rows, tk):
    qrows = NA_QCHUNKS * tk // GRID_W
    nblk = rows // qrows
    nk = rows * GRID_W // tk
    krows = NA_CHUNKS * tk // GRID_W
    kh = min(NA_WIN_ROWS, rows)
    kw = NA_WIN_COLS
    blocks = np.array([0, 1, nblk - 1])
    c0 = np.clip(blocks * NA_QCHUNKS - 1, 0, nk - NA_CHUNKS)
    r = blocks[:, None] * qrows + np.arange(qrows)[None, :]
    kr = c0[:, None] * (tk // GRID_W) + np.arange(krows)[None, :]
    r0 = np.clip(r - kh // 2, 0, rows - kh)
    dr = kr[:, None, :] - r[:, :, None]
    valid_r = (kr[:, None, :] >= r0[:, :, None]) & (kr[:, None, :] < r0[:, :, None] + kh)
    cols = np.arange(GRID_W)
    cstart = np.clip(cols - kw // 2, 0, GRID_W - kw)
    dc = cols[None, :] - cols[:, None]
    valid_c = (cols[None, :] >= cstart[:, None]) & (cols[None, :] < cstart[:, None] + kw)
    ndc = 2 * NA_WIN_COLS - 1
    onehot = ((dc[None] + NA_WIN_COLS - 1) == np.arange(ndc)[:, None, None]) & valid_c[None]
    onehot = np.concatenate([onehot, ~valid_c[None]], axis=0)
    dr_idx = np.clip(dr + NA_WIN_ROWS - 1, 0, 2 * NA_WIN_ROWS - 2)
    g1 = rel_bias[:, dr_idx, :] * LOG2E
    g1 = jnp.concatenate([g1, jnp.full(g1.shape[:-1] + (1,), NEG, F32)], axis=-1)
    g1 = jnp.where(jnp.asarray(valid_r)[None, :, :, :, None], g1, NEG)
    tiles = jnp.einsum("htabd,dcx->thbxac", g1, jnp.asarray(onehot, F32),
                       precision=lax.Precision.HIGHEST)
    return tiles.reshape(3, rel_bias.shape[0], NA_CHUNKS, tk, qrows * GRID_W)


def _na_attention(qt, kc, vct, k, vt, bias):
    b, nqc, wtot, cw = qt.shape
    nblk, tq = nqc // NA_QCHUNKS, NA_QCHUNKS * cw
    npair = wtot // LANES
    nkc, tk = vt.shape[1], vt.shape[3]
    c = kc.shape[1]
    n = k.shape[1]

    def bias_map(bi, p, i):
        return (jnp.where(i == 0, 0, jnp.where(i == nblk - 1, 2, 1)), p, 0, 0, 0)

    return pl.pallas_call(
        _na_kernel,
        grid=(b, npair, nblk),
        in_specs=[
            pl.BlockSpec((1, NA_QCHUNKS, LANES, cw), lambda bi, p, i: (bi, i, p, 0)),
            pl.BlockSpec((1, n, LANES), lambda bi, p, i: (bi, 0, p)),
            pl.BlockSpec((1, nkc, LANES, tk), lambda bi, p, i: (bi, 0, p, 0)),
            pl.BlockSpec((1, c, LANES), lambda bi, p, i: (bi, 0, p)),
            pl.BlockSpec((1, 1, LANES, c), lambda bi, p, i: (bi, 0, p, 0)),
            pl.BlockSpec((1, 2, NA_CHUNKS, tk, tq), bias_map),
        ],
        out_specs=pl.BlockSpec((1, tq, LANES), lambda bi, p, i: (bi, i, p)),
        out_shape=jax.ShapeDtypeStruct((b, nblk * tq, wtot), BF16),
        compiler_params=_cparams(("parallel", "parallel", "parallel")),
        name="na_attention",
    )(qt, k, vt, kc, vct, bias)


def _first_index_of_max(vals, row, big):
    mx = jnp.max(vals, axis=0, keepdims=True)
    idx = jnp.min(jnp.where(vals == mx, row, big), axis=0, keepdims=True)
    return mx, idx


def _router_gates(scores_t, sel_t):
    irow = lax.broadcasted_iota(jnp.int32, sel_t.shape, 0)
    row = irow.astype(F32)
    grp = lax.shift_right_logical(irow, int(math.log2(EXPERTS_PER_GROUP))).astype(F32)
    big = float(N_EXPERTS)
    best = None
    for g in range(MOE_GROUPS):
        vals = jnp.where(grp == float(g), sel_t, -jnp.inf)
        m1, i1 = _first_index_of_max(vals, row, big)
        m2 = jnp.max(jnp.where(row == i1, -jnp.inf, vals), axis=0, keepdims=True)
        gs = m1 + m2
        if best is None:
            best, bi = gs, jnp.zeros(gs.shape, F32)
        else:
            better = gs > best
            bi = jnp.where(better, float(g), bi)
            best = jnp.where(better, gs, best)
    msel = jnp.where(grp == bi, sel_t, -jnp.inf)
    _, i1 = _first_index_of_max(msel, row, big)
    msel2 = jnp.where(row == i1, -jnp.inf, msel)
    _, i2 = _first_index_of_max(msel2, row, big)
    w = jnp.where((row == i1) | (row == i2), scores_t, 0.0)
    return w / jnp.sum(w, axis=0, keepdims=True)


def _out_kernel(a_ref, b_ref, c_ref, d_ref, x_ref, g1_ref, sh_ref, sc_ref, gffn_ref,
                w_ref, rwh_ref, rwl_ref, rb_ref, xo_ref, h2_ref, gate_ref):
    tm = x_ref.shape[1]
    parts = OUT_SUBTILES if tm % (OUT_SUBTILES * LANES) == 0 else 1
    sub = tm // parts
    ys = []
    for h in range(parts):
        rows = slice(h * sub, (h + 1) * sub)
        y = None
        for gi, r in enumerate((a_ref, b_ref, c_ref, d_ref)):
            part = jnp.dot(r[0, rows, :], w_ref[gi * SEG_W:(gi + 1) * SEG_W, :],
                           preferred_element_type=F32)
            y = part if y is None else y + part
        ys.append(y)
    for h in range(parts):
        rows = slice(h * sub, (h + 1) * sub)
        xn = x_ref[0, rows, :] + g1_ref[0] * ys[h]
        xo_ref[0, rows, :] = xn
        ms = jnp.mean(xn * xn, axis=-1, keepdims=True)
        h2 = xn * lax.rsqrt(ms + EPS) * gffn_ref[...]
        h2 = h2 * (1.0 + sc_ref[0]) + sh_ref[0]
        hi = h2.astype(BF16)
        h2_ref[0, rows, :] = hi
        lo = (h2 - hi.astype(F32)).astype(BF16)
        logits = (jnp.dot(hi, rwh_ref[...], preferred_element_type=F32)
                  + jnp.dot(lo, rwh_ref[...], preferred_element_type=F32)
                  + jnp.dot(hi, rwl_ref[...], preferred_element_type=F32))
        scores_t = _sigmoid(logits).T[:N_EXPERTS]
        sel_t = scores_t + _tile_lanes(rb_ref[...], sub)
        gates_t = _router_gates(scores_t, sel_t)
        pad = jnp.zeros((LANES - N_EXPERTS, sub), F32)
        gate_ref[0, rows, :] = jnp.concatenate([gates_t, pad], axis=0).T


def _out_router(parts, x, g1, sh2, sc2, row_of_batch, pw, tm):
    b, n, d = x.shape
    full = lambda a: pl.BlockSpec(a.shape, lambda bi, i: (0,) * a.ndim)
    tok = lambda w: pl.BlockSpec((1, tm, w), lambda bi, i: (bi, i, 0))
    mod_spec = pl.BlockSpec((1, 1, d), lambda bi, i: (row_of_batch(bi), 0, 0))
    consts = [pw["gffn"], pw["w_out"], pw["rw_hi"], pw["rw_lo"], pw["rb"]]
    return pl.pallas_call(
        _out_kernel,
        grid=(b, n // tm),
        in_specs=[tok(SEG_W)] * 4 + [tok(d), mod_spec, mod_spec, mod_spec] + [full(a) for a in consts],
        out_specs=[tok(d), tok(d), tok(LANES)],
        out_shape=[jax.ShapeDtypeStruct((b, n, d), F32), jax.ShapeDtypeStruct((b, n, d), BF16),
                   jax.ShapeDtypeStruct((b, n, LANES), F32)],
        compiler_params=_cparams(("parallel", "parallel")),
        name="out_router",
    )(*parts, x, g1, sh2, sc2, *consts)


MOE_EXPERTS_PER_STEP = 4


def _swiglu_act(gu, scale=None):
    de = gu.shape[1] // 2
    g, u = gu[:, :de], gu[:, de:]
    act = (g * _sigmoid(g)) * u
    return (act if scale is None else act * scale).astype(BF16)


def _moe_kernel(h_ref, gate_ref, x_ref, g2_ref, wgu_ref, wd_ref, sgu_ref, sd_ref, fin_ref,
                o_ref, acc_ref, *, final_norm):
    step = pl.program_id(1)
    k = wgu_ref.shape[0]
    hb = h_ref[...]
    gates = gate_ref[...]
    lane = _lane_iota(gates.shape)
    acts = []
    for i in range(k):
        gu = jnp.dot(hb, wgu_ref[i], preferred_element_type=F32)
        col = jnp.sum(jnp.where(lane == step * k + i, gates, 0.0), axis=1, keepdims=True)
        acts.append(_swiglu_act(gu, col))
    wd = wd_ref[...]
    contrib = jnp.dot(jnp.concatenate(acts, axis=1), wd.reshape(k * wd.shape[1], wd.shape[2]),
                      preferred_element_type=F32)

    @pl.when(step == 0)
    def _():
        shared = _swiglu_act(jnp.dot(hb, sgu_ref[...], preferred_element_type=F32))
        acc_ref[...] = contrib + jnp.dot(shared, sd_ref[...], preferred_element_type=F32)

    @pl.when(step > 0)
    def _():
        acc_ref[...] += contrib

    @pl.when(step == pl.num_programs(1) - 1)
    def _():
        y = x_ref[...] + g2_ref[0] * acc_ref[...]
        if final_norm:
            ms = jnp.mean(y * y, axis=-1, keepdims=True)
            y = y * lax.rsqrt(ms + EPS) * fin_ref[...]
        o_ref[...] = y


def _moe(h2, gates, x, g2, row_of_tile, pw, tm, final_norm):
    t, d = x.shape
    ne, _, de2 = pw["wgu"].shape
    k = MOE_EXPERTS_PER_STEP
    full = lambda a: pl.BlockSpec(a.shape, lambda i, e: (0,) * a.ndim)
    return pl.pallas_call(
        functools.partial(_moe_kernel, final_norm=final_norm),
        grid=(t // tm, ne // k),
        in_specs=[
            pl.BlockSpec((tm, d), lambda i, e: (i, 0)),
            pl.BlockSpec((tm, LANES), lambda i, e: (i, 0)),
            pl.BlockSpec((tm, d), lambda i, e: (i, 0)),
            pl.BlockSpec((1, 1, d), lambda i, e: (row_of_tile(i), 0, 0)),
            pl.BlockSpec((k, d, de2), lambda i, e: (e, 0, 0)),
            pl.BlockSpec((k, de2 // 2, d), lambda i, e: (e, 0, 0)),
            full(pw["sgu"]), full(pw["sd"]), full(pw["fin"]),
        ],
        out_specs=pl.BlockSpec((tm, d), lambda i, e: (i, 0)),
        out_shape=jax.ShapeDtypeStruct((t, d), F32),
        scratch_shapes=[pltpu.VMEM((tm, d), F32)],
        compiler_params=_cparams(("parallel", "arbitrary")),
        name="moe_final" if final_norm else "moe",
    )(h2, gates, x, g2, pw["wgu"], pw["wd"], pw["sgu"], pw["sd"], pw["fin"])


def _axial_angles(n_tok, rot_dim):
    half = rot_dim // 2
    freqs = ROPE_BASE ** (-jnp.arange(0, half, 2, dtype=F32) / half)
    t = jnp.arange(n_tok, dtype=jnp.int32)
    row = (t // GRID_W).astype(F32)
    col = (t % GRID_W).astype(F32)
    return row[:, None] * freqs, col[:, None] * freqs


def _rope_table(n_tok, rot_dim):
    ar, ac = _axial_angles(n_tok, rot_dim)
    cos = jnp.concatenate([jnp.cos(ar)] * 2 + [jnp.cos(ac)] * 2, axis=1)
    sin = jnp.concatenate([-jnp.sin(ar), jnp.sin(ar), -jnp.sin(ac), jnp.sin(ac)], axis=1)
    return cos, sin


def _rope_tables(n_tok):
    cg, sg = _rope_table(n_tok, GQA_DIM)
    cd, sd = _rope_table(n_tok, DIFF_QK_DIM)
    cm32, sm32 = _rope_table(n_tok, MLA_ROPE)
    ones = jnp.ones((n_tok, MLA_NOPE), F32)
    zeros = jnp.zeros((n_tok, MLA_NOPE), F32)
    pad = LANES - MLA_NOPE - MLA_ROPE
    cm = jnp.concatenate([ones, cm32, ones[:, :pad]], axis=1)
    sm = jnp.concatenate([zeros, sm32, zeros[:, :pad]], axis=1)
    return (jnp.tile(cg, (1, LANES // GQA_DIM)), jnp.tile(sg, (1, LANES // GQA_DIM)),
            jnp.tile(cd, (1, LANES // DIFF_QK_DIM)), jnp.tile(sd, (1, LANES // DIFF_QK_DIM)),
            cm, sm)


def _pack_layer(l, p):
    d = p["w_in"].shape[1]
    w = p["w_in"][l]
    o = np.cumsum([0, 256, 128, 128, 256, 256, 256, 256, 256, 256, MLA_Q_LORA, MLA_KV_LORA, MLA_ROPE])
    col = lambda i: w[:, int(o[i]):int(o[i + 1])]
    dup = lambda a: jnp.concatenate([a[:, :GQA_DIM], a[:, :GQA_DIM], a[:, GQA_DIM:], a[:, GQA_DIM:]], axis=1)
    z = lambda n: jnp.zeros((d, n), F32)
    w_in = jnp.concatenate(
        [col(0), dup(col(1)), dup(col(2)), col(3), col(4), col(5), col(6), col(7), col(8),
         col(9), z(SEG_W - MLA_Q_LORA), col(10),
         z(MLA_NOPE), col(11), z(LANES - MLA_NOPE - MLA_ROPE)], axis=1).astype(BF16)
    dq = MLA_NOPE + MLA_ROPE
    wqu = p["mla_w_q_up"][l].reshape(MLA_Q_LORA, MLA_HEADS, dq)
    wqu = jnp.pad(wqu, ((0, SEG_W - MLA_Q_LORA), (0, 0), (0, LANES - dq)))
    wkv = p["mla_w_kv_up"][l].reshape(MLA_KV_LORA, MLA_HEADS, MLA_NOPE + MLA_V)
    wkk = jnp.pad(wkv[:, :, :MLA_NOPE], ((0, 0), (0, 0), (0, LANES - MLA_NOPE)))
    e = p["moe_w_gate"].shape[1]
    rw = jnp.pad(p["router_w"], ((0, 0), (0, LANES - e)))
    rw_hi = rw.astype(BF16)
    rw_lo = (rw - rw_hi.astype(F32)).astype(BF16)
    return {
        "gmix": p["norm_mix"][l][None], "gffn": p["norm_ffn"][l][None],
        "w_in": w_in,
        "wq_up": wqu.reshape(SEG_W, MLA_HEADS * LANES).astype(BF16),
        "wkv_k": wkk.reshape(MLA_KV_LORA, MLA_HEADS * LANES).astype(BF16),
        "wkv_v": wkv[:, :, MLA_NOPE:].reshape(MLA_KV_LORA, MLA_HEADS * MLA_V).astype(BF16),
        "gq": (jnp.tile(p["gqa_q_norm"][l], GQA_HEADS) * (GQA_DIM ** -0.5 * LOG2E))[None],
        "gk": jnp.tile(p["gqa_k_norm"][l], GQA_HEADS)[None],
        "gcq": jnp.pad(p["mla_q_norm"][l], (0, SEG_W - MLA_Q_LORA))[None],
        "gckv": p["mla_kv_norm"][l][None],
        "w_out": p["w_out"][l].astype(BF16),
        "rw_hi": rw_hi, "rw_lo": rw_lo,
        "rb": jnp.broadcast_to(p["router_b"][:, None], (e, LANES)),
        "wgu": jnp.concatenate([p["moe_w_gate"][l], p["moe_w_up"][l]], axis=2).astype(BF16),
        "wd": p["moe_w_down"][l].astype(BF16),
        "sgu": jnp.concatenate([p["shared_w_gate"][l], p["shared_w_up"][l]], axis=1).astype(BF16),
        "sd": p["shared_w_down"][l].astype(BF16),
        "fin": p["final_norm"][None],
        "diff": (p["diff_lq1"][l][None], p["diff_lk1"][l][None], p["diff_lq2"][l][None],
                 p["diff_lk2"][l][None], p["diff_subln"][l][:, None]),
    }


def kernel(x, c, ctx, c_ctx, w_mod, b_mod, norm_mix, norm_ffn, w_in, w_out, gqa_q_norm, gqa_k_norm,
           na_rel_bias, diff_lq1, diff_lk1, diff_lq2, diff_lk2, diff_subln, mla_q_norm, mla_w_q_up,
           mla_kv_norm, mla_w_kv_up, router_w, router_b, moe_w_gate, moe_w_up, moe_w_down,
           shared_w_gate, shared_w_up, shared_w_down, final_norm):
    p = dict(w_in=w_in, norm_mix=norm_mix, norm_ffn=norm_ffn, w_out=w_out, gqa_q_norm=gqa_q_norm,
             gqa_k_norm=gqa_k_norm, diff_lq1=diff_lq1, diff_lk1=diff_lk1, diff_lq2=diff_lq2,
             diff_lk2=diff_lk2, diff_subln=diff_subln, mla_q_norm=mla_q_norm, mla_w_q_up=mla_w_q_up,
             mla_kv_norm=mla_kv_norm, mla_w_kv_up=mla_w_kv_up, router_w=router_w, router_b=router_b,
             moe_w_gate=moe_w_gate, moe_w_up=moe_w_up, moe_w_down=moe_w_down,
             shared_w_gate=shared_w_gate, shared_w_up=shared_w_up, shared_w_down=shared_w_down,
             final_norm=final_norm)
    b, s, d = x.shape
    nctx = ctx.shape[1]
    depth = w_mod.shape[0]
    rows = s // GRID_W
    assert s % TOK_TILE == 0 and s // NA_TILE >= NA_CHUNKS + NA_QCHUNKS
    assert nctx % LANES == 0 and (b * s) % MOE_TILE == 0 and s % MOE_TILE == 0

    mrows = -(-(b + 1) // SUBLANES) * SUBLANES
    cc = jnp.zeros((mrows, d), F32).at[:b].set(c).at[b].set(c_ctx)
    mod = _modulation(cc, w_mod, b_mod)
    tables = _rope_tables(s)
    lat_row = lambda bi: bi
    ctx_row = lambda bi: b
    tm_ctx = b * nctx if b * nctx <= MOE_TILE else nctx

    xc = ctx
    for l in range(depth):
        want_ctx = l < depth - 1
        lam_init = 0.8 - 0.6 * math.exp(-0.3 * l)
        pw = _pack_layer(l, p)
        m6 = mod[l].reshape(mrows, 6, 1, d)
        sh1, sc1, g1, sh2, sc2, g2 = (m6[:, k] for k in range(6))

        lat = _project(x, sh1, sc1, lat_row, pw, tables, TOK_TILE, LAT_OUTS)
        cx = _project(xc, sh1, sc1, ctx_row, pw, None, nctx, CTX_OUTS if want_ctx else CTX_KV_OUTS)

        def dense(g, kind, extras=()):
            return _flash_lat(lat[g + ".qT"], cx[g + ".k"], cx[g + ".vT"], lat[g + ".k"], lat[g + ".vT"],
                              kind, extras, lam_init)

        def ctx_only(g, kind, extras=()):
            return _flash_ctx(cx[g + ".qT"], cx[g + ".k"], cx[g + ".vT"], kind, extras, lam_init)

        a_lat = dense("A", "pair")
        bias = _na_bias_tiles(na_rel_bias[l], rows, NA_TILE)
        b_lat = _na_attention(lat["B.qT"], cx["B.k"], cx["B.vT"], lat["B.k"], lat["B.vT"], bias)
        c_lat = dense("C", "diff", pw["diff"])
        d_lat = dense("D", "mla")
        x1, h2, gates = _out_router((a_lat, b_lat, c_lat, d_lat), x, g1, sh2, sc2, lat_row, pw, TOK_TILE)
        tiles_per_batch = s // MOE_TILE
        x = _moe(h2.reshape(b * s, d), gates.reshape(b * s, LANES), x1.reshape(b * s, d), g2,
                 lambda i: i // tiles_per_batch, pw, MOE_TILE, not want_ctx).reshape(b, s, d)

        if want_ctx:
            a_c = ctx_only("A", "pair")
            b_c = ctx_only("B", "pair")
            c_c = ctx_only("C", "diff", pw["diff"])
            d_c = ctx_only("D", "mla")
            xc1, h2c, gates_c = _out_router((a_c, b_c, c_c, d_c), xc, g1, sh2, sc2, ctx_row, pw, nctx)
            xc = _moe(h2c.reshape(b * nctx, d), gates_c.reshape(b * nctx, LANES),
                      xc1.reshape(b * nctx, d), g2, lambda i: b, pw, tm_ctx, False).reshape(b, nctx, d)
    return x
```

```python
import functools
import math

import numpy as np
import jax
import jax.numpy as jnp
from jax import lax
from jax.experimental import pallas as pl
from jax.experimental.pallas import tpu as pltpu

F32 = jnp.float32
BF16 = jnp.bfloat16

GRID_W = 64
EPS = 1e-6
ROPE_BASE = 10000.0
GQA_HEADS, GQA_KV_HEADS, GQA_DIM = 4, 2, 64
NA_HEADS, NA_DIM, NA_WIN_ROWS, NA_WIN_COLS = 4, 64, 8, 16
DIFF_HEADS, DIFF_QK_DIM, DIFF_V_DIM = 4, 32, 64
MLA_HEADS, MLA_NOPE, MLA_ROPE, MLA_V, MLA_Q_LORA, MLA_KV_LORA = 4, 64, 32, 64, 192, 128
N_EXPERTS, MOE_GROUPS, D_EXPERT = 16, 4, 256
EXPERTS_PER_GROUP = N_EXPERTS // MOE_GROUPS

LANES = 128
SUBLANES = 8
VMEM_LIMIT = 56 * 1024 * 1024

TOK_TILE = 512
NA_TILE = 256
MOE_TILE = 1024
MOD_TILE = 1024
PROJ_SUBTILES = 2
OUT_SUBTILES = 2
NEG = -1e30
LOG2E = math.log2(math.e)

SEG_W = 256
OFF_AQ, OFF_AK, OFF_AV = 0, 256, 512
OFF_BQ, OFF_BK, OFF_BV = 768, 1024, 1280
OFF_CQ, OFF_CK, OFF_CV = 1536, 1792, 2048
OFF_DQ, OFF_DKV, OFF_DPE = 2304, 2560, 2688


def _cparams(sem):
    return pltpu.CompilerParams(dimension_semantics=sem, vmem_limit_bytes=VMEM_LIMIT)


def _lane_iota(shape):
    return lax.broadcasted_iota(jnp.int32, shape, 1)


def _lane_group(shape, width):
    return lax.shift_right_logical(_lane_iota(shape), int(math.log2(width)))


def _sigmoid(x):
    return 1.0 / (1.0 + jnp.exp(-x))


def _mod_kernel(c_ref, w_ref, b_ref, o_ref):
    c = c_ref[...]
    s = c * _sigmoid(c)
    o_ref[0] = jnp.dot(s, w_ref[0], preferred_element_type=F32,
                       precision=lax.Precision.HIGHEST) + b_ref[0]


def _modulation(cc, w_mod, b_mod):
    depth, d, n = w_mod.shape
    rows = cc.shape[0]
    return pl.pallas_call(
        _mod_kernel,
        grid=(depth, n // MOD_TILE),
        in_specs=[
            pl.BlockSpec((rows, d), lambda l, j: (0, 0)),
            pl.BlockSpec((1, d, MOD_TILE), lambda l, j: (l, 0, j)),
            pl.BlockSpec((1, 1, MOD_TILE), lambda l, j: (l, 0, j)),
        ],
        out_specs=pl.BlockSpec((1, rows, MOD_TILE), lambda l, j: (l, 0, j)),
        out_shape=jax.ShapeDtypeStruct((depth, rows, n), F32),
        compiler_params=_cparams(("parallel", "parallel")),
        name="modulation",
    )(cc, w_mod, b_mod.reshape(depth, 1, n))


def _swap_halves(x, half):
    w = x.shape[1]
    lane = _lane_iota(x.shape)
    fwd = pltpu.roll(x, half, 1)
    bwd = pltpu.roll(x, w - half, 1)
    return jnp.where((lane & (2 * half - 1)) < half, bwd, fwd)


def _rope(x, cos, sin, half):
    reps = x.shape[1] // cos.shape[1]
    c = jnp.concatenate([cos] * reps, axis=1) if reps > 1 else cos
    s = jnp.concatenate([sin] * reps, axis=1) if reps > 1 else sin
    return x * c + _swap_halves(x, half) * s


def _group_rms(x, group, count):
    gid = _lane_group(x.shape, group)
    x2 = x * x
    inv = jnp.zeros_like(x)
    for g in range(x.shape[1] // group):
        msk = gid == g
        ms = jnp.sum(jnp.where(msk, x2, 0.0), axis=1, keepdims=True) * (1.0 / count)
        inv = jnp.where(msk, lax.rsqrt(ms + EPS), inv)
    return x * inv


LAT_OUTS = ("A.qT", "A.k", "A.vT", "B.qT", "B.k", "B.vT", "C.qT", "C.k", "C.vT", "D.qT", "D.k", "D.vT")
CTX_KV_OUTS = ("A.k", "A.vT", "B.k", "B.vT", "C.k", "C.vT", "D.k", "D.vT")
CTX_OUTS = CTX_KV_OUTS + ("A.qT", "B.qT", "C.qT", "D.qT")
GROUP_W = {"A": SEG_W, "B": SEG_W, "C": SEG_W, "D": 2 * SEG_W}


def _proj_kernel(*refs, rope, outs):
    (x_ref, sh_ref, sc_ref, gmix_ref, w_ref, wqu_ref, wkk_ref, wkv_ref,
     gq_ref, gk_ref, gcq_ref, gckv_ref) = refs[:12]
    pos = 12
    tabs = refs[pos:pos + 6] if rope else ()
    pos += len(tabs)
    out = dict(zip(outs, refs[pos:pos + len(outs)]))
    tm = x_ref.shape[1]
    parts = PROJ_SUBTILES if tm % (PROJ_SUBTILES * 2 * LANES) == 0 else 1
    sub = tm // parts
    for h in range(parts):
        _proj_subtile(slice(h * sub, (h + 1) * sub), x_ref, sh_ref, sc_ref, gmix_ref, w_ref, wqu_ref,
                      wkk_ref, wkv_ref, gq_ref, gk_ref, gcq_ref, gckv_ref, tabs, out, rope)


def _proj_subtile(rows, x_ref, sh_ref, sc_ref, gmix_ref, w_ref, wqu_ref, wkk_ref, wkv_ref,
                  gq_ref, gk_ref, gcq_ref, gckv_ref, tabs, out, rope):
    r0, sub = rows.start, rows.stop - rows.start
    if rope:
        cg, sg, cd, sd, cm, sm = (t[rows, :] for t in tabs)

    def want(g, t):
        return (g + "." + t) in out or (g + "." + t + "T") in out

    def put(g, t, val):
        if g + "." + t in out:
            out[g + "." + t][0, rows, :] = val.astype(BF16)
        if g + "." + t + "T" in out:
            ref = out[g + "." + t + "T"]
            vt = val.T.astype(BF16)
            cw = ref.shape[3]
            if cw >= sub:
                ref[0, r0 // cw, :, r0 % cw:r0 % cw + sub] = vt
            else:
                for c in range(sub // cw):
                    ref[0, r0 // cw + c] = vt[:, c * cw:(c + 1) * cw]

    x = x_ref[0, rows, :]
    ms = jnp.mean(x * x, axis=-1, keepdims=True)
    h = x * lax.rsqrt(ms + EPS) * gmix_ref[...]
    h = h * (1.0 + sc_ref[0]) + sh_ref[0]
    hb = h.astype(BF16)

    def seg(off, width=SEG_W):
        return jnp.dot(hb, w_ref[:, off:off + width], preferred_element_type=F32)

    cq_raw = seg(OFF_DQ) if want("D", "q") else None
    ckv_raw = seg(OFF_DKV, LANES)
    kpe = seg(OFF_DPE, LANES)

    if want("A", "q"):
        q = _group_rms(seg(OFF_AQ), GQA_DIM, GQA_DIM) * gq_ref[...]
        if rope:
            q = _rope(q, cg, sg, GQA_DIM // 4)
        put("A", "q", q)
    k = _group_rms(seg(OFF_AK), GQA_DIM, GQA_DIM) * gk_ref[...]
    if rope:
        k = _rope(k, cg, sg, GQA_DIM // 4)
    put("A", "k", k)
    put("A", "v", seg(OFF_AV))

    if want("B", "q"):
        put("B", "q", seg(OFF_BQ) * (NA_DIM ** -0.5 * LOG2E))
    put("B", "k", seg(OFF_BK))
    put("B", "v", seg(OFF_BV))

    if want("C", "q"):
        q = seg(OFF_CQ) * (DIFF_QK_DIM ** -0.5 * LOG2E)
        if rope:
            q = _rope(q, cd, sd, DIFF_QK_DIM // 4)
        put("C", "q", q)
    k = seg(OFF_CK)
    if rope:
        k = _rope(k, cd, sd, DIFF_QK_DIM // 4)
    put("C", "k", k)
    put("C", "v", seg(OFF_CV))

    if want("D", "q"):
        cq = _group_rms(cq_raw, SEG_W, MLA_Q_LORA) * gcq_ref[...]
        q = jnp.dot(cq.astype(BF16), wqu_ref[...], preferred_element_type=F32)
        q = q * ((MLA_NOPE + MLA_ROPE) ** -0.5 * LOG2E)
        if rope:
            q = _rope(q, cm, sm, MLA_ROPE // 4)
        put("D", "q", q)
    ckv = _group_rms(ckv_raw, LANES, MLA_KV_LORA) * gckv_ref[...]
    ckvb = ckv.astype(BF16)
    k = jnp.dot(ckvb, wkk_ref[...], preferred_element_type=F32)
    if rope:
        kpe = _rope(kpe, cm, sm, MLA_ROPE // 4)
    put("D", "k", k + jnp.concatenate([kpe] * MLA_HEADS, axis=1))
    put("D", "v", jnp.dot(ckvb, wkv_ref[...], preferred_element_type=F32))


def _project(x, sh, sc, row_of_batch, pw, tables, tm, outs):
    b, n, d = x.shape
    nt = n // tm
    rope = tables is not None
    full = lambda a: pl.BlockSpec(a.shape, lambda bi, i: (0,) * a.ndim)
    mod_spec = pl.BlockSpec((1, 1, d), lambda bi, i: (row_of_batch(bi), 0, 0))
    ins = [x, sh, sc, pw["gmix"], pw["w_in"], pw["wq_up"], pw["wkv_k"], pw["wkv_v"],
           pw["gq"], pw["gk"], pw["gcq"], pw["gckv"]]
    in_specs = [pl.BlockSpec((1, tm, d), lambda bi, i: (bi, i, 0)), mod_spec, mod_spec]
    in_specs += [full(a) for a in ins[3:]]
    if rope:
        ins += list(tables)
        in_specs += [pl.BlockSpec((tm, LANES), lambda bi, i: (i, 0)) for _ in tables]
    shapes, specs = [], []
    for name in outs:
        g, t = name.split(".")
        w = SEG_W if t[0] == "v" else GROUP_W[g]
        if t.endswith("T"):
            cw = min(tm, NA_TILE) if g == "B" else tm
            shapes.append(jax.ShapeDtypeStruct((b, n // cw, w, cw), BF16))
            specs.append(pl.BlockSpec((1, tm // cw, w, cw), lambda bi, i: (bi, i, 0, 0)))
        else:
            shapes.append(jax.ShapeDtypeStruct((b, n, w), BF16))
            specs.append(pl.BlockSpec((1, tm, w), lambda bi, i: (bi, i, 0)))
    res = pl.pallas_call(
        functools.partial(_proj_kernel, rope=rope, outs=tuple(outs)),
        grid=(b, nt),
        in_specs=in_specs,
        out_specs=specs,
        out_shape=shapes,
        compiler_params=_cparams(("parallel", "parallel")),
        name="proj_rope" if rope else "proj_ctx",
    )(*ins)
    return dict(zip(outs, res))


HEAD_V = 64


def _tile_lanes(x, width):
    reps = width // x.shape[1]
    return jnp.tile(x, (1, reps)) if reps > 1 else x


def with_ones(vt):
    rows = lax.broadcasted_iota(jnp.int32, (LANES - HEAD_V, vt.shape[1]), 0)
    ones = jnp.where(rows == 0, 1.0, 0.0).astype(BF16)
    return jnp.concatenate([vt, ones], axis=0)


def _map_slices(kind):
    def k_lanes(m):
        return slice(m * LANES, (m + 1) * LANES) if kind == "mla" else slice(None)

    def v_rows(m):
        hv = m // 2 if kind == "diff" else m
        return slice(hv * HEAD_V, (hv + 1) * HEAD_V)

    return k_lanes, v_rows


def _query_slabs(qt_ref, kind, nm):
    qv = jnp.concatenate([qt_ref[0, c] for c in range(qt_ref.shape[1])], axis=1)
    if kind == "mla":
        return [qv[m * LANES:(m + 1) * LANES, :] for m in range(nm)]
    rows = lax.broadcasted_iota(jnp.int32, qv.shape, 0)
    grp = lax.shift_right_logical(rows, int(math.log2(LANES // nm)))
    return [jnp.where(grp == m, qv, jnp.zeros_like(qv)) for m in range(nm)]


def _finish_heads(accs, kind, lam_init, diff_refs):
    outs = [a[0:HEAD_V] / a[HEAD_V:HEAD_V + 1] for a in accs]
    if kind == "diff":
        lq1_ref, lk1_ref, lq2_ref, lk2_ref, sub_ref = diff_refs
        lam = (jnp.exp(jnp.sum(lq1_ref[...] * lk1_ref[...], axis=1, keepdims=True))
               - jnp.exp(jnp.sum(lq2_ref[...] * lk2_ref[...], axis=1, keepdims=True))
               + lam_init)
        heads = []
        for hd in range(2):
            dlt = outs[2 * hd] - lam * outs[2 * hd + 1]
            ms2 = jnp.mean(dlt * dlt, axis=0, keepdims=True)
            heads.append(dlt * lax.rsqrt(ms2 + EPS) * sub_ref[...] * (1.0 - lam_init))
        outs = heads
    return jnp.concatenate(outs, axis=0).T


def _flash_ctx_kernel(*refs, kind, lam_init):
    qt_ref, kc_ref, vct_ref = refs[:3]
    nd = 5 if kind == "diff" else 0
    diff_refs, o_ref = refs[3:3 + nd], refs[3 + nd]
    nm = 4 if kind == "diff" else 2
    k_lanes, v_rows = _map_slices(kind)
    accs = []
    for m, qm in enumerate(_query_slabs(qt_ref, kind, nm)):
        st = jnp.dot(kc_ref[0, :, k_lanes(m)], qm, preferred_element_type=F32)
        p = jnp.exp2(st - jnp.max(st, axis=0, keepdims=True)).astype(BF16)
        accs.append(jnp.dot(with_ones(vct_ref[0, 0, v_rows(m), :]), p, preferred_element_type=F32))
    o_ref[0] = _finish_heads(accs, kind, lam_init, diff_refs).astype(o_ref.dtype)


def _flash_lat_kernel(*refs, kind, lam_init):
    qt_ref, kc_ref, vct_ref, k_ref, vt_ref = refs[:5]
    nd = 5 if kind == "diff" else 0
    diff_refs, o_ref = refs[5:5 + nd], refs[5 + nd]
    qs_e, acc_e, s_e, qs_l, acc_l, s_l, m_ref, mc_ref = refs[6 + nd:14 + nd]
    early = (qs_e, acc_e, s_e, 0, False)
    late = (qs_l, acc_l, s_l, vt_ref.shape[1] // 2, True)

    nm = qs_e.shape[0]
    nk, tk = vt_ref.shape[1], vt_ref.shape[3]
    half = nk // 2
    iters = half // 2
    g = pl.program_id(2)
    nq = pl.num_programs(2) - 1
    k_lanes, v_rows = _map_slices(kind)

    def produce(tile, slot, j, m):
        qs_ref, _, s_ref = tile[:3]
        rows = pl.ds(pl.multiple_of(j * tk, tk), tk)
        st = jnp.dot(k_ref[0, rows, k_lanes(m)], qs_ref[m], preferred_element_type=F32)
        s_ref[slot, m] = st
        return jnp.max(st, axis=0, keepdims=True)

    def consume(tile, slot, j, m, m_prev, m_chunk):
        _, acc_ref, s_ref = tile[:3]
        m_new = jnp.maximum(m_prev, m_chunk)
        alpha = jnp.exp2(m_prev - m_new)
        p = jnp.exp2(s_ref[slot, m] - m_new).astype(BF16)
        acc_ref[m] = alpha * acc_ref[m] + jnp.dot(
            with_ones(vt_ref[0, j, v_rows(m), :]), p, preferred_element_type=F32)
        return m_new

    def iteration(i, carries, tiles, last):
        maps = range(nm)
        j0s = [tile[3] + 2 * i for tile in tiles]
        c1 = [[None] * nm for _ in tiles]
        m1 = [[None] * nm for _ in tiles]
        c2 = [[None] * nm for _ in tiles]
        m2 = [[None] * nm for _ in tiles]
        for t, (tile, j0, c) in enumerate(zip(tiles, j0s, carries)):
            for m in maps:
                c1[t][m] = produce(tile, 1, j0 + 1, m)
                m1[t][m] = consume(tile, 0, j0, m, c[0][m], c[1][m])
        for t, (tile, j0) in enumerate(zip(tiles, j0s)):
            for m in maps:
                c2[t][m] = c1[t][m] if (last and tile[4]) else produce(tile, 0, j0 + 2, m)
                m2[t][m] = consume(tile, 1, j0 + 1, m, m1[t][m], c1[t][m])
        return tuple((tuple(a), tuple(b)) for a, b in zip(m2, c2))

    def run(carries, tiles):
        carries = iteration(0, carries, tiles, iters == 1)
        if iters > 2:
            carries = lax.fori_loop(1, iters - 1, lambda i, c: iteration(i, c, tiles, False), carries)
        if iters > 1:
            carries = iteration(iters - 1, carries, tiles, True)
        return carries

    def start_early():
        for m, qm in enumerate(_query_slabs(qt_ref, kind, nm)):
            qs_e[m] = qm
        sts = [jnp.dot(kc_ref[0, :, k_lanes(m)], qs_e[m], preferred_element_type=F32)
               for m in range(nm)]
        ms = tuple(jnp.max(st, axis=0, keepdims=True) for st in sts)
        first = tuple(produce(early, 0, 0, m) for m in range(nm))

        def init_from_context():
            for m in range(nm):
                p = jnp.exp2(sts[m] - ms[m]).astype(BF16)
                acc_e[m] = jnp.dot(with_ones(vct_ref[0, 0, v_rows(m), :]), p,
                                   preferred_element_type=F32)

        return (ms, first), init_from_context

    def load_late():
        return (tuple(m_ref[m] for m in range(nm)), tuple(mc_ref[m] for m in range(nm)))

    def hand_over(carry):
        qs_l[...] = qs_e[...]
        acc_l[...] = acc_e[...]
        s_l[0] = s_e[0]
        for m in range(nm):
            m_ref[m] = carry[0][m]
            mc_ref[m] = carry[1][m]

    def finish_late():
        accs = [acc_l[m] for m in range(nm)]
        o_ref[0] = _finish_heads(accs, kind, lam_init, diff_refs).astype(o_ref.dtype)

    @pl.when(g == 0)
    def _():
        carry, init_from_context = start_early()
        init_from_context()
        hand_over(run((carry,), (early,))[0])

    @pl.when((g > 0) & (g < nq))
    def _():
        carry_l = load_late()
        carry_e, init_from_context = start_early()
        init_from_context()
        carry_l, carry_e = run((carry_l, carry_e), (late, early))
        finish_late()
        hand_over(carry_e)

    @pl.when(g == nq)
    def _():
        run((load_late(),), (late,))
        finish_late()


def _flash_specs(qt, kc, vct, kind, extras):
    b, nqc, wtot, cw = qt.shape
    wq = 2 * LANES if kind == "mla" else LANES
    c = kc.shape[1]
    extra_specs = [pl.BlockSpec(e.shape, lambda bi, p, i: (0, 0)) for e in extras]
    kv_specs = [pl.BlockSpec((1, c, wq), lambda bi, p, i: (bi, 0, p)),
                pl.BlockSpec((1, 1, LANES, c), lambda bi, p, i: (bi, 0, p, 0))]
    return b, nqc, wtot // wq, wq, cw, kv_specs, extra_specs


def _flash_ctx(qt, kc, vct, kind, extras=(), lam_init=0.0):
    b, nqc, npair, wq, cw, kv_specs, extra_specs = _flash_specs(qt, kc, vct, kind, extras)
    return pl.pallas_call(
        functools.partial(_flash_ctx_kernel, kind=kind, lam_init=lam_init),
        grid=(b, npair, nqc),
        in_specs=[pl.BlockSpec((1, 1, wq, cw), lambda bi, p, i: (bi, i, p, 0))] + kv_specs + extra_specs,
        out_specs=pl.BlockSpec((1, cw, LANES), lambda bi, p, i: (bi, i, p)),
        out_shape=jax.ShapeDtypeStruct((b, nqc * cw, npair * LANES), BF16),
        compiler_params=_cparams(("parallel", "parallel", "parallel")),
        name="flash_" + kind + "_ctx",
    )(qt, kc, vct, *extras)


def _flash_lat(qt, kc, vct, k, vt, kind, extras=(), lam_init=0.0):
    b, nq, npair, wq, tq, kv_specs, extra_specs = _flash_specs(qt, kc, vct, kind, extras)
    nm = 4 if kind == "diff" else 2
    nk, tk = vt.shape[1], vt.shape[3]
    assert nk % 4 == 0, "two phases of key chunks, processed in pairs"
    return pl.pallas_call(
        functools.partial(_flash_lat_kernel, kind=kind, lam_init=lam_init),
        grid=(b, npair, nq + 1),
        in_specs=[pl.BlockSpec((1, 1, wq, tq), lambda bi, p, g: (bi, jnp.minimum(g, nq - 1), p, 0))]
        + kv_specs
        + [pl.BlockSpec((1, k.shape[1], wq), lambda bi, p, g: (bi, 0, p)),
           pl.BlockSpec((1, nk, LANES, tk), lambda bi, p, g: (bi, 0, p, 0))]
        + extra_specs,
        out_specs=pl.BlockSpec((1, tq, LANES), lambda bi, p, g: (bi, jnp.maximum(g - 1, 0), p)),
        out_shape=jax.ShapeDtypeStruct((b, nq * tq, npair * LANES), BF16),
        scratch_shapes=2 * [
            pltpu.VMEM((nm, LANES, tq), BF16),
            pltpu.VMEM((nm, LANES, tq), F32),
            pltpu.VMEM((2, nm, tk, tq), F32),
        ] + 2 * [pltpu.VMEM((nm, 1, tq), F32)],
        compiler_params=_cparams(("parallel", "parallel", "arbitrary")),
        name="flash_" + kind,
    )(qt, kc, vct, k, vt, *extras)


NA_QCHUNKS = 1
NA_CHUNKS = NA_QCHUNKS + 2
NA_STEP_TILES = 2


def _na_kernel(qt_ref, k_ref, vt_ref, kc_ref, vct_ref, *rest):
    bias_refs, o_ref = rest[:NA_STEP_TILES], rest[NA_STEP_TILES]
    nkc, tk = vt_ref.shape[1], vt_ref.shape[3]
    cw = qt_ref.shape[3]
    i = pl.program_id(2)

    units = []
    for c in range(NA_STEP_TILES):
        c0 = jnp.clip(i * NA_STEP_TILES + c - 1, 0, nkc - NA_CHUNKS)
        kjs = [k_ref[0, pl.ds(pl.multiple_of((c0 + j) * tk, tk), tk), :] for j in range(NA_CHUNKS)]
        qv = qt_ref[0, c]
        rows = lax.broadcasted_iota(jnp.int32, qv.shape, 0)
        for m in range(2):
            own = lax.shift_right_logical(rows, int(math.log2(LANES // 2))) == m
            qm = jnp.where(own, qv, jnp.zeros_like(qv))
            ss = [jnp.dot(kjs[j], qm, preferred_element_type=F32) + bias_refs[c][0, m, j]
                  for j in range(NA_CHUNKS)]
            units.append((c, m, c0, ss, jnp.dot(kc_ref[0], qm, preferred_element_type=F32)))
    outs = {}
    for c, m, c0, ss, sc in units:
        v_rows = slice(m * HEAD_V, (m + 1) * HEAD_V)
        mx = jnp.max(sc, axis=0, keepdims=True)
        for st in ss:
            mx = jnp.maximum(mx, jnp.max(st, axis=0, keepdims=True))
        acc = jnp.dot(with_ones(vct_ref[0, 0, v_rows, :]), jnp.exp2(sc - mx).astype(BF16),
                      preferred_element_type=F32)
        for j, st in enumerate(ss):
            acc = acc + jnp.dot(with_ones(vt_ref[0, c0 + j, v_rows, :]),
                                jnp.exp2(st - mx).astype(BF16), preferred_element_type=F32)
        outs[c, m] = acc[0:HEAD_V] / acc[HEAD_V:HEAD_V + 1]
    for c in range(NA_STEP_TILES):
        o_ref[0, c * cw:(c + 1) * cw, :] = jnp.concatenate(
            [outs[c, 0], outs[c, 1]], axis=0).T.astype(o_ref.dtype)


def _na_bias_tiles(rel_bias, rows, tk):
    qrows = NA_QCHUNKS * tk // GRID_W
    nblk = rows // qrows
    nk = rows * GRID_W // tk
    krows = NA_CHUNKS * tk // GRID_W
    kh = min(NA_WIN_ROWS, rows)
    kw = NA_WIN_COLS
    blocks = np.array([0, 1, nblk - 1])
    c0 = np.clip(blocks * NA_QCHUNKS - 1, 0, nk - NA_CHUNKS)
    r = blocks[:, None] * qrows + np.arange(qrows)[None, :]
    kr = c0[:, None] * (tk // GRID_W) + np.arange(krows)[None, :]
    r0 = np.clip(r - kh // 2, 0, rows - kh)
    dr = kr[:, None, :] - r[:, :, None]
    valid_r = (kr[:, None, :] >= r0[:, :, None]) & (kr[:, None, :] < r0[:, :, None] + kh)
    cols = np.arange(GRID_W)
    cstart = np.clip(cols - kw // 2, 0, GRID_W - kw)
    dc = cols[None, :] - cols[:, None]
    valid_c = (cols[None, :] >= cstart[:, None]) & (cols[None, :] < cstart[:, None] + kw)
    ndc = 2 * NA_WIN_COLS - 1
    onehot = ((dc[None] + NA_WIN_COLS - 1) == np.arange(ndc)[:, None, None]) & valid_c[None]
    onehot = np.concatenate([onehot, ~valid_c[None]], axis=0)
    dr_idx = np.clip(dr + NA_WIN_ROWS - 1, 0, 2 * NA_WIN_ROWS - 2)
    g1 = rel_bias[:, dr_idx, :] * LOG2E
    g1 = jnp.concatenate([g1, jnp.full(g1.shape[:-1] + (1,), NEG, F32)], axis=-1)
    g1 = jnp.where(jnp.asarray(valid_r)[None, :, :, :, None], g1, NEG)
    tiles = jnp.einsum("htabd,dcx->thbxac", g1, jnp.asarray(onehot, F32),
                       precision=lax.Precision.HIGHEST)
    return tiles.reshape(3, rel_bias.shape[0], NA_CHUNKS, tk, qrows * GRID_W)


def _na_attention(qt, kc, vct, k, vt, bias):
    b, ntile, wtot, tq = qt.shape
    nstep = ntile // NA_STEP_TILES
    npair = wtot // LANES
    nkc, tk = vt.shape[1], vt.shape[3]
    c = kc.shape[1]
    n = k.shape[1]

    def bias_map(which):
        def index(bi, p, i):
            t = i * NA_STEP_TILES + which
            return (jnp.where(t == 0, 0, jnp.where(t == ntile - 1, 2, 1)), p, 0, 0, 0)
        return index

    return pl.pallas_call(
        _na_kernel,
        grid=(b, npair, nstep),
        in_specs=[
            pl.BlockSpec((1, NA_STEP_TILES, LANES, tq), lambda bi, p, i: (bi, i, p, 0)),
            pl.BlockSpec((1, n, LANES), lambda bi, p, i: (bi, 0, p)),
            pl.BlockSpec((1, nkc, LANES, tk), lambda bi, p, i: (bi, 0, p, 0)),
            pl.BlockSpec((1, c, LANES), lambda bi, p, i: (bi, 0, p)),
            pl.BlockSpec((1, 1, LANES, c), lambda bi, p, i: (bi, 0, p, 0)),
        ] + [pl.BlockSpec((1, 2, NA_CHUNKS, tk, tq), bias_map(w)) for w in range(NA_STEP_TILES)],
        out_specs=pl.BlockSpec((1, NA_STEP_TILES * tq, LANES), lambda bi, p, i: (bi, i, p)),
        out_shape=jax.ShapeDtypeStruct((b, ntile * tq, wtot), BF16),
        compiler_params=_cparams(("parallel", "parallel", "parallel")),
        name="na_attention",
    )(qt, k, vt, kc, vct, *([bias] * NA_STEP_TILES))


def _first_index_of_max(vals, row, big):
    mx = jnp.max(vals, axis=0, keepdims=True)
    idx = jnp.min(jnp.where(vals == mx, row, big), axis=0, keepdims=True)
    return mx, idx


def _router_gates(scores_t, sel_t):
    irow = lax.broadcasted_iota(jnp.int32, sel_t.shape, 0)
    row = irow.astype(F32)
    grp = lax.shift_right_logical(irow, int(math.log2(EXPERTS_PER_GROUP))).astype(F32)
    big = float(N_EXPERTS)
    best = None
    for g in range(MOE_GROUPS):
        vals = jnp.where(grp == float(g), sel_t, -jnp.inf)
        m1, i1 = _first_index_of_max(vals, row, big)
        m2 = jnp.max(jnp.where(row == i1, -jnp.inf, vals), axis=0, keepdims=True)
        gs = m1 + m2
        if best is None:
            best, bi = gs, jnp.zeros(gs.shape, F32)
        else:
            better = gs > best
            bi = jnp.where(better, float(g), bi)
            best = jnp.where(better, gs, best)
    msel = jnp.where(grp == bi, sel_t, -jnp.inf)
    _, i1 = _first_index_of_max(msel, row, big)
    msel2 = jnp.where(row == i1, -jnp.inf, msel)
    _, i2 = _first_index_of_max(msel2, row, big)
    w = jnp.where((row == i1) | (row == i2), scores_t, 0.0)
    return w / jnp.sum(w, axis=0, keepdims=True)


def _out_kernel(a_ref, b_ref, c_ref, d_ref, x_ref, g1_ref, sh_ref, sc_ref, gffn_ref,
                w_ref, rwh_ref, rwl_ref, rb_ref, xo_ref, h2_ref, gate_ref):
    tm = x_ref.shape[1]
    parts = OUT_SUBTILES if tm % (OUT_SUBTILES * LANES) == 0 else 1
    sub = tm // parts
    ys = []
    for h in range(parts):
        rows = slice(h * sub, (h + 1) * sub)
        y = None
        for gi, r in enumerate((a_ref, b_ref, c_ref, d_ref)):
            part = jnp.dot(r[0, rows, :], w_ref[gi * SEG_W:(gi + 1) * SEG_W, :],
                           preferred_element_type=F32)
            y = part if y is None else y + part
        ys.append(y)
    for h in range(parts):
        rows = slice(h * sub, (h + 1) * sub)
        xn = x_ref[0, rows, :] + g1_ref[0] * ys[h]
        xo_ref[0, rows, :] = xn
        ms = jnp.mean(xn * xn, axis=-1, keepdims=True)
        h2 = xn * lax.rsqrt(ms + EPS) * gffn_ref[...]
        h2 = h2 * (1.0 + sc_ref[0]) + sh_ref[0]
        hi = h2.astype(BF16)
        h2_ref[0, rows, :] = hi
        lo = (h2 - hi.astype(F32)).astype(BF16)
        logits = (jnp.dot(hi, rwh_ref[...], preferred_element_type=F32)
                  + jnp.dot(lo, rwh_ref[...], preferred_element_type=F32)
                  + jnp.dot(hi, rwl_ref[...], preferred_element_type=F32))
        scores_t = _sigmoid(logits).T[:N_EXPERTS]
        sel_t = scores_t + _tile_lanes(rb_ref[...], sub)
        gates_t = _router_gates(scores_t, sel_t)
        pad = jnp.zeros((LANES - N_EXPERTS, sub), F32)
        gate_ref[0, rows, :] = jnp.concatenate([gates_t, pad], axis=0).T


def _out_router(parts, x, g1, sh2, sc2, row_of_batch, pw, tm):
    b, n, d = x.shape
    full = lambda a: pl.BlockSpec(a.shape, lambda bi, i: (0,) * a.ndim)
    tok = lambda w: pl.BlockSpec((1, tm, w), lambda bi, i: (bi, i, 0))
    mod_spec = pl.BlockSpec((1, 1, d), lambda bi, i: (row_of_batch(bi), 0, 0))
    consts = [pw["gffn"], pw["w_out"], pw["rw_hi"], pw["rw_lo"], pw["rb"]]
    return pl.pallas_call(
        _out_kernel,
        grid=(b, n // tm),
        in_specs=[tok(SEG_W)] * 4 + [tok(d), mod_spec, mod_spec, mod_spec] + [full(a) for a in consts],
        out_specs=[tok(d), tok(d), tok(LANES)],
        out_shape=[jax.ShapeDtypeStruct((b, n, d), F32), jax.ShapeDtypeStruct((b, n, d), BF16),
                   jax.ShapeDtypeStruct((b, n, LANES), F32)],
        compiler_params=_cparams(("parallel", "parallel")),
        name="out_router",
    )(*parts, x, g1, sh2, sc2, *consts)


MOE_EXPERTS_PER_STEP = 4


def _swiglu_act(gu, scale=None):
    de = gu.shape[1] // 2
    g, u = gu[:, :de], gu[:, de:]
    act = (g * _sigmoid(g)) * u
    return (act if scale is None else act * scale).astype(BF16)


def _moe_kernel(h_ref, gate_ref, x_ref, g2_ref, wgu_ref, wd_ref, sgu_ref, sd_ref, fin_ref,
                o_ref, acc_ref, *, final_norm):
    step = pl.program_id(1)
    k = wgu_ref.shape[0]
    hb = h_ref[...]
    gates = gate_ref[...]
    lane = _lane_iota(gates.shape)
    acts = []
    for i in range(k):
        gu = jnp.dot(hb, wgu_ref[i], preferred_element_type=F32)
        col = jnp.sum(jnp.where(lane == step * k + i, gates, 0.0), axis=1, keepdims=True)
        acts.append(_swiglu_act(gu, col))
    wd = wd_ref[...]
    contrib = jnp.dot(jnp.concatenate(acts, axis=1), wd.reshape(k * wd.shape[1], wd.shape[2]),
                      preferred_element_type=F32)

    @pl.when(step == 0)
    def _():
        shared = _swiglu_act(jnp.dot(hb, sgu_ref[...], preferred_element_type=F32))
        acc_ref[...] = contrib + jnp.dot(shared, sd_ref[...], preferred_element_type=F32)

    @pl.when(step > 0)
    def _():
        acc_ref[...] += contrib

    @pl.when(step == pl.num_programs(1) - 1)
    def _():
        y = x_ref[...] + g2_ref[0] * acc_ref[...]
        if final_norm:
            ms = jnp.mean(y * y, axis=-1, keepdims=True)
            y = y * lax.rsqrt(ms + EPS) * fin_ref[...]
        o_ref[...] = y


def _moe(h2, gates, x, g2, row_of_tile, pw, tm, final_norm):
    t, d = x.shape
    ne, _, de2 = pw["wgu"].shape
    k = MOE_EXPERTS_PER_STEP
    full = lambda a: pl.BlockSpec(a.shape, lambda i, e: (0,) * a.ndim)
    return pl.pallas_call(
        functools.partial(_moe_kernel, final_norm=final_norm),
        grid=(t // tm, ne // k),
        in_specs=[
            pl.BlockSpec((tm, d), lambda i, e: (i, 0)),
            pl.BlockSpec((tm, LANES), lambda i, e: (i, 0)),
            pl.BlockSpec((tm, d), lambda i, e: (i, 0)),
            pl.BlockSpec((1, 1, d), lambda i, e: (row_of_tile(i), 0, 0)),
            pl.BlockSpec((k, d, de2), lambda i, e: (e, 0, 0)),
            pl.BlockSpec((k, de2 // 2, d), lambda i, e: (e, 0, 0)),
            full(pw["sgu"]), full(pw["sd"]), full(pw["fin"]),
        ],
        out_specs=pl.BlockSpec((tm, d), lambda i, e: (i, 0)),
        out_shape=jax.ShapeDtypeStruct((t, d), F32),
        scratch_shapes=[pltpu.VMEM((tm, d), F32)],
        compiler_params=_cparams(("parallel", "arbitrary")),
        name="moe_final" if final_norm else "moe",
    )(h2, gates, x, g2, pw["wgu"], pw["wd"], pw["sgu"], pw["sd"], pw["fin"])


def _axial_angles(n_tok, rot_dim):
    half = rot_dim // 2
    freqs = ROPE_BASE ** (-jnp.arange(0, half, 2, dtype=F32) / half)
    t = jnp.arange(n_tok, dtype=jnp.int32)
    row = (t // GRID_W).astype(F32)
    col = (t % GRID_W).astype(F32)
    return row[:, None] * freqs, col[:, None] * freqs


def _rope_table(n_tok, rot_dim):
    ar, ac = _axial_angles(n_tok, rot_dim)
    cos = jnp.concatenate([jnp.cos(ar)] * 2 + [jnp.cos(ac)] * 2, axis=1)
    sin = jnp.concatenate([-jnp.sin(ar), jnp.sin(ar), -jnp.sin(ac), jnp.sin(ac)], axis=1)
    return cos, sin


def _rope_tables(n_tok):
    cg, sg = _rope_table(n_tok, GQA_DIM)
    cd, sd = _rope_table(n_tok, DIFF_QK_DIM)
    cm32, sm32 = _rope_table(n_tok, MLA_ROPE)
    ones = jnp.ones((n_tok, MLA_NOPE), F32)
    zeros = jnp.zeros((n_tok, MLA_NOPE), F32)
    pad = LANES - MLA_NOPE - MLA_ROPE
    cm = jnp.concatenate([ones, cm32, ones[:, :pad]], axis=1)
    sm = jnp.concatenate([zeros, sm32, zeros[:, :pad]], axis=1)
    return (jnp.tile(cg, (1, LANES // GQA_DIM)), jnp.tile(sg, (1, LANES // GQA_DIM)),
            jnp.tile(cd, (1, LANES // DIFF_QK_DIM)), jnp.tile(sd, (1, LANES // DIFF_QK_DIM)),
            cm, sm)


def _pack_layer(l, p):
    d = p["w_in"].shape[1]
    w = p["w_in"][l]
    o = np.cumsum([0, 256, 128, 128, 256, 256, 256, 256, 256, 256, MLA_Q_LORA, MLA_KV_LORA, MLA_ROPE])
    col = lambda i: w[:, int(o[i]):int(o[i + 1])]
    dup = lambda a: jnp.concatenate([a[:, :GQA_DIM], a[:, :GQA_DIM], a[:, GQA_DIM:], a[:, GQA_DIM:]], axis=1)
    z = lambda n: jnp.zeros((d, n), F32)
    w_in = jnp.concatenate(
        [col(0), dup(col(1)), dup(col(2)), col(3), col(4), col(5), col(6), col(7), col(8),
         col(9), z(SEG_W - MLA_Q_LORA), col(10),
         z(MLA_NOPE), col(11), z(LANES - MLA_NOPE - MLA_ROPE)], axis=1).astype(BF16)
    dq = MLA_NOPE + MLA_ROPE
    wqu = p["mla_w_q_up"][l].reshape(MLA_Q_LORA, MLA_HEADS, dq)
    wqu = jnp.pad(wqu, ((0, SEG_W - MLA_Q_LORA), (0, 0), (0, LANES - dq)))
    wkv = p["mla_w_kv_up"][l].reshape(MLA_KV_LORA, MLA_HEADS, MLA_NOPE + MLA_V)
    wkk = jnp.pad(wkv[:, :, :MLA_NOPE], ((0, 0), (0, 0), (0, LANES - MLA_NOPE)))
    e = p["moe_w_gate"].shape[1]
    rw = jnp.pad(p["router_w"], ((0, 0), (0, LANES - e)))
    rw_hi = rw.astype(BF16)
    rw_lo = (rw - rw_hi.astype(F32)).astype(BF16)
    return {
        "gmix": p["norm_mix"][l][None], "gffn": p["norm_ffn"][l][None],
        "w_in": w_in,
        "wq_up": wqu.reshape(SEG_W, MLA_HEADS * LANES).astype(BF16),
        "wkv_k": wkk.reshape(MLA_KV_LORA, MLA_HEADS * LANES).astype(BF16),
        "wkv_v": wkv[:, :, MLA_NOPE:].reshape(MLA_KV_LORA, MLA_HEADS * MLA_V).astype(BF16),
        "gq": (jnp.tile(p["gqa_q_norm"][l], GQA_HEADS) * (GQA_DIM ** -0.5 * LOG2E))[None],
        "gk": jnp.tile(p["gqa_k_norm"][l], GQA_HEADS)[None],
        "gcq": jnp.pad(p["mla_q_norm"][l], (0, SEG_W - MLA_Q_LORA))[None],
        "gckv": p["mla_kv_norm"][l][None],
        "w_out": p["w_out"][l].astype(BF16),
        "rw_hi": rw_hi, "rw_lo": rw_lo,
        "rb": jnp.broadcast_to(p["router_b"][:, None], (e, LANES)),
        "wgu": jnp.concatenate([p["moe_w_gate"][l], p["moe_w_up"][l]], axis=2).astype(BF16),
        "wd": p["moe_w_down"][l].astype(BF16),
        "sgu": jnp.concatenate([p["shared_w_gate"][l], p["shared_w_up"][l]], axis=1).astype(BF16),
        "sd": p["shared_w_down"][l].astype(BF16),
        "fin": p["final_norm"][None],
        "diff": (p["diff_lq1"][l][None], p["diff_lk1"][l][None], p["diff_lq2"][l][None],
                 p["diff_lk2"][l][None], p["diff_subln"][l][:, None]),
    }


def kernel(x, c, ctx, c_ctx, w_mod, b_mod, norm_mix, norm_ffn, w_in, w_out, gqa_q_norm, gqa_k_norm,
           na_rel_bias, diff_lq1, diff_lk1, diff_lq2, diff_lk2, diff_subln, mla_q_norm, mla_w_q_up,
           mla_kv_norm, mla_w_kv_up, router_w, router_b, moe_w_gate, moe_w_up, moe_w_down,
           shared_w_gate, shared_w_up, shared_w_down, final_norm):
    p = dict(w_in=w_in, norm_mix=norm_mix, norm_ffn=norm_ffn, w_out=w_out, gqa_q_norm=gqa_q_norm,
             gqa_k_norm=gqa_k_norm, diff_lq1=diff_lq1, diff_lk1=diff_lk1, diff_lq2=diff_lq2,
             diff_lk2=diff_lk2, diff_subln=diff_subln, mla_q_norm=mla_q_norm, mla_w_q_up=mla_w_q_up,
             mla_kv_norm=mla_kv_norm, mla_w_kv_up=mla_w_kv_up, router_w=router_w, router_b=router_b,
             moe_w_gate=moe_w_gate, moe_w_up=moe_w_up, moe_w_down=moe_w_down,
             shared_w_gate=shared_w_gate, shared_w_up=shared_w_up, shared_w_down=shared_w_down,
             final_norm=final_norm)
    b, s, d = x.shape
    nctx = ctx.shape[1]
    depth = w_mod.shape[0]
    rows = s // GRID_W
    assert s % TOK_TILE == 0 and s // NA_TILE >= NA_CHUNKS + NA_STEP_TILES
    assert nctx % LANES == 0 and (b * s) % MOE_TILE == 0 and s % MOE_TILE == 0

    mrows = -(-(b + 1) // SUBLANES) * SUBLANES
    cc = jnp.zeros((mrows, d), F32).at[:b].set(c).at[b].set(c_ctx)
    mod = _modulation(cc, w_mod, b_mod)
    tables = _rope_tables(s)
    lat_row = lambda bi: bi
    ctx_row = lambda bi: b
    tm_ctx = b * nctx if b * nctx <= MOE_TILE else nctx

    xc = ctx
    for l in range(depth):
        want_ctx = l < depth - 1
        lam_init = 0.8 - 0.6 * math.exp(-0.3 * l)
        pw = _pack_layer(l, p)
        m6 = mod[l].reshape(mrows, 6, 1, d)
        sh1, sc1, g1, sh2, sc2, g2 = (m6[:, k] for k in range(6))

        lat = _project(x, sh1, sc1, lat_row, pw, tables, TOK_TILE, LAT_OUTS)
        cx = _project(xc, sh1, sc1, ctx_row, pw, None, nctx, CTX_OUTS if want_ctx else CTX_KV_OUTS)

        def dense(g, kind, extras=()):
            return _flash_lat(lat[g + ".qT"], cx[g + ".k"], cx[g + ".vT"], lat[g + ".k"], lat[g + ".vT"],
                              kind, extras, lam_init)

        def ctx_only(g, kind, extras=()):
            return _flash_ctx(cx[g + ".qT"], cx[g + ".k"], cx[g + ".vT"], kind, extras, lam_init)

        a_lat = dense("A", "pair")
        bias = _na_bias_tiles(na_rel_bias[l], rows, NA_TILE)
        b_lat = _na_attention(lat["B.qT"], cx["B.k"], cx["B.vT"], lat["B.k"], lat["B.vT"], bias)
        c_lat = dense("C", "diff", pw["diff"])
        d_lat = dense("D", "mla")
        x1, h2, gates = _out_router((a_lat, b_lat, c_lat, d_lat), x, g1, sh2, sc2, lat_row, pw, TOK_TILE)
        tiles_per_batch = s // MOE_TILE
        x = _moe(h2.reshape(b * s, d), gates.reshape(b * s, LANES), x1.reshape(b * s, d), g2,
                 lambda i: i // tiles_per_batch, pw, MOE_TILE, not want_ctx).reshape(b, s, d)

        if want_ctx:
            a_c = ctx_only("A", "pair")
            b_c = ctx_only("B", "pair")
            c_c = ctx_only("C", "diff", pw["diff"])
            d_c = ctx_only("D", "mla")
            xc1, h2c, gates_c = _out_router((a_c, b_c, c_c, d_c), xc, g1, sh2, sc2, ctx_row, pw, nctx)
            xc = _moe(h2c.reshape(b * nctx, d), gates_c.reshape(b * nctx, LANES),
                      xc1.reshape(b * nctx, d), g2, lambda i: b, pw, tm_ctx, False).reshape(b, nctx, d)
    return x
```

```python
import functools
import math

import numpy as np
import jax
import jax.numpy as jnp
from jax import lax
from jax.experimental import pallas as pl
from jax.experimental.pallas import tpu as pltpu

F32 = jnp.float32
BF16 = jnp.bfloat16

GRID_W = 64
EPS = 1e-6
ROPE_BASE = 10000.0
GQA_HEADS, GQA_KV_HEADS, GQA_DIM = 4, 2, 64
NA_HEADS, NA_DIM, NA_WIN_ROWS, NA_WIN_COLS = 4, 64, 8, 16
DIFF_HEADS, DIFF_QK_DIM, DIFF_V_DIM = 4, 32, 64
MLA_HEADS, MLA_NOPE, MLA_ROPE, MLA_V, MLA_Q_LORA, MLA_KV_LORA = 4, 64, 32, 64, 192, 128
N_EXPERTS, MOE_GROUPS, D_EXPERT = 16, 4, 256
EXPERTS_PER_GROUP = N_EXPERTS // MOE_GROUPS

LANES = 128
SUBLANES = 8
VMEM_LIMIT = 56 * 1024 * 1024

TOK_TILE = 512
NA_TILE = 256
MOE_TILE = 1024
MOD_TILE = 1024
PROJ_SUBTILES = 2
OUT_SUBTILES = 2
NEG = -1e30
LOG2E = math.log2(math.e)

SEG_W = 256
OFF_AQ, OFF_AK, OFF_AV = 0, 256, 512
OFF_BQ, OFF_BK, OFF_BV = 768, 1024, 1280
OFF_CQ, OFF_CK, OFF_CV = 1536, 1792, 2048
OFF_DQ, OFF_DKV, OFF_DPE = 2304, 2560, 2688


def _cparams(sem):
    return pltpu.CompilerParams(dimension_semantics=sem, vmem_limit_bytes=VMEM_LIMIT)


def _lane_iota(shape):
    return lax.broadcasted_iota(jnp.int32, shape, 1)


def _lane_group(shape, width):
    return lax.shift_right_logical(_lane_iota(shape), int(math.log2(width)))


def _sigmoid(x):
    return 1.0 / (1.0 + jnp.exp(-x))


def _mod_kernel(c_ref, w_ref, b_ref, o_ref):
    c = c_ref[...]
    s = c * _sigmoid(c)
    o_ref[0] = jnp.dot(s, w_ref[0], preferred_element_type=F32,
                       precision=lax.Precision.HIGHEST) + b_ref[0]


def _modulation(cc, w_mod, b_mod):
    depth, d, n = w_mod.shape
    rows = cc.shape[0]
    return pl.pallas_call(
        _mod_kernel,
        grid=(depth, n // MOD_TILE),
        in_specs=[
            pl.BlockSpec((rows, d), lambda l, j: (0, 0)),
            pl.BlockSpec((1, d, MOD_TILE), lambda l, j: (l, 0, j)),
            pl.BlockSpec((1, 1, MOD_TILE), lambda l, j: (l, 0, j)),
        ],
        out_specs=pl.BlockSpec((1, rows, MOD_TILE), lambda l, j: (l, 0, j)),
        out_shape=jax.ShapeDtypeStruct((depth, rows, n), F32),
        compiler_params=_cparams(("parallel", "parallel")),
        name="modulation",
    )(cc, w_mod, b_mod.reshape(depth, 1, n))


def _swap_halves(x, half):
    w = x.shape[1]
    lane = _lane_iota(x.shape)
    fwd = pltpu.roll(x, half, 1)
    bwd = pltpu.roll(x, w - half, 1)
    return jnp.where((lane & (2 * half - 1)) < half, bwd, fwd)


def _rope(x, cos, sin, half):
    reps = x.shape[1] // cos.shape[1]
    c = jnp.concatenate([cos] * reps, axis=1) if reps > 1 else cos
    s = jnp.concatenate([sin] * reps, axis=1) if reps > 1 else sin
    return x * c + _swap_halves(x, half) * s


def _group_rms(x, group, count):
    gid = _lane_group(x.shape, group)
    x2 = x * x
    inv = jnp.zeros_like(x)
    for g in range(x.shape[1] // group):
        msk = gid == g
        ms = jnp.sum(jnp.where(msk, x2, 0.0), axis=1, keepdims=True) * (1.0 / count)
        inv = jnp.where(msk, lax.rsqrt(ms + EPS), inv)
    return x * inv


LAT_OUTS = ("A.qT", "A.k", "A.vT", "B.qT", "B.k", "B.vT", "C.qT", "C.k", "C.vT", "D.qT", "D.k", "D.vT")
CTX_KV_OUTS = ("A.k", "A.vT", "B.k", "B.vT", "C.k", "C.vT", "D.k", "D.vT")
CTX_OUTS = CTX_KV_OUTS + ("A.qT", "B.qT", "C.qT", "D.qT")
GROUP_W = {"A": SEG_W, "B": SEG_W, "C": SEG_W, "D": 2 * SEG_W}


def _proj_kernel(*refs, rope, outs):
    (x_ref, sh_ref, sc_ref, gmix_ref, w_ref, wqu_ref, wkk_ref, wkv_ref,
     gq_ref, gk_ref, gcq_ref, gckv_ref) = refs[:12]
    pos = 12
    tabs = refs[pos:pos + 6] if rope else ()
    pos += len(tabs)
    out = dict(zip(outs, refs[pos:pos + len(outs)]))
    tm = x_ref.shape[1]
    parts = PROJ_SUBTILES if tm % (PROJ_SUBTILES * 2 * LANES) == 0 else 1
    sub = tm // parts
    for h in range(parts):
        _proj_subtile(slice(h * sub, (h + 1) * sub), x_ref, sh_ref, sc_ref, gmix_ref, w_ref, wqu_ref,
                      wkk_ref, wkv_ref, gq_ref, gk_ref, gcq_ref, gckv_ref, tabs, out, rope)


def _proj_subtile(rows, x_ref, sh_ref, sc_ref, gmix_ref, w_ref, wqu_ref, wkk_ref, wkv_ref,
                  gq_ref, gk_ref, gcq_ref, gckv_ref, tabs, out, rope):
    r0, sub = rows.start, rows.stop - rows.start
    if rope:
        cg, sg, cd, sd, cm, sm = (t[rows, :] for t in tabs)

    def want(g, t):
        return (g + "." + t) in out or (g + "." + t + "T") in out

    def put(g, t, val):
        if g + "." + t in out:
            out[g + "." + t][0, rows, :] = val.astype(BF16)
        if g + "." + t + "T" in out:
            ref = out[g + "." + t + "T"]
            vt = val.T.astype(BF16)
            cw = ref.shape[3]
            if cw >= sub:
                ref[0, r0 // cw, :, r0 % cw:r0 % cw + sub] = vt
            else:
                for c in range(sub // cw):
                    ref[0, r0 // cw + c] = vt[:, c * cw:(c + 1) * cw]

    x = x_ref[0, rows, :]
    ms = jnp.mean(x * x, axis=-1, keepdims=True)
    h = x * lax.rsqrt(ms + EPS) * gmix_ref[...]
    h = h * (1.0 + sc_ref[0]) + sh_ref[0]
    hb = h.astype(BF16)

    def seg(off, width=SEG_W):
        return jnp.dot(hb, w_ref[:, off:off + width], preferred_element_type=F32)

    cq_raw = seg(OFF_DQ) if want("D", "q") else None
    ckv_raw = seg(OFF_DKV, LANES)
    kpe = seg(OFF_DPE, LANES)

    if want("A", "q"):
        q = _group_rms(seg(OFF_AQ), GQA_DIM, GQA_DIM) * gq_ref[...]
        if rope:
            q = _rope(q, cg, sg, GQA_DIM // 4)
        put("A", "q", q)
    k = _group_rms(seg(OFF_AK), GQA_DIM, GQA_DIM) * gk_ref[...]
    if rope:
        k = _rope(k, cg, sg, GQA_DIM // 4)
    put("A", "k", k)
    put("A", "v", seg(OFF_AV))

    if want("B", "q"):
        put("B", "q", seg(OFF_BQ) * (NA_DIM ** -0.5 * LOG2E))
    put("B", "k", seg(OFF_BK))
    put("B", "v", seg(OFF_BV))

    if want("C", "q"):
        q = seg(OFF_CQ) * (DIFF_QK_DIM ** -0.5 * LOG2E)
        if rope:
            q = _rope(q, cd, sd, DIFF_QK_DIM // 4)
        put("C", "q", q)
    k = seg(OFF_CK)
    if rope:
        k = _rope(k, cd, sd, DIFF_QK_DIM // 4)
    put("C", "k", k)
    put("C", "v", seg(OFF_CV))

    if want("D", "q"):
        cq = _group_rms(cq_raw, SEG_W, MLA_Q_LORA) * gcq_ref[...]
        q = jnp.dot(cq.astype(BF16), wqu_ref[...], preferred_element_type=F32)
        q = q * ((MLA_NOPE + MLA_ROPE) ** -0.5 * LOG2E)
        if rope:
            q = _rope(q, cm, sm, MLA_ROPE // 4)
        put("D", "q", q)
    ckv = _group_rms(ckv_raw, LANES, MLA_KV_LORA) * gckv_ref[...]
    ckvb = ckv.astype(BF16)
    k = jnp.dot(ckvb, wkk_ref[...], preferred_element_type=F32)
    if rope:
        kpe = _rope(kpe, cm, sm, MLA_ROPE // 4)
    put("D", "k", k + jnp.concatenate([kpe] * MLA_HEADS, axis=1))
    put("D", "v", jnp.dot(ckvb, wkv_ref[...], preferred_element_type=F32))


def _project(x, sh, sc, row_of_batch, pw, tables, tm, outs):
    b, n, d = x.shape
    nt = n // tm
    rope = tables is not None
    full = lambda a: pl.BlockSpec(a.shape, lambda bi, i: (0,) * a.ndim)
    mod_spec = pl.BlockSpec((1, 1, d), lambda bi, i: (row_of_batch(bi), 0, 0))
    ins = [x, sh, sc, pw["gmix"], pw["w_in"], pw["wq_up"], pw["wkv_k"], pw["wkv_v"],
           pw["gq"], pw["gk"], pw["gcq"], pw["gckv"]]
    in_specs = [pl.BlockSpec((1, tm, d), lambda bi, i: (bi, i, 0)), mod_spec, mod_spec]
    in_specs += [full(a) for a in ins[3:]]
    if rope:
        ins += list(tables)
        in_specs += [pl.BlockSpec((tm, LANES), lambda bi, i: (i, 0)) for _ in tables]
    shapes, specs = [], []
    for name in outs:
        g, t = name.split(".")
        w = SEG_W if t[0] == "v" else GROUP_W[g]
        if t.endswith("T"):
            cw = min(tm, NA_TILE) if g == "B" else tm
            shapes.append(jax.ShapeDtypeStruct((b, n // cw, w, cw), BF16))
            specs.append(pl.BlockSpec((1, tm // cw, w, cw), lambda bi, i: (bi, i, 0, 0)))
        else:
            shapes.append(jax.ShapeDtypeStruct((b, n, w), BF16))
            specs.append(pl.BlockSpec((1, tm, w), lambda bi, i: (bi, i, 0)))
    res = pl.pallas_call(
        functools.partial(_proj_kernel, rope=rope, outs=tuple(outs)),
        grid=(b, nt),
        in_specs=in_specs,
        out_specs=specs,
        out_shape=shapes,
        compiler_params=_cparams(("parallel", "parallel")),
        name="proj_rope" if rope else "proj_ctx",
    )(*ins)
    return dict(zip(outs, res))


HEAD_V = 64


def _tile_lanes(x, width):
    reps = width // x.shape[1]
    return jnp.tile(x, (1, reps)) if reps > 1 else x


def with_ones(vt):
    rows = lax.broadcasted_iota(jnp.int32, (LANES - HEAD_V, vt.shape[1]), 0)
    ones = jnp.where(rows == 0, 1.0, 0.0).astype(BF16)
    return jnp.concatenate([vt, ones], axis=0)


def _map_slices(kind):
    def k_lanes(m):
        return slice(m * LANES, (m + 1) * LANES) if kind == "mla" else slice(None)

    def v_rows(m):
        hv = m // 2 if kind == "diff" else m
        return slice(hv * HEAD_V, (hv + 1) * HEAD_V)

    return k_lanes, v_rows


def _query_slabs(qt_ref, kind, nm):
    qv = jnp.concatenate([qt_ref[0, c] for c in range(qt_ref.shape[1])], axis=1)
    if kind == "mla":
        return [qv[m * LANES:(m + 1) * LANES, :] for m in range(nm)]
    rows = lax.broadcasted_iota(jnp.int32, qv.shape, 0)
    grp = lax.shift_right_logical(rows, int(math.log2(LANES // nm)))
    return [jnp.where(grp == m, qv, jnp.zeros_like(qv)) for m in range(nm)]


def _finish_heads(accs, kind, lam_init, diff_refs):
    outs = [a[0:HEAD_V] / a[HEAD_V:HEAD_V + 1] for a in accs]
    if kind == "diff":
        lq1_ref, lk1_ref, lq2_ref, lk2_ref, sub_ref = diff_refs
        lam = (jnp.exp(jnp.sum(lq1_ref[...] * lk1_ref[...], axis=1, keepdims=True))
               - jnp.exp(jnp.sum(lq2_ref[...] * lk2_ref[...], axis=1, keepdims=True))
               + lam_init)
        heads = []
        for hd in range(2):
            dlt = outs[2 * hd] - lam * outs[2 * hd + 1]
            ms2 = jnp.mean(dlt * dlt, axis=0, keepdims=True)
            heads.append(dlt * lax.rsqrt(ms2 + EPS) * sub_ref[...] * (1.0 - lam_init))
        outs = heads
    return jnp.concatenate(outs, axis=0).T


def _flash_ctx_kernel(*refs, kind, lam_init):
    qt_ref, kc_ref, vct_ref = refs[:3]
    nd = 5 if kind == "diff" else 0
    diff_refs, o_ref = refs[3:3 + nd], refs[3 + nd]
    nm = 4 if kind == "diff" else 2
    k_lanes, v_rows = _map_slices(kind)
    accs = []
    for m, qm in enumerate(_query_slabs(qt_ref, kind, nm)):
        st = jnp.dot(kc_ref[0, :, k_lanes(m)], qm, preferred_element_type=F32)
        p = jnp.exp2(st - jnp.max(st, axis=0, keepdims=True)).astype(BF16)
        accs.append(jnp.dot(with_ones(vct_ref[0, 0, v_rows(m), :]), p, preferred_element_type=F32))
    o_ref[0] = _finish_heads(accs, kind, lam_init, diff_refs).astype(o_ref.dtype)


def _flash_lat_kernel(*refs, kind, lam_init):
    qt_ref, kc_ref, vct_ref, k_ref, vt_ref = refs[:5]
    nd = 5 if kind == "diff" else 0
    diff_refs, o_ref = refs[5:5 + nd], refs[5 + nd]
    qs_e, acc_e, s_e, qs_l, acc_l, s_l, m_ref, mc_ref = refs[6 + nd:14 + nd]
    early = (qs_e, acc_e, s_e, 0, False)
    late = (qs_l, acc_l, s_l, vt_ref.shape[1] // 2, True)

    nm = qs_e.shape[0]
    nk, tk = vt_ref.shape[1], vt_ref.shape[3]
    half = nk // 2
    iters = half // 2
    g = pl.program_id(2)
    nq = pl.num_programs(2) - 1
    k_lanes, v_rows = _map_slices(kind)

    def produce(tile, slot, j, m):
        qs_ref, _, s_ref = tile[:3]
        rows = pl.ds(pl.multiple_of(j * tk, tk), tk)
        st = jnp.dot(k_ref[0, rows, k_lanes(m)], qs_ref[m], preferred_element_type=F32)
        s_ref[slot, m] = st
        return jnp.max(st, axis=0, keepdims=True)

    def consume(tile, slot, j, m, m_prev, m_chunk):
        _, acc_ref, s_ref = tile[:3]
        m_new = jnp.maximum(m_prev, m_chunk)
        alpha = jnp.exp2(m_prev - m_new)
        p = jnp.exp2(s_ref[slot, m] - m_new).astype(BF16)
        acc_ref[m] = alpha * acc_ref[m] + jnp.dot(
            with_ones(vt_ref[0, j, v_rows(m), :]), p, preferred_element_type=F32)
        return m_new

    def iteration(i, carries, tiles, last):
        maps = range(nm)
        j0s = [tile[3] + 2 * i for tile in tiles]
        c1 = [[None] * nm for _ in tiles]
        m1 = [[None] * nm for _ in tiles]
        c2 = [[None] * nm for _ in tiles]
        m2 = [[None] * nm for _ in tiles]
        for t, (tile, j0, c) in enumerate(zip(tiles, j0s, carries)):
            for m in maps:
                c1[t][m] = produce(tile, 1, j0 + 1, m)
                m1[t][m] = consume(tile, 0, j0, m, c[0][m], c[1][m])
        for t, (tile, j0) in enumerate(zip(tiles, j0s)):
            for m in maps:
                c2[t][m] = c1[t][m] if (last and tile[4]) else produce(tile, 0, j0 + 2, m)
                m2[t][m] = consume(tile, 1, j0 + 1, m, m1[t][m], c1[t][m])
        return tuple((tuple(a), tuple(b)) for a, b in zip(m2, c2))

    def run(carries, tiles):
        carries = iteration(0, carries, tiles, iters == 1)
        if iters > 2:
            carries = lax.fori_loop(1, iters - 1, lambda i, c: iteration(i, c, tiles, False), carries)
        if iters > 1:
            carries = iteration(iters - 1, carries, tiles, True)
        return carries

    def start_early():
        for m, qm in enumerate(_query_slabs(qt_ref, kind, nm)):
            qs_e[m] = qm
        sts = [jnp.dot(kc_ref[0, :, k_lanes(m)], qs_e[m], preferred_element_type=F32)
               for m in range(nm)]
        ms = tuple(jnp.max(st, axis=0, keepdims=True) for st in sts)
        first = tuple(produce(early, 0, 0, m) for m in range(nm))

        def init_from_context():
            for m in range(nm):
                p = jnp.exp2(sts[m] - ms[m]).astype(BF16)
                acc_e[m] = jnp.dot(with_ones(vct_ref[0, 0, v_rows(m), :]), p,
                                   preferred_element_type=F32)

        return (ms, first), init_from_context

    def load_late():
        return (tuple(m_ref[m] for m in range(nm)), tuple(mc_ref[m] for m in range(nm)))

    def hand_over(carry):
        qs_l[...] = qs_e[...]
        acc_l[...] = acc_e[...]
        s_l[0] = s_e[0]
        for m in range(nm):
            m_ref[m] = carry[0][m]
            mc_ref[m] = carry[1][m]

    def finish_late():
        accs = [acc_l[m] for m in range(nm)]
        o_ref[0] = _finish_heads(accs, kind, lam_init, diff_refs).astype(o_ref.dtype)

    @pl.when(g == 0)
    def _():
        carry, init_from_context = start_early()
        init_from_context()
        hand_over(run((carry,), (early,))[0])

    @pl.when((g > 0) & (g < nq))
    def _():
        carry_l = load_late()
        carry_e, init_from_context = start_early()
        init_from_context()
        carry_l, carry_e = run((carry_l, carry_e), (late, early))
        finish_late()
        hand_over(carry_e)

    @pl.when(g == nq)
    def _():
        run((load_late(),), (late,))
        finish_late()


def _flash_specs(qt, kc, vct, kind, extras):
    b, nqc, wtot, cw = qt.shape
    wq = 2 * LANES if kind == "mla" else LANES
    c = kc.shape[1]
    extra_specs = [pl.BlockSpec(e.shape, lambda bi, p, i: (0, 0)) for e in extras]
    kv_specs = [pl.BlockSpec((1, c, wq), lambda bi, p, i: (bi, 0, p)),
                pl.BlockSpec((1, 1, LANES, c), lambda bi, p, i: (bi, 0, p, 0))]
    return b, nqc, wtot // wq, wq, cw, kv_specs, extra_specs


def _flash_ctx(qt, kc, vct, kind, extras=(), lam_init=0.0):
    b, nqc, npair, wq, cw, kv_specs, extra_specs = _flash_specs(qt, kc, vct, kind, extras)
    return pl.pallas_call(
        functools.partial(_flash_ctx_kernel, kind=kind, lam_init=lam_init),
        grid=(b, npair, nqc),
        in_specs=[pl.BlockSpec((1, 1, wq, cw), lambda bi, p, i: (bi, i, p, 0))] + kv_specs + extra_specs,
        out_specs=pl.BlockSpec((1, cw, LANES), lambda bi, p, i: (bi, i, p)),
        out_shape=jax.ShapeDtypeStruct((b, nqc * cw, npair * LANES), BF16),
        compiler_params=_cparams(("parallel", "parallel", "parallel")),
        name="flash_" + kind + "_ctx",
    )(qt, kc, vct, *extras)


def _flash_lat(qt, kc, vct, k, vt, kind, extras=(), lam_init=0.0):
    b, nq, npair, wq, tq, kv_specs, extra_specs = _flash_specs(qt, kc, vct, kind, extras)
    nm = 4 if kind == "diff" else 2
    nk, tk = vt.shape[1], vt.shape[3]
    assert nk % 4 == 0, "two phases of key chunks, processed in pairs"
    return pl.pallas_call(
        functools.partial(_flash_lat_kernel, kind=kind, lam_init=lam_init),
        grid=(b, npair, nq + 1),
        in_specs=[pl.BlockSpec((1, 1, wq, tq), lambda bi, p, g: (bi, jnp.minimum(g, nq - 1), p, 0))]
        + kv_specs
        + [pl.BlockSpec((1, k.shape[1], wq), lambda bi, p, g: (bi, 0, p)),
           pl.BlockSpec((1, nk, LANES, tk), lambda bi, p, g: (bi, 0, p, 0))]
        + extra_specs,
        out_specs=pl.BlockSpec((1, tq, LANES), lambda bi, p, g: (bi, jnp.maximum(g - 1, 0), p)),
        out_shape=jax.ShapeDtypeStruct((b, nq * tq, npair * LANES), BF16),
        scratch_shapes=2 * [
            pltpu.VMEM((nm, LANES, tq), BF16),
            pltpu.VMEM((nm, LANES, tq), F32),
            pltpu.VMEM((2, nm, tk, tq), F32),
        ] + 2 * [pltpu.VMEM((nm, 1, tq), F32)],
        compiler_params=_cparams(("parallel", "parallel", "arbitrary")),
        name="flash_" + kind,
    )(qt, kc, vct, k, vt, *extras)


NA_QCHUNKS = 1
NA_CHUNKS = NA_QCHUNKS + 2
NA_STEP_TILES = 4


def _na_kernel(qt_ref, k_ref, vt_ref, kc_ref, vct_ref, *rest):
    bias_refs, o_ref = rest[:NA_STEP_TILES], rest[NA_STEP_TILES]
    nkc, tk = vt_ref.shape[1], vt_ref.shape[3]
    cw = qt_ref.shape[3]
    i = pl.program_id(2)

    units = []
    for c in range(NA_STEP_TILES):
        c0 = jnp.clip(i * NA_STEP_TILES + c - 1, 0, nkc - NA_CHUNKS)
        kjs = [k_ref[0, pl.ds(pl.multiple_of((c0 + j) * tk, tk), tk), :] for j in range(NA_CHUNKS)]
        qv = qt_ref[0, c]
        rows = lax.broadcasted_iota(jnp.int32, qv.shape, 0)
        for m in range(2):
            own = lax.shift_right_logical(rows, int(math.log2(LANES // 2))) == m
            qm = jnp.where(own, qv, jnp.zeros_like(qv))
            ss = [jnp.dot(kjs[j], qm, preferred_element_type=F32) + bias_refs[c][0, m, j]
                  for j in range(NA_CHUNKS)]
            units.append((c, m, c0, ss, jnp.dot(kc_ref[0], qm, preferred_element_type=F32)))
    outs = {}
    for c, m, c0, ss, sc in units:
        v_rows = slice(m * HEAD_V, (m + 1) * HEAD_V)
        mx = jnp.max(sc, axis=0, keepdims=True)
        for st in ss:
            mx = jnp.maximum(mx, jnp.max(st, axis=0, keepdims=True))
        acc = jnp.dot(with_ones(vct_ref[0, 0, v_rows, :]), jnp.exp2(sc - mx).astype(BF16),
                      preferred_element_type=F32)
        for j, st in enumerate(ss):
            acc = acc + jnp.dot(with_ones(vt_ref[0, c0 + j, v_rows, :]),
                                jnp.exp2(st - mx).astype(BF16), preferred_element_type=F32)
        outs[c, m] = acc[0:HEAD_V] / acc[HEAD_V:HEAD_V + 1]
    for c in range(NA_STEP_TILES):
        o_ref[0, c * cw:(c + 1) * cw, :] = jnp.concatenate(
            [outs[c, 0], outs[c, 1]], axis=0).T.astype(o_ref.dtype)


def _na_bias_tiles(rel_bias, rows, tk):
    qrows = NA_QCHUNKS * tk // GRID_W
    nblk = rows // qrows
    nk = rows * GRID_W // tk
    krows = NA_CHUNKS * tk // GRID_W
    kh = min(NA_WIN_ROWS, rows)
    kw = NA_WIN_COLS
    blocks = np.array([0, 1, nblk - 1])
    c0 = np.clip(blocks * NA_QCHUNKS - 1, 0, nk - NA_CHUNKS)
    r = blocks[:, None] * qrows + np.arange(qrows)[None, :]
    kr = c0[:, None] * (tk // GRID_W) + np.arange(krows)[None, :]
    r0 = np.clip(r - kh // 2, 0, rows - kh)
    dr = kr[:, None, :] - r[:, :, None]
    valid_r = (kr[:, None, :] >= r0[:, :, None]) & (kr[:, None, :] < r0[:, :, None] + kh)
    cols = np.arange(GRID_W)
    cstart = np.clip(cols - kw // 2, 0, GRID_W - kw)
    dc = cols[None, :] - cols[:, None]
    valid_c = (cols[None, :] >= cstart[:, None]) & (cols[None, :] < cstart[:, None] + kw)
    ndc = 2 * NA_WIN_COLS - 1
    onehot = ((dc[None] + NA_WIN_COLS - 1) == np.arange(ndc)[:, None, None]) & valid_c[None]
    onehot = np.concatenate([onehot, ~valid_c[None]], axis=0)
    dr_idx = np.clip(dr + NA_WIN_ROWS - 1, 0, 2 * NA_WIN_ROWS - 2)
    g1 = rel_bias[:, dr_idx, :] * LOG2E
    g1 = jnp.concatenate([g1, jnp.full(g1.shape[:-1] + (1,), NEG, F32)], axis=-1)
    g1 = jnp.where(jnp.asarray(valid_r)[None, :, :, :, None], g1, NEG)
    tiles = jnp.einsum("htabd,dcx->thbxac", g1, jnp.asarray(onehot, F32),
                       precision=lax.Precision.HIGHEST)
    return tiles.reshape(3, rel_bias.shape[0], NA_CHUNKS, tk, qrows * GRID_W)


def _na_attention(qt, kc, vct, k, vt, bias):
    b, ntile, wtot, tq = qt.shape
    nstep = ntile // NA_STEP_TILES
    npair = wtot // LANES
    nkc, tk = vt.shape[1], vt.shape[3]
    c = kc.shape[1]
    n = k.shape[1]

    def bias_map(which):
        def index(bi, p, i):
            t = i * NA_STEP_TILES + which
            return (jnp.where(t == 0, 0, jnp.where(t == ntile - 1, 2, 1)), p, 0, 0, 0)
        return index

    return pl.pallas_call(
        _na_kernel,
        grid=(b, npair, nstep),
        in_specs=[
            pl.BlockSpec((1, NA_STEP_TILES, LANES, tq), lambda bi, p, i: (bi, i, p, 0)),
            pl.BlockSpec((1, n, LANES), lambda bi, p, i: (bi, 0, p)),
            pl.BlockSpec((1, nkc, LANES, tk), lambda bi, p, i: (bi, 0, p, 0)),
            pl.BlockSpec((1, c, LANES), lambda bi, p, i: (bi, 0, p)),
            pl.BlockSpec((1, 1, LANES, c), lambda bi, p, i: (bi, 0, p, 0)),
        ] + [pl.BlockSpec((1, 2, NA_CHUNKS, tk, tq), bias_map(w)) for w in range(NA_STEP_TILES)],
        out_specs=pl.BlockSpec((1, NA_STEP_TILES * tq, LANES), lambda bi, p, i: (bi, i, p)),
        out_shape=jax.ShapeDtypeStruct((b, ntile * tq, wtot), BF16),
        compiler_params=_cparams(("parallel", "parallel", "parallel")),
        name="na_attention",
    )(qt, k, vt, kc, vct, *([bias] * NA_STEP_TILES))


def _first_index_of_max(vals, row, big):
    mx = jnp.max(vals, axis=0, keepdims=True)
    idx = jnp.min(jnp.where(vals == mx, row, big), axis=0, keepdims=True)
    return mx, idx


def _router_gates(scores_t, sel_t):
    irow = lax.broadcasted_iota(jnp.int32, sel_t.shape, 0)
    row = irow.astype(F32)
    grp = lax.shift_right_logical(irow, int(math.log2(EXPERTS_PER_GROUP))).astype(F32)
    big = float(N_EXPERTS)
    best = None
    for g in range(MOE_GROUPS):
        vals = jnp.where(grp == float(g), sel_t, -jnp.inf)
        m1, i1 = _first_index_of_max(vals, row, big)
        m2 = jnp.max(jnp.where(row == i1, -jnp.inf, vals), axis=0, keepdims=True)
        gs = m1 + m2
        if best is None:
            best, bi = gs, jnp.zeros(gs.shape, F32)
        else:
            better = gs > best
            bi = jnp.where(better, float(g), bi)
            best = jnp.where(better, gs, best)
    msel = jnp.where(grp == bi, sel_t, -jnp.inf)
    _, i1 = _first_index_of_max(msel, row, big)
    msel2 = jnp.where(row == i1, -jnp.inf, msel)
    _, i2 = _first_index_of_max(msel2, row, big)
    w = jnp.where((row == i1) | (row == i2), scores_t, 0.0)
    return w / jnp.sum(w, axis=0, keepdims=True)


def _out_kernel(a_ref, b_ref, c_ref, d_ref, x_ref, g1_ref, sh_ref, sc_ref, gffn_ref,
                w_ref, rwh_ref, rwl_ref, rb_ref, xo_ref, h2_ref, gate_ref):
    tm = x_ref.shape[1]
    parts = OUT_SUBTILES if tm % (OUT_SUBTILES * LANES) == 0 else 1
    sub = tm // parts
    ys = []
    for h in range(parts):
        rows = slice(h * sub, (h + 1) * sub)
        y = None
        for gi, r in enumerate((a_ref, b_ref, c_ref, d_ref)):
            part = jnp.dot(r[0, rows, :], w_ref[gi * SEG_W:(gi + 1) * SEG_W, :],
                           preferred_element_type=F32)
            y = part if y is None else y + part
        ys.append(y)
    for h in range(parts):
        rows = slice(h * sub, (h + 1) * sub)
        xn = x_ref[0, rows, :] + g1_ref[0] * ys[h]
        xo_ref[0, rows, :] = xn
        ms = jnp.mean(xn * xn, axis=-1, keepdims=True)
        h2 = xn * lax.rsqrt(ms + EPS) * gffn_ref[...]
        h2 = h2 * (1.0 + sc_ref[0]) + sh_ref[0]
        hi = h2.astype(BF16)
        h2_ref[0, rows, :] = hi
        lo = (h2 - hi.astype(F32)).astype(BF16)
        logits = (jnp.dot(hi, rwh_ref[...], preferred_element_type=F32)
                  + jnp.dot(lo, rwh_ref[...], preferred_element_type=F32)
                  + jnp.dot(hi, rwl_ref[...], preferred_element_type=F32))
        scores_t = _sigmoid(logits).T[:N_EXPERTS]
        sel_t = scores_t + _tile_lanes(rb_ref[...], sub)
        gates_t = _router_gates(scores_t, sel_t)
        pad = jnp.zeros((LANES - N_EXPERTS, sub), F32)
        gate_ref[0, rows, :] = jnp.concatenate([gates_t, pad], axis=0).T


def _out_router(parts, x, g1, sh2, sc2, row_of_batch, pw, tm):
    b, n, d = x.shape
    full = lambda a: pl.BlockSpec(a.shape, lambda bi, i: (0,) * a.ndim)
    tok = lambda w: pl.BlockSpec((1, tm, w), lambda bi, i: (bi, i, 0))
    mod_spec = pl.BlockSpec((1, 1, d), lambda bi, i: (row_of_batch(bi), 0, 0))
    consts = [pw["gffn"], pw["w_out"], pw["rw_hi"], pw["rw_lo"], pw["rb"]]
    return pl.pallas_call(
        _out_kernel,
        grid=(b, n // tm),
        in_specs=[tok(SEG_W)] * 4 + [tok(d), mod_spec, mod_spec, mod_spec] + [full(a) for a in consts],
        out_specs=[tok(d), tok(d), tok(LANES)],
        out_shape=[jax.ShapeDtypeStruct((b, n, d), F32), jax.ShapeDtypeStruct((b, n, d), BF16),
                   jax.ShapeDtypeStruct((b, n, LANES), F32)],
        compiler_params=_cparams(("parallel", "parallel")),
        name="out_router",
    )(*parts, x, g1, sh2, sc2, *consts)


MOE_EXPERTS_PER_STEP = 4


def _swiglu_act(gu, scale=None):
    de = gu.shape[1] // 2
    g, u = gu[:, :de], gu[:, de:]
    act = (g * _sigmoid(g)) * u
    return (act if scale is None else act * scale).astype(BF16)


def _moe_kernel(h_ref, gate_ref, x_ref, g2_ref, wgu_ref, wd_ref, sgu_ref, sd_ref, fin_ref,
                o_ref, acc_ref, *, final_norm):
    step = pl.program_id(1)
    k = wgu_ref.shape[0]
    hb = h_ref[...]
    gates = gate_ref[...]
    lane = _lane_iota(gates.shape)
    acts = []
    for i in range(k):
        gu = jnp.dot(hb, wgu_ref[i], preferred_element_type=F32)
        col = jnp.sum(jnp.where(lane == step * k + i, gates, 0.0), axis=1, keepdims=True)
        acts.append(_swiglu_act(gu, col))
    wd = wd_ref[...]
    contrib = jnp.dot(jnp.concatenate(acts, axis=1), wd.reshape(k * wd.shape[1], wd.shape[2]),
                      preferred_element_type=F32)

    @pl.when(step == 0)
    def _():
        shared = _swiglu_act(jnp.dot(hb, sgu_ref[...], preferred_element_type=F32))
        acc_ref[...] = contrib + jnp.dot(shared, sd_ref[...], preferred_element_type=F32)

    @pl.when(step > 0)
    def _():
        acc_ref[...] += contrib

    @pl.when(step == pl.num_programs(1) - 1)
    def _():
        y = x_ref[...] + g2_ref[0] * acc_ref[...]
        if final_norm:
            ms = jnp.mean(y * y, axis=-1, keepdims=True)
            y = y * lax.rsqrt(ms + EPS) * fin_ref[...]
        o_ref[...] = y


def _moe(h2, gates, x, g2, row_of_tile, pw, tm, final_norm):
    t, d = x.shape
    ne, _, de2 = pw["wgu"].shape
    k = MOE_EXPERTS_PER_STEP
    full = lambda a: pl.BlockSpec(a.shape, lambda i, e: (0,) * a.ndim)
    return pl.pallas_call(
        functools.partial(_moe_kernel, final_norm=final_norm),
        grid=(t // tm, ne // k),
        in_specs=[
            pl.BlockSpec((tm, d), lambda i, e: (i, 0)),
            pl.BlockSpec((tm, LANES), lambda i, e: (i, 0)),
            pl.BlockSpec((tm, d), lambda i, e: (i, 0)),
            pl.BlockSpec((1, 1, d), lambda i, e: (row_of_tile(i), 0, 0)),
            pl.BlockSpec((k, d, de2), lambda i, e: (e, 0, 0)),
            pl.BlockSpec((k, de2 // 2, d), lambda i, e: (e, 0, 0)),
            full(pw["sgu"]), full(pw["sd"]), full(pw["fin"]),
        ],
        out_specs=pl.BlockSpec((tm, d), lambda i, e: (i, 0)),
        out_shape=jax.ShapeDtypeStruct((t, d), F32),
        scratch_shapes=[pltpu.VMEM((tm, d), F32)],
        compiler_params=_cparams(("parallel", "arbitrary")),
        name="moe_final" if final_norm else "moe",
    )(h2, gates, x, g2, pw["wgu"], pw["wd"], pw["sgu"], pw["sd"], pw["fin"])


def _axial_angles(n_tok, rot_dim):
    half = rot_dim // 2
    freqs = ROPE_BASE ** (-jnp.arange(0, half, 2, dtype=F32) / half)
    t = jnp.arange(n_tok, dtype=jnp.int32)
    row = (t // GRID_W).astype(F32)
    col = (t % GRID_W).astype(F32)
    return row[:, None] * freqs, col[:, None] * freqs


def _rope_table(n_tok, rot_dim):
    ar, ac = _axial_angles(n_tok, rot_dim)
    cos = jnp.concatenate([jnp.cos(ar)] * 2 + [jnp.cos(ac)] * 2, axis=1)
    sin = jnp.concatenate([-jnp.sin(ar), jnp.sin(ar), -jnp.sin(ac), jnp.sin(ac)], axis=1)
    return cos, sin


def _rope_tables(n_tok):
    cg, sg = _rope_table(n_tok, GQA_DIM)
    cd, sd = _rope_table(n_tok, DIFF_QK_DIM)
    cm32, sm32 = _rope_table(n_tok, MLA_ROPE)
    ones = jnp.ones((n_tok, MLA_NOPE), F32)
    zeros = jnp.zeros((n_tok, MLA_NOPE), F32)
    pad = LANES - MLA_NOPE - MLA_ROPE
    cm = jnp.concatenate([ones, cm32, ones[:, :pad]], axis=1)
    sm = jnp.concatenate([zeros, sm32, zeros[:, :pad]], axis=1)
    return (jnp.tile(cg, (1, LANES // GQA_DIM)), jnp.tile(sg, (1, LANES // GQA_DIM)),
            jnp.tile(cd, (1, LANES // DIFF_QK_DIM)), jnp.tile(sd, (1, LANES // DIFF_QK_DIM)),
            cm, sm)


def _pack_layer(l, p):
    d = p["w_in"].shape[1]
    w = p["w_in"][l]
    o = np.cumsum([0, 256, 128, 128, 256, 256, 256, 256, 256, 256, MLA_Q_LORA, MLA_KV_LORA, MLA_ROPE])
    col = lambda i: w[:, int(o[i]):int(o[i + 1])]
    dup = lambda a: jnp.concatenate([a[:, :GQA_DIM], a[:, :GQA_DIM], a[:, GQA_DIM:], a[:, GQA_DIM:]], axis=1)
    z = lambda n: jnp.zeros((d, n), F32)
    w_in = jnp.concatenate(
        [col(0), dup(col(1)), dup(col(2)), col(3), col(4), col(5), col(6), col(7), col(8),
         col(9), z(SEG_W - MLA_Q_LORA), col(10),
         z(MLA_NOPE), col(11), z(LANES - MLA_NOPE - MLA_ROPE)], axis=1).astype(BF16)
    dq = MLA_NOPE + MLA_ROPE
    wqu = p["mla_w_q_up"][l].reshape(MLA_Q_LORA, MLA_HEADS, dq)
    wqu = jnp.pad(wqu, ((0, SEG_W - MLA_Q_LORA), (0, 0), (0, LANES - dq)))
    wkv = p["mla_w_kv_up"][l].reshape(MLA_KV_LORA, MLA_HEADS, MLA_NOPE + MLA_V)
    wkk = jnp.pad(wkv[:, :, :MLA_NOPE], ((0, 0), (0, 0), (0, LANES - MLA_NOPE)))
    e = p["moe_w_gate"].shape[1]
    rw = jnp.pad(p["router_w"], ((0, 0), (0, LANES - e)))
    rw_hi = rw.astype(BF16)
    rw_lo = (rw - rw_hi.astype(F32)).astype(BF16)
    return {
        "gmix": p["norm_mix"][l][None], "gffn": p["norm_ffn"][l][None],
        "w_in": w_in,
        "wq_up": wqu.reshape(SEG_W, MLA_HEADS * LANES).astype(BF16),
        "wkv_k": wkk.reshape(MLA_KV_LORA, MLA_HEADS * LANES).astype(BF16),
        "wkv_v": wkv[:, :, MLA_NOPE:].reshape(MLA_KV_LORA, MLA_HEADS * MLA_V).astype(BF16),
        "gq": (jnp.tile(p["gqa_q_norm"][l], GQA_HEADS) * (GQA_DIM ** -0.5 * LOG2E))[None],
        "gk": jnp.tile(p["gqa_k_norm"][l], GQA_HEADS)[None],
        "gcq": jnp.pad(p["mla_q_norm"][l], (0, SEG_W - MLA_Q_LORA))[None],
        "gckv": p["mla_kv_norm"][l][None],
        "w_out": p["w_out"][l].astype(BF16),
        "rw_hi": rw_hi, "rw_lo": rw_lo,
        "rb": jnp.broadcast_to(p["router_b"][:, None], (e, LANES)),
        "wgu": jnp.concatenate([p["moe_w_gate"][l], p["moe_w_up"][l]], axis=2).astype(BF16),
        "wd": p["moe_w_down"][l].astype(BF16),
        "sgu": jnp.concatenate([p["shared_w_gate"][l], p["shared_w_up"][l]], axis=1).astype(BF16),
        "sd": p["shared_w_down"][l].astype(BF16),
        "fin": p["final_norm"][None],
        "diff": (p["diff_lq1"][l][None], p["diff_lk1"][l][None], p["diff_lq2"][l][None],
                 p["diff_lk2"][l][None], p["diff_subln"][l][:, None]),
    }


def kernel(x, c, ctx, c_ctx, w_mod, b_mod, norm_mix, norm_ffn, w_in, w_out, gqa_q_norm, gqa_k_norm,
           na_rel_bias, diff_lq1, diff_lk1, diff_lq2, diff_lk2, diff_subln, mla_q_norm, mla_w_q_up,
           mla_kv_norm, mla_w_kv_up, router_w, router_b, moe_w_gate, moe_w_up, moe_w_down,
           shared_w_gate, shared_w_up, shared_w_down, final_norm):
    p = dict(w_in=w_in, norm_mix=norm_mix, norm_ffn=norm_ffn, w_out=w_out, gqa_q_norm=gqa_q_norm,
             gqa_k_norm=gqa_k_norm, diff_lq1=diff_lq1, diff_lk1=diff_lk1, diff_lq2=diff_lq2,
             diff_lk2=diff_lk2, diff_subln=diff_subln, mla_q_norm=mla_q_norm, mla_w_q_up=mla_w_q_up,
             mla_kv_norm=mla_kv_norm, mla_w_kv_up=mla_w_kv_up, router_w=router_w, router_b=router_b,
             moe_w_gate=moe_w_gate, moe_w_up=moe_w_up, moe_w_down=moe_w_down,
             shared_w_gate=shared_w_gate, shared_w_up=shared_w_up, shared_w_down=shared_w_down,
             final_norm=final_norm)
    b, s, d = x.shape
    nctx = ctx.shape[1]
    depth = w_mod.shape[0]
    rows = s // GRID_W
    assert s % TOK_TILE == 0 and s // NA_TILE >= NA_CHUNKS + NA_STEP_TILES
    assert nctx % LANES == 0 and (b * s) % MOE_TILE == 0 and s % MOE_TILE == 0

    mrows = -(-(b + 1) // SUBLANES) * SUBLANES
    cc = jnp.zeros((mrows, d), F32).at[:b].set(c).at[b].set(c_ctx)
    mod = _modulation(cc, w_mod, b_mod)
    tables = _rope_tables(s)
    lat_row = lambda bi: bi
    ctx_row = lambda bi: b
    tm_ctx = b * nctx if b * nctx <= MOE_TILE else nctx

    xc = ctx
    for l in range(depth):
        want_ctx = l < depth - 1
        lam_init = 0.8 - 0.6 * math.exp(-0.3 * l)
        pw = _pack_layer(l, p)
        m6 = mod[l].reshape(mrows, 6, 1, d)
        sh1, sc1, g1, sh2, sc2, g2 = (m6[:, k] for k in range(6))

        lat = _project(x, sh1, sc1, lat_row, pw, tables, TOK_TILE, LAT_OUTS)
        cx = _project(xc, sh1, sc1, ctx_row, pw, None, nctx, CTX_OUTS if want_ctx else CTX_KV_OUTS)

        def dense(g, kind, extras=()):
            return _flash_lat(lat[g + ".qT"], cx[g + ".k"], cx[g + ".vT"], lat[g + ".k"], lat[g + ".vT"],
                              kind, extras, lam_init)

        def ctx_only(g, kind, extras=()):
            return _flash_ctx(cx[g + ".qT"], cx[g + ".k"], cx[g + ".vT"], kind, extras, lam_init)

        a_lat = dense("A", "pair")
        bias = _na_bias_tiles(na_rel_bias[l], rows, NA_TILE)
        b_lat = _na_attention(lat["B.qT"], cx["B.k"], cx["B.vT"], lat["B.k"], lat["B.vT"], bias)
        c_lat = dense("C", "diff", pw["diff"])
        d_lat = dense("D", "mla")
        x1, h2, gates = _out_router((a_lat, b_lat, c_lat, d_lat), x, g1, sh2, sc2, lat_row, pw, TOK_TILE)
        tiles_per_batch = s // MOE_TILE
        x = _moe(h2.reshape(b * s, d), gates.reshape(b * s, LANES), x1.reshape(b * s, d), g2,
                 lambda i: i // tiles_per_batch, pw, MOE_TILE, not want_ctx).reshape(b, s, d)

        if want_ctx:
            a_c = ctx_only("A", "pair")
            b_c = ctx_only("B", "pair")
            c_c = ctx_only("C", "diff", pw["diff"])
            d_c = ctx_only("D", "mla")
            xc1, h2c, gates_c = _out_router((a_c, b_c, c_c, d_c), xc, g1, sh2, sc2, ctx_row, pw, nctx)
            xc = _moe(h2c.reshape(b * nctx, d), gates_c.reshape(b * nctx, LANES),
                      xc1.reshape(b * nctx, d), g2, lambda i: b, pw, tm_ctx, False).reshape(b, nctx, d)
    return x
```
